```python
import jax
import jax.numpy as jnp
from jax import lax
import numpy as np

D_MODEL = 2048
BATCH = 16
SEQ = 2048
DEPTH = 4

CTX_LEN = 256
GRID_W = 64
HEAD_DIM = 128
WIN_HEADS = 8
WIN_KV_HEADS = 2
WINDOW = 128
BLOCK = 128
MLA_HEADS = 8
MLA_NOPE = 128
MLA_ROPE = 64
MLA_V = 128
MLA_Q_RANK = 512
MLA_KV_RANK = 256
GLB_HEADS = 8
GLB_KV_HEADS = 2
BRANCH_W = 1024
N_BRANCH = 3
FFN_DIM = 5632
N_EXPERTS = 8
TOP_K = 2
EXPERT_DIM = 4096
ROPE_THETA = 10000.0
EPS = 1e-6
NEG_INF = -1e30
N_DENSE = (DEPTH + 1) // 2
N_MOE = DEPTH // 2
IN_SIZES = (WIN_HEADS * HEAD_DIM, WIN_KV_HEADS * HEAD_DIM, WIN_KV_HEADS * HEAD_DIM,
            MLA_Q_RANK, MLA_KV_RANK, MLA_ROPE,
            GLB_HEADS * HEAD_DIM, GLB_KV_HEADS * HEAD_DIM, GLB_KV_HEADS * HEAD_DIM,
            N_BRANCH * D_MODEL)
IN_COLS = sum(IN_SIZES)
IN_SPLITS = tuple(int(s) for s in np.cumsum(IN_SIZES)[:-1])

kernel_name = 'hybrid_diffusion_block'


def rms_norm(x, gain=None):
    xf = x.astype(jnp.float32)
    y = xf * lax.rsqrt(jnp.mean(xf * xf, axis=-1, keepdims=True) + EPS)
    if gain is not None:
        y = y * gain.astype(jnp.float32)
    return y.astype(x.dtype)


def modulate(h, shift, scale):
    return h * (1 + scale) + shift


def axial_rope_tables(n_tokens, dim, dtype):
    rows = n_tokens // GRID_W
    row = jnp.repeat(jnp.arange(rows), GRID_W).astype(jnp.float32)
    col = jnp.tile(jnp.arange(GRID_W), rows).astype(jnp.float32)
    quarter = dim // 4
    freqs = ROPE_THETA ** (-jnp.arange(quarter, dtype=jnp.float32) / quarter)
    ang_r = row[:, None] * freqs
    ang_c = col[:, None] * freqs
    ang = jnp.concatenate([ang_r, ang_r, ang_c, ang_c], axis=-1)
    return (jnp.cos(ang).astype(dtype), jnp.sin(ang).astype(dtype))


def rotate_axial(x):
    a, b, c, d = jnp.split(x, 4, axis=-1)
    return jnp.concatenate([-b, a, -d, c], axis=-1)


def apply_rope(x, cos, sin):
    shape = (1, cos.shape[0]) + (1,) * (x.ndim - 3) + (cos.shape[1],)
    return x * cos.reshape(shape) + rotate_axial(x) * sin.reshape(shape)


def joint_softmax(s_a, s_b, sink=None):
    m = jnp.maximum(jnp.max(s_a, -1, keepdims=True), jnp.max(s_b, -1, keepdims=True))
    if sink is not None:
        m = jnp.maximum(m, sink)
    e_a = jnp.exp(s_a - m)
    e_b = jnp.exp(s_b - m)
    den = jnp.sum(e_a, -1, keepdims=True) + jnp.sum(e_b, -1, keepdims=True)
    if sink is not None:
        den = den + jnp.exp(sink - m)
    return e_a / den, e_b / den


def context_attention(q, k, v, sink=None):
    scale = q.shape[-1] ** -0.5
    s = jnp.einsum('bqhgd,bkhd->bhgqk', q, k).astype(jnp.float32) * scale
    if sink is None:
        p = jax.nn.softmax(s, axis=-1)
    else:
        m = jnp.maximum(jnp.max(s, -1, keepdims=True), sink)
        e = jnp.exp(s - m)
        p = e / (jnp.sum(e, -1, keepdims=True) + jnp.exp(sink - m))
    o = jnp.einsum('bhgqk,bkhe->bqhge', p.astype(v.dtype), v)
    return o.reshape(o.shape[0], o.shape[1], -1)


def window_attention(q, k, v, k_ctx, v_ctx, sink):
    b, s, hkv, g, d = q.shape
    nb = s // BLOCK
    scale = d ** -0.5
    qb = q.reshape(b, nb, BLOCK, hkv, g, d)

    def band(t):
        tb = t.reshape((b, nb, BLOCK) + t.shape[2:])
        tp = jnp.pad(tb, ((0, 0), (1, 1)) + ((0, 0),) * (tb.ndim - 2))
        return jnp.concatenate([tp[:, :-2], tp[:, 1:-1], tp[:, 2:]], axis=2)

    kw, vw = band(k), band(v)
    blk = jnp.arange(nb)[:, None]
    q_pos = blk * BLOCK + jnp.arange(BLOCK)[None, :]
    k_pos = (blk - 1) * BLOCK + jnp.arange(3 * BLOCK)[None, :]
    kp = k_pos[:, None, :]
    valid = (jnp.abs(q_pos[:, :, None] - kp) <= WINDOW) & (kp >= 0) & (kp < s)
    s_loc = jnp.einsum('bnqhgd,bnkhd->bnhgqk', qb, kw).astype(jnp.float32) * scale
    s_loc = jnp.where(valid[None, :, None, None], s_loc, NEG_INF)
    s_ctx = jnp.einsum('bnqhgd,bchd->bnhgqc', qb, k_ctx).astype(jnp.float32) * scale
    p_loc, p_ctx = joint_softmax(s_loc, s_ctx, sink)
    o = (jnp.einsum('bnhgqk,bnkhe->bnqhge', p_loc.astype(v.dtype), vw)
         + jnp.einsum('bnhgqc,bche->bnqhge', p_ctx.astype(v.dtype), v_ctx))
    return o.reshape(b, s, -1)


def global_attention(q, k, v, k_ctx, v_ctx):
    b, s, hkv, g, d = q.shape
    nb = s // BLOCK
    scale = d ** -0.5
    qb = jnp.moveaxis(q.reshape(b, nb, BLOCK, hkv, g, d), 1, 0)

    def one_block(qi):
        s_lat = jnp.einsum('bqhgd,bkhd->bhgqk', qi, k).astype(jnp.float32) * scale
        s_ctx = jnp.einsum('bqhgd,bchd->bhgqc', qi, k_ctx).astype(jnp.float32) * scale
        p_lat, p_ctx = joint_softmax(s_lat, s_ctx)
        return (jnp.einsum('bhgqk,bkhe->bqhge', p_lat.astype(v.dtype), v)
                + jnp.einsum('bhgqc,bche->bqhge', p_ctx.astype(v.dtype), v_ctx))

    o = lax.map(one_block, qb)
    return jnp.moveaxis(o, 0, 1).reshape(b, s, -1)


def project_heads(h, w_in_l, win_qn, win_kn, mla_qln, mla_kvln, mla_wq, mla_wkv,
                  mla_qn, mla_kn, glb_qn, glb_kn, rope_h, rope_r):
    b, t, _ = h.shape
    wq, wk, wv, mq, mkv, mkr, gq, gk, gv, gates = jnp.split(h @ w_in_l, IN_SPLITS, axis=-1)
    qa = rms_norm(wq.reshape(b, t, WIN_KV_HEADS, WIN_HEADS // WIN_KV_HEADS, HEAD_DIM), win_qn)
    ka = rms_norm(wk.reshape(b, t, WIN_KV_HEADS, HEAD_DIM), win_kn)
    va = wv.reshape(b, t, WIN_KV_HEADS, HEAD_DIM)
    qb = (rms_norm(mq, mla_qln) @ mla_wq).reshape(b, t, MLA_HEADS, 1, MLA_NOPE + MLA_ROPE)
    kvb = (rms_norm(mkv, mla_kvln) @ mla_wkv).reshape(b, t, MLA_HEADS, MLA_NOPE + MLA_V)
    k_nope, vb = jnp.split(kvb, [MLA_NOPE], axis=-1)
    k_rope = jnp.broadcast_to(mkr[:, :, None, :], (b, t, MLA_HEADS, MLA_ROPE))
    qb = rms_norm(qb, mla_qn)
    kb = rms_norm(jnp.concatenate([k_nope, k_rope], axis=-1), mla_kn)
    qc = rms_norm(gq.reshape(b, t, GLB_KV_HEADS, GLB_HEADS // GLB_KV_HEADS, HEAD_DIM), glb_qn)
    kc = rms_norm(gk.reshape(b, t, GLB_KV_HEADS, HEAD_DIM), glb_kn)
    vc = gv.reshape(b, t, GLB_KV_HEADS, HEAD_DIM)
    if rope_h is not None:
        cos_h, sin_h = rope_h
        cos_r, sin_r = rope_r
        qa = apply_rope(qa, cos_h, sin_h)
        ka = apply_rope(ka, cos_h, sin_h)
        qc = apply_rope(qc, cos_h, sin_h)
        kc = apply_rope(kc, cos_h, sin_h)
        qb = jnp.concatenate([qb[..., :MLA_NOPE], apply_rope(qb[..., MLA_NOPE:], cos_r, sin_r)], axis=-1)
        kb = jnp.concatenate([kb[..., :MLA_NOPE], apply_rope(kb[..., MLA_NOPE:], cos_r, sin_r)], axis=-1)
    return (qa, ka, va, qb, kb, vb, qc, kc, vc, jax.nn.sigmoid(gates))


def merge_branches(o_win, o_mla, o_glb, gates, w_branch_l, w_out_l):
    g_win, g_mla, g_glb = jnp.split(gates, N_BRANCH, axis=-1)
    y = (g_win * (o_win @ w_branch_l[0]) + g_mla * (o_mla @ w_branch_l[1])
         + g_glb * (o_glb @ w_branch_l[2]))
    return y @ w_out_l


def swiglu(h, w_gate_up, w_down):
    g, u = jnp.split(h @ w_gate_up, 2, axis=-1)
    return (jax.nn.silu(g) * u) @ w_down


def moe_swiglu(h, router, w_gate_up, w_down):
    logits = (h @ router).astype(jnp.float32)
    top_val, top_idx = lax.top_k(logits, TOP_K)
    top_w = jax.nn.softmax(top_val, axis=-1)
    gate = jnp.sum(jax.nn.one_hot(top_idx, N_EXPERTS, dtype=jnp.float32) * top_w[..., None], axis=-2)
    gate = gate.astype(h.dtype)
    out = jnp.zeros_like(h)
    for e in range(N_EXPERTS):
        out = out + gate[..., e:e + 1] * swiglu(h, w_gate_up[e], w_down[e])
    return out


def channel_mix(h, layer, ffn_w_gate_up, ffn_w_down, moe_router, moe_w_gate_up, moe_w_down):
    i = layer // 2
    if layer % 2 == 0:
        return swiglu(h, ffn_w_gate_up[i], ffn_w_down[i])
    return moe_swiglu(h, moe_router[i], moe_w_gate_up[i], moe_w_down[i])


def setup_inputs(seed: int = 0) -> dict:
    key = jax.random.key(seed)
    ks = jax.random.split(key, 25)
    f32 = jnp.float32
    d = D_MODEL

    def nrm(k, shape, scale):
        return jax.random.normal(k, shape, f32) * scale

    def gain(k, shape):
        return 1.0 + 0.05 * jax.random.normal(k, shape, f32)

    return {
        'x': nrm(ks[0], (BATCH, SEQ, d), 1.0),
        'c': nrm(ks[1], (BATCH, d), 1.0),
        'ctx': nrm(ks[2], (BATCH, CTX_LEN, d), 1.0),
        'c_ctx': nrm(ks[3], (d,), 1.0),
        'w_mod': nrm(ks[4], (DEPTH, d, 6 * d), d ** -0.5),
        'b_mod': nrm(ks[5], (DEPTH, 6 * d), 0.02),
        'w_in': nrm(ks[6], (DEPTH, d, IN_COLS), d ** -0.5),
        'win_q_norm': gain(ks[7], (DEPTH, HEAD_DIM)),
        'win_k_norm': gain(ks[8], (DEPTH, HEAD_DIM)),
        'win_sink': nrm(ks[9], (DEPTH, WIN_HEADS), 0.5),
        'mla_q_lat_norm': gain(ks[10], (DEPTH, MLA_Q_RANK)),
        'mla_kv_lat_norm': gain(ks[11], (DEPTH, MLA_KV_RANK)),
        'mla_w_q_up': nrm(ks[12], (DEPTH, MLA_Q_RANK, MLA_HEADS * (MLA_NOPE + MLA_ROPE)), MLA_Q_RANK ** -0.5),
        'mla_w_kv_up': nrm(ks[13], (DEPTH, MLA_KV_RANK, MLA_HEADS * (MLA_NOPE + MLA_V)), MLA_KV_RANK ** -0.5),
        'mla_q_norm': gain(ks[14], (DEPTH, MLA_NOPE + MLA_ROPE)),
        'mla_k_norm': gain(ks[15], (DEPTH, MLA_NOPE + MLA_ROPE)),
        'glb_q_norm': gain(ks[16], (DEPTH, HEAD_DIM)),
        'glb_k_norm': gain(ks[17], (DEPTH, HEAD_DIM)),
        'w_branch': nrm(ks[18], (DEPTH, N_BRANCH, BRANCH_W, d), BRANCH_W ** -0.5),
        'w_out': nrm(ks[19], (DEPTH, d, d), d ** -0.5),
        'ffn_w_gate_up': nrm(ks[20], (N_DENSE, d, 2 * FFN_DIM), d ** -0.5),
        'ffn_w_down': nrm(ks[21], (N_DENSE, FFN_DIM, d), FFN_DIM ** -0.5),
        'moe_router': nrm(ks[22], (N_MOE, d, N_EXPERTS), d ** -0.5),
        'moe_w_gate_up': nrm(ks[23], (N_MOE, N_EXPERTS, d, 2 * EXPERT_DIM), d ** -0.5),
        'moe_w_down': nrm(ks[24], (N_MOE, N_EXPERTS, EXPERT_DIM, d), EXPERT_DIM ** -0.5),
    }


def reference(x, c, ctx, c_ctx, w_mod, b_mod, w_in, win_q_norm, win_k_norm, win_sink,
              mla_q_lat_norm, mla_kv_lat_norm, mla_w_q_up, mla_w_kv_up, mla_q_norm, mla_k_norm,
              glb_q_norm, glb_k_norm, w_branch, w_out, ffn_w_gate_up, ffn_w_down,
              moe_router, moe_w_gate_up, moe_w_down):
    n_lat = x.shape[1]
    rope_h = axial_rope_tables(n_lat, HEAD_DIM, x.dtype)
    rope_r = axial_rope_tables(n_lat, MLA_ROPE, x.dtype)
    silu_c = jax.nn.silu(c)[:, None, :]
    silu_cc = jax.nn.silu(c_ctx)[None, None, :]
    xl, xc = x, ctx
    for layer in range(DEPTH):
        last = layer == DEPTH - 1
        mod_l = jnp.split(silu_c @ w_mod[layer] + b_mod[layer], 6, axis=-1)
        mod_c = jnp.split(silu_cc @ w_mod[layer] + b_mod[layer], 6, axis=-1)
        head_w = (w_in[layer], win_q_norm[layer], win_k_norm[layer], mla_q_lat_norm[layer],
                  mla_kv_lat_norm[layer], mla_w_q_up[layer], mla_w_kv_up[layer], mla_q_norm[layer],
                  mla_k_norm[layer], glb_q_norm[layer], glb_k_norm[layer])
        sink = win_sink[layer].astype(jnp.float32).reshape(WIN_KV_HEADS, WIN_HEADS // WIN_KV_HEADS, 1, 1)
        hl = modulate(rms_norm(xl), mod_l[0], mod_l[1])
        hc = modulate(rms_norm(xc), mod_c[0], mod_c[1])
        qa, ka, va, qb, kb, vb, qc, kc, vc, gl = project_heads(hl, *head_w, rope_h, rope_r)
        qa_c, ka_c, va_c, qb_c, kb_c, vb_c, qc_c, kc_c, vc_c, gc = project_heads(hc, *head_w, None, None)
        o_win = window_attention(qa, ka, va, ka_c, va_c, sink)
        o_mla = global_attention(qb, kb, vb, kb_c, vb_c)
        o_glb = global_attention(qc, kc, vc, kc_c, vc_c)
        xl = xl + mod_l[2] * merge_branches(o_win, o_mla, o_glb, gl, w_branch[layer], w_out[layer])
        if not last:
            o_win_c = context_attention(qa_c, ka_c, va_c, sink)
            o_mla_c = context_attention(qb_c, kb_c, vb_c)
            o_glb_c = context_attention(qc_c, kc_c, vc_c)
            xc = xc + mod_c[2] * merge_branches(o_win_c, o_mla_c, o_glb_c, gc, w_branch[layer], w_out[layer])
        hl2 = modulate(rms_norm(xl), mod_l[3], mod_l[4])
        xl = xl + mod_l[5] * channel_mix(hl2, layer, ffn_w_gate_up, ffn_w_down, moe_router, moe_w_gate_up, moe_w_down)
        if not last:
            hc2 = modulate(rms_norm(xc), mod_c[3], mod_c[4])
            xc = xc + mod_c[5] * channel_mix(hc2, layer, ffn_w_gate_up, ffn_w_down, moe_router, moe_w_gate_up, moe_w_down)
    return xl
```

```python
import functools

import jax
import jax.numpy as jnp
import numpy as np
from jax import lax
from jax.experimental import pallas as pl
from jax.experimental.pallas import tpu as pltpu

CTX_LEN = 256
GRID_W = 64
HEAD_DIM = 128
WIN_HEADS = 8
WIN_KV_HEADS = 2
WINDOW = 128
BLOCK = 128
MLA_HEADS = 8
MLA_NOPE = 128
MLA_ROPE = 64
MLA_V = 128
MLA_Q_RANK = 512
MLA_KV_RANK = 256
GLB_HEADS = 8
GLB_KV_HEADS = 2
BRANCH_W = 1024
N_BRANCH = 3
N_EXPERTS = 8
ROPE_THETA = 10000.0
EPS = 1e-6
NEG_INF = -1e30

LANES = 128
MLA_HEAD_PAD = 256
MOD_ROWS = 24
VMEM_LIMIT = 60 * 1024 * 1024

C_WQ = 0
C_WK = C_WQ + WIN_HEADS * HEAD_DIM
C_WV = C_WK + WIN_KV_HEADS * HEAD_DIM
C_GQ = C_WV + WIN_KV_HEADS * HEAD_DIM
C_GK = C_GQ + GLB_HEADS * HEAD_DIM
C_GV = C_GK + GLB_KV_HEADS * HEAD_DIM
C_MQ = C_GV + GLB_KV_HEADS * HEAD_DIM
C_MKV = C_MQ + MLA_Q_RANK
C_MKR = C_MKV + MLA_KV_RANK
C_GATE = 4096

BF16 = jnp.bfloat16
F32 = jnp.float32


def _largest_divisor(n, candidates):
    for cand in candidates:
        if n % cand == 0:
            return cand
    raise ValueError(f"no tile in {candidates} divides {n}")


def _params(n_axes):
    return pltpu.CompilerParams(dimension_semantics=("arbitrary",) * n_axes,
                                vmem_limit_bytes=VMEM_LIMIT)


def _rms(x, width):
    ms = jnp.sum(x * x, axis=-1, keepdims=True) * (1.0 / width)
    return x * lax.rsqrt(ms + EPS)


def _rope(y, cos, sin_a, sin_b, quarter):
    return (y * cos + pltpu.roll(y, LANES - quarter, 1) * sin_a + pltpu.roll(y, quarter, 1) * sin_b)


def _norm_modulate(x_ref, sh_ref, sc_ref, h_ref, batch, row0, n_rows):
    chunk = 16

    def body(ci, carry):
        r0 = pl.multiple_of(ci * chunk, chunk)
        xf = x_ref[pl.ds(r0, chunk), :]
        y = _rms(xf, xf.shape[-1])
        row = jnp.where(row0 + r0 < CTX_LEN, MOD_ROWS - 8, batch)
        sh = sh_ref[pl.ds(row, 1), :]
        sc = sc_ref[pl.ds(row, 1), :]
        h_ref[pl.ds(r0, chunk), :] = (y * (1.0 + sc) + sh).astype(h_ref.dtype)
        return carry

    lax.fori_loop(0, n_rows // chunk, body, 0)


def _gate_rows(g_ref, batch, row0, n_rows):
    rows = lax.broadcasted_iota(jnp.int32, (n_rows, 1), 0) + row0
    g_lat = g_ref[pl.ds(batch, 1), :]
    g_ctx = g_ref[pl.ds(MOD_ROWS - 8, 1), :]
    return jnp.where(rows < CTX_LEN, g_ctx, g_lat)


def _mod_kernel(c_ref, w_ref, b_ref, o_ref):
    cv = c_ref[...]
    sc = (cv * jax.nn.sigmoid(cv)).astype(BF16)
    o_ref[...] = jnp.dot(sc, w_ref[...].astype(BF16), preferred_element_type=F32) + b_ref[...]


def _mod_call(cvec, w_mod, b_mod):
    depth, d, n = w_mod.shape
    tn = _largest_divisor(n, (1024, 512, 256, 128))
    return pl.pallas_call(
        _mod_kernel,
        grid=(depth, n // tn),
        in_specs=[pl.BlockSpec((MOD_ROWS, d), lambda l, j: (0, 0)),
                  pl.BlockSpec((None, d, tn), lambda l, j: (l, 0, j)),
                  pl.BlockSpec((None, 1, tn), lambda l, j: (l, 0, j))],
        out_specs=pl.BlockSpec((None, MOD_ROWS, tn), lambda l, j: (l, 0, j)),
        out_shape=jax.ShapeDtypeStruct((depth, MOD_ROWS, n), F32),
        compiler_params=_params(2),
        name="mod_vectors",
    )(cvec, w_mod, b_mod.reshape(depth, 1, n))


def _inproj_kernel(x_ref, sh_ref, sc_ref, w_ref, o_ref, h_ref, *, tiles_per_batch):
    i = pl.program_id(0)
    tm = x_ref.shape[0]

    @pl.when(pl.program_id(1) == 0)
    def _():
        _norm_modulate(x_ref, sh_ref, sc_ref, h_ref, i // tiles_per_batch,
                       (i % tiles_per_batch) * tm, tm)

    o_ref[...] = jnp.dot(h_ref[...], w_ref[...], preferred_element_type=F32).astype(o_ref.dtype)


def _inproj_call(xs, mod, w_in_p, layer, tpb):
    t, d = xs.shape
    n = w_in_p.shape[-1]
    tm = _largest_divisor(tpb, (1152, 768, 384, 256, 128))
    tn = _largest_divisor(n, (1024, 512, 256, 128))
    kern = functools.partial(_inproj_kernel, tiles_per_batch=tpb // tm)
    return pl.pallas_call(
        kern,
        grid=(t // tm, n // tn),
        in_specs=[pl.BlockSpec((tm, d), lambda i, j: (i, 0)),
                  pl.BlockSpec((None, MOD_ROWS, d), lambda i, j: (layer, 0, 0)),
                  pl.BlockSpec((None, MOD_ROWS, d), lambda i, j: (layer, 0, 1)),
                  pl.BlockSpec((None, d, tn), lambda i, j: (layer, 0, j))],
        out_specs=pl.BlockSpec((tm, tn), lambda i, j: (i, j)),
        out_shape=jax.ShapeDtypeStruct((t, n), BF16),
        scratch_shapes=[pltpu.VMEM((tm, d), BF16)],
        compiler_params=_params(2),
        name="in_projection",
    )(xs, mod, mod, w_in_p)


def _kprep_kernel(wk_ref, gk_ref, wg_ref, gg_ref, cos_ref, sa_ref, sb_ref, ka_ref, kc_ref):
    cos, sa, sb = cos_ref[...], sa_ref[...], sb_ref[...]
    for src, gain, dst in ((wk_ref, wg_ref, ka_ref), (gk_ref, gg_ref, kc_ref)):
        for hd in range(src.shape[1] // HEAD_DIM):
            cols = slice(hd * HEAD_DIM, (hd + 1) * HEAD_DIM)
            y = _rms(src[:, cols].astype(F32), HEAD_DIM) * gain[...]
            dst[:, cols] = _rope(y, cos, sa, sb, HEAD_DIM // 4).astype(dst.dtype)


def _kprep_call(proj, win_k_gain, glb_k_gain, rope_h, tpb):
    t = proj.shape[0]
    tm = _largest_divisor(tpb, (768, 384, 256, 128))
    per_batch = tpb // tm
    kw = WIN_KV_HEADS * HEAD_DIM
    row_spec = lambda col: pl.BlockSpec((tm, kw), lambda i: (i, col))
    gain_spec = pl.BlockSpec((1, HEAD_DIM), lambda i: (0, 0))
    rope_spec = pl.BlockSpec((tm, LANES), lambda i: (i % per_batch, 0))
    out = jax.ShapeDtypeStruct((t, kw), BF16)
    return pl.pallas_call(
        _kprep_kernel,
        grid=(t // tm,),
        in_specs=[row_spec(C_WK // kw), row_spec(C_GK // kw), gain_spec, gain_spec,
                  rope_spec, rope_spec, rope_spec],
        out_specs=[pl.BlockSpec((tm, kw), lambda i: (i, 0))] * 2,
        out_shape=[out, out],
        compiler_params=_params(1),
        name="gqa_key_prep",
    )(proj, proj, win_k_gain, glb_k_gain, *rope_h)


def _mla_prep_kernel(mq_ref, mkv_ref, mkr_ref, wq_ref, wkv_ref, qln_ref, kvln_ref, qn_ref, kn_ref,
                     cos_ref, sa_ref, sb_ref, q_out, k_out, v_out):
    width = MLA_NOPE + MLA_ROPE
    quarter = MLA_ROPE // 4
    cos, sa, sb = cos_ref[...], sa_ref[...], sb_ref[...]
    q_lat = (_rms(mq_ref[...].astype(F32), MLA_Q_RANK) * qln_ref[...]).astype(BF16)
    kv_lat = (_rms(mkv_ref[...].astype(F32), MLA_KV_RANK) * kvln_ref[...]).astype(BF16)
    qu = jnp.dot(q_lat, wq_ref[...], preferred_element_type=F32)
    kvu = jnp.dot(kv_lat, wkv_ref[...], preferred_element_type=F32)
    kr = mkr_ref[...].astype(F32)
    kr_sq = jnp.sum(kr * kr, axis=-1, keepdims=True)
    q_scale = width ** -0.5
    for hd in range(MLA_HEADS):
        lo = hd * MLA_HEAD_PAD
        qa = qu[:, lo:lo + LANES]
        qb = qu[:, lo + LANES:lo + 2 * LANES]
        ms = (jnp.sum(qa * qa, axis=-1, keepdims=True) + jnp.sum(qb * qb, axis=-1, keepdims=True)) / width
        rstd = lax.rsqrt(ms + EPS) * q_scale
        q_out[:, lo:lo + LANES] = (qa * rstd * qn_ref[:, :LANES]).astype(q_out.dtype)
        q_out[:, lo + LANES:lo + 2 * LANES] = _rope(qb * rstd * qn_ref[:, LANES:], cos, sa, sb,
                                                     quarter).astype(q_out.dtype)
        kn = kvu[:, lo:lo + MLA_NOPE]
        ms = (jnp.sum(kn * kn, axis=-1, keepdims=True) + kr_sq) / width
        rstd = lax.rsqrt(ms + EPS)
        k_out[:, lo:lo + LANES] = (kn * rstd * kn_ref[:, :LANES]).astype(k_out.dtype)
        k_out[:, lo + LANES:lo + 2 * LANES] = _rope(kr * rstd * kn_ref[:, LANES:], cos, sa, sb,
                                                     quarter).astype(k_out.dtype)
        v_out[:, hd * MLA_V:(hd + 1) * MLA_V] = kvu[:, lo + MLA_NOPE:lo + MLA_NOPE + MLA_V].astype(v_out.dtype)


def _mla_prep_call(proj, wq_p, wkv, qln, kvln, qn_p, kn_p, rope_r, layer, tpb):
    t = proj.shape[0]
    tm = _largest_divisor(tpb, (768, 384, 256, 128))
    per_batch = tpb // tm
    hp = MLA_HEADS * MLA_HEAD_PAD
    rope_spec = pl.BlockSpec((tm, LANES), lambda i: (i % per_batch, 0))
    vec = lambda w: pl.BlockSpec((1, w), lambda i: (0, 0))
    return pl.pallas_call(
        _mla_prep_kernel,
        grid=(t // tm,),
        in_specs=[pl.BlockSpec((tm, MLA_Q_RANK), lambda i: (i, C_MQ // MLA_Q_RANK)),
                  pl.BlockSpec((tm, MLA_KV_RANK), lambda i: (i, C_MKV // MLA_KV_RANK)),
                  pl.BlockSpec((tm, LANES), lambda i: (i, C_MKR // LANES)),
                  pl.BlockSpec((None, MLA_Q_RANK, hp), lambda i: (layer, 0, 0)),
                  pl.BlockSpec((None, MLA_KV_RANK, hp), lambda i: (layer, 0, 0)),
                  vec(MLA_Q_RANK), vec(MLA_KV_RANK), vec(MLA_HEAD_PAD), vec(MLA_HEAD_PAD),
                  rope_spec, rope_spec, rope_spec],
        out_specs=[pl.BlockSpec((tm, hp), lambda i: (i, 0)),
                   pl.BlockSpec((tm, hp), lambda i: (i, 0)),
                   pl.BlockSpec((tm, MLA_HEADS * MLA_V), lambda i: (i, 0))],
        out_shape=[jax.ShapeDtypeStruct((t, hp), BF16), jax.ShapeDtypeStruct((t, hp), BF16),
                   jax.ShapeDtypeStruct((t, MLA_HEADS * MLA_V), BF16)],
        compiler_params=_params(1),
        name="mla_prep",
    )(proj, proj, proj, wq_p, wkv, qln, kvln, qn_p, kn_p, *rope_r)


def _qk_scores(q, k):
    return lax.dot_general(q, k, (((1,), (1,)), ((), ())), preferred_element_type=F32)


def _attention_kernel(*refs, mode, group, tq, seq, tile0):
    if mode == "win":
        sink_ref, q_ref, qg_ref, cos_ref, sa_ref, sb_ref, k_ref, v_ref, o_ref = refs
    elif mode == "glb":
        q_ref, qg_ref, cos_ref, sa_ref, sb_ref, k_ref, v_ref, o_ref = refs
    else:
        q_ref, k_ref, v_ref, o_ref = refs
    head = pl.program_id(1)
    tile = pl.program_id(2) + tile0
    dv = v_ref.shape[-1]

    if mode == "mla":
        q = q_ref[...]
    else:
        cos, sa, sb = cos_ref[...], sa_ref[...], sb_ref[...]
        parts = []
        for g in range(group):
            y = _rms(q_ref[:, g * HEAD_DIM:(g + 1) * HEAD_DIM].astype(F32), HEAD_DIM) * qg_ref[...]
            y = _rope(y, cos, sa, sb, HEAD_DIM // 4) * (HEAD_DIM ** -0.5)
            parts.append(y.astype(BF16))
        q = jnp.concatenate(parts, axis=0)

    if mode == "win":
        sink = jnp.concatenate(
            [jnp.full((tq, 1), sink_ref[head * group + g], F32) for g in range(group)], axis=0)
    else:
        sink = None

    def finish(exps, vals, row_max):
        den = sum(jnp.sum(e, axis=-1, keepdims=True) for e in exps)
        if sink is not None:
            den = den + jnp.exp(sink - row_max)
        acc = sum(jnp.dot(e.astype(BF16), v, preferred_element_type=F32) for e, v in zip(exps, vals))
        out = acc / den
        for g in range(group):
            o_ref[:, g * dv:(g + 1) * dv] = out[g * tq:(g + 1) * tq, :].astype(o_ref.dtype)

    def softmax_over(scores, vals):
        row_max = functools.reduce(jnp.maximum, [jnp.max(s, axis=-1, keepdims=True) for s in scores])
        if sink is not None:
            row_max = jnp.maximum(row_max, sink)
        finish([jnp.exp(s - row_max) for s in scores], vals, row_max)

    @pl.when(tile < CTX_LEN // tq)
    def _():
        softmax_over([_qk_scores(q, k_ref[:CTX_LEN, :])], [v_ref[:CTX_LEN, :]])

    @pl.when(tile >= CTX_LEN // tq)
    def _():
        if mode == "win":
            blk = tile - CTX_LEN // tq
            start = jnp.clip((blk - 1) * BLOCK, 0, seq - 3 * BLOCK)
            row0 = pl.multiple_of(CTX_LEN + start, BLOCK)
            s_loc = _qk_scores(q, k_ref[pl.ds(row0, 3 * BLOCK), :])
            k_pos = start + lax.broadcasted_iota(jnp.int32, s_loc.shape, 1)
            q_pos = blk * BLOCK + lax.broadcasted_iota(jnp.int32, s_loc.shape, 0) % tq
            s_loc = jnp.where(jnp.abs(q_pos - k_pos) <= WINDOW, s_loc, NEG_INF)
            s_ctx = _qk_scores(q, k_ref[:CTX_LEN, :])
            softmax_over([s_loc, s_ctx], [v_ref[pl.ds(row0, 3 * BLOCK), :], v_ref[:CTX_LEN, :]])
        else:
            softmax_over([_qk_scores(q, k_ref[...])], [v_ref[...]])


def _attention_call(mode, q_src, k_src, v_src, extras, n_batch, tpb, skip_ctx):
    t = q_src[0].shape[0]
    if mode == "mla":
        n_kv, group, dq, tq = MLA_HEADS, 1, MLA_HEAD_PAD, 256
    else:
        n_kv, group, dq, tq = WIN_KV_HEADS, WIN_HEADS // WIN_KV_HEADS, HEAD_DIM, BLOCK
    dv = HEAD_DIM
    tiles = tpb // tq
    tile0 = CTX_LEN // tq if skip_ctx else 0
    qw = group * dq
    q_blk, k_blk, v_blk = q_src[1] // qw, k_src[1] // dq, v_src[1] // dv
    q_spec = pl.BlockSpec((tq, qw), lambda b, h, i: (b * tiles + i + tile0, q_blk + h))
    k_spec = pl.BlockSpec((tpb, dq), lambda b, h, i: (b, k_blk + h))
    v_spec = pl.BlockSpec((tpb, dv), lambda b, h, i: (b, v_blk + h))
    in_specs, args = [], []
    if mode == "win":
        in_specs.append(pl.BlockSpec(memory_space=pltpu.SMEM))
        args.append(extras["sink"])
    in_specs.append(q_spec)
    args.append(q_src[0])
    if mode != "mla":
        rope_spec = pl.BlockSpec((tq, LANES), lambda b, h, i: (i + tile0, 0))
        in_specs += [pl.BlockSpec((1, HEAD_DIM), lambda b, h, i: (0, 0)), rope_spec, rope_spec, rope_spec]
        args += [extras["q_gain"], *extras["rope"]]
    in_specs += [k_spec, v_spec]
    args += [k_src[0], v_src[0]]
    kern = functools.partial(_attention_kernel, mode=mode, group=group, tq=tq, seq=tpb - CTX_LEN,
                             tile0=tile0)
    return pl.pallas_call(
        kern,
        grid=(n_batch, n_kv, tiles - tile0),
        in_specs=in_specs,
        out_specs=pl.BlockSpec((tq, group * dv), lambda b, h, i: (b * tiles + i + tile0, h)),
        out_shape=jax.ShapeDtypeStruct((t, n_kv * group * dv), BF16),
        compiler_params=_params(3),
        name=f"attention_{mode}",
    )(*args)


def _merge_kernel(ow_ref, om_ref, og_ref, gw_ref, gm_ref, gg_ref, wb_ref, wo_ref, x_ref, gate_ref,
                  o_ref, *, tiles_per_batch):
    i, j = pl.program_id(0), pl.program_id(1)
    tm = x_ref.shape[0]

    @pl.when(j == 0)
    def _():
        o_ref[...] = jnp.zeros_like(o_ref)

    y = None
    for br, (o_br, g_br) in enumerate(((ow_ref, gw_ref), (om_ref, gm_ref), (og_ref, gg_ref))):
        term = jax.nn.sigmoid(g_br[...].astype(F32)) * jnp.dot(o_br[...], wb_ref[br],
                                                                preferred_element_type=F32)
        y = term if y is None else y + term
    o_ref[...] += jnp.dot(y.astype(BF16), wo_ref[...], preferred_element_type=F32)

    @pl.when(j == pl.num_programs(1) - 1)
    def _():
        gate = _gate_rows(gate_ref, i // tiles_per_batch, (i % tiles_per_batch) * tm, tm)
        o_ref[...] = x_ref[...] + gate * o_ref[...]


def _merge_call(xs, o_win, o_mla, o_glb, proj, mod, w_branch, w_out, layer, tpb):
    t, d = xs.shape
    tm = _largest_divisor(tpb, (768, 384, 256, 128))
    tn = _largest_divisor(d, (512, 256, 128))
    o_spec = pl.BlockSpec((tm, BRANCH_W), lambda i, j: (i, 0))
    gate_spec = lambda br: pl.BlockSpec((tm, tn), lambda i, j: (i, (C_GATE + br * d) // tn + j))
    kern = functools.partial(_merge_kernel, tiles_per_batch=tpb // tm)
    return pl.pallas_call(
        kern,
        grid=(t // tm, d // tn),
        in_specs=[o_spec, o_spec, o_spec, gate_spec(0), gate_spec(1), gate_spec(2),
                  pl.BlockSpec((None, N_BRANCH, BRANCH_W, tn), lambda i, j: (layer, 0, 0, j)),
                  pl.BlockSpec((None, tn, d), lambda i, j: (layer, j, 0)),
                  pl.BlockSpec((tm, d), lambda i, j: (i, 0)),
                  pl.BlockSpec((None, MOD_ROWS, d), lambda i, j: (layer, 0, 2))],
        out_specs=pl.BlockSpec((tm, d), lambda i, j: (i, 0)),
        out_shape=jax.ShapeDtypeStruct((t, d), F32),
        input_output_aliases={8: 0},
        compiler_params=_params(2),
        name="merge_out_projection",
    )(o_win, o_mla, o_glb, proj, proj, proj, w_branch, w_out, xs, mod)


def _ffn_kernel(x_ref, sh_ref, sc_ref, gate_ref, wg_ref, wu_ref, wd_ref, o_ref, h_ref, *, tiles_per_batch):
    i, f = pl.program_id(0), pl.program_id(1)
    tm = x_ref.shape[0]
    batch, row0 = i // tiles_per_batch, (i % tiles_per_batch) * tm

    @pl.when(f == 0)
    def _():
        _norm_modulate(x_ref, sh_ref, sc_ref, h_ref, batch, row0, tm)
        o_ref[...] = jnp.zeros_like(o_ref)

    h = h_ref[...]
    g = jnp.dot(h, wg_ref[...], preferred_element_type=F32)
    u = jnp.dot(h, wu_ref[...], preferred_element_type=F32)
    act = (g * jax.nn.sigmoid(g) * u).astype(BF16)
    o_ref[...] += jnp.dot(act, wd_ref[...], preferred_element_type=F32)

    @pl.when(f == pl.num_programs(1) - 1)
    def _():
        o_ref[...] = x_ref[...] + _gate_rows(gate_ref, batch, row0, tm) * o_ref[...]


def _ffn_call(xs, mod, w_gate_up, w_down, idx, layer, tpb):
    t, d = xs.shape
    ff = w_down.shape[1]
    tm = _largest_divisor(tpb, (768, 384, 256, 128))
    tf = _largest_divisor(ff, (512, 256, 128))
    n_f = ff // tf
    mod_spec = lambda k: pl.BlockSpec((None, MOD_ROWS, d), lambda i, f: (layer, 0, k))
    kern = functools.partial(_ffn_kernel, tiles_per_batch=tpb // tm)
    return pl.pallas_call(
        kern,
        grid=(t // tm, n_f),
        in_specs=[pl.BlockSpec((tm, d), lambda i, f: (i, 0)), mod_spec(3), mod_spec(4), mod_spec(5),
                  pl.BlockSpec((None, d, tf), lambda i, f: (idx, 0, f)),
                  pl.BlockSpec((None, d, tf), lambda i, f: (idx, 0, f + n_f)),
                  pl.BlockSpec((None, tf, d), lambda i, f: (idx, f, 0))],
        out_specs=pl.BlockSpec((tm, d), lambda i, f: (i, 0)),
        out_shape=jax.ShapeDtypeStruct((t, d), F32),
        scratch_shapes=[pltpu.VMEM((tm, d), BF16)],
        input_output_aliases={0: 0},
        compiler_params=_params(2),
        name="swiglu_ffn",
    )(xs, mod, mod, mod, w_gate_up, w_gate_up, w_down)


def _router_kernel(x_ref, sh_ref, sc_ref, r_ref, o_ref, h_ref, *, tiles_per_batch):
    i = pl.program_id(0)
    tm = x_ref.shape[0]
    _norm_modulate(x_ref, sh_ref, sc_ref, h_ref, i // tiles_per_batch, (i % tiles_per_batch) * tm, tm)
    logits = jnp.dot(h_ref[...], r_ref[...], preferred_element_type=F32, precision=lax.Precision.HIGHEST)
    lane = lax.broadcasted_iota(jnp.int32, logits.shape, 1).astype(F32)
    logits = jnp.where(lane < N_EXPERTS, logits, -jnp.inf)
    top1 = jnp.max(logits, axis=-1, keepdims=True)
    idx1 = jnp.min(jnp.where(logits == top1, lane, float(LANES)), axis=-1, keepdims=True)
    rest = jnp.where(lane == idx1, -jnp.inf, logits)
    top2 = jnp.max(rest, axis=-1, keepdims=True)
    idx2 = jnp.min(jnp.where(rest == top2, lane, float(LANES)), axis=-1, keepdims=True)
    e2 = jnp.exp(top2 - top1)
    den = 1.0 + e2
    o_ref[...] = jnp.where(lane == idx1, 1.0 / den, jnp.where(lane == idx2, e2 / den, 0.0))


def _router_call(xs, mod, router_p, layer, tpb):
    t, d = xs.shape
    tm = _largest_divisor(tpb, (768, 384, 256, 128))
    mod_spec = lambda k: pl.BlockSpec((None, MOD_ROWS, d), lambda i: (layer, 0, k))
    kern = functools.partial(_router_kernel, tiles_per_batch=tpb // tm)
    return pl.pallas_call(
        kern,
        grid=(t // tm,),
        in_specs=[pl.BlockSpec((tm, d), lambda i: (i, 0)), mod_spec(3), mod_spec(4),
                  pl.BlockSpec((d, LANES), lambda i: (0, 0))],
        out_specs=pl.BlockSpec((tm, LANES), lambda i: (i, 0)),
        out_shape=jax.ShapeDtypeStruct((t, LANES), F32),
        scratch_shapes=[pltpu.VMEM((tm, d), F32)],
        compiler_params=_params(1),
        name="moe_router",
    )(xs, mod, mod, router_p)


def _moe_kernel(x_ref, sh_ref, sc_ref, gate_ref, rg_ref, wg_ref, wu_ref, wd_ref, o_ref, h_ref, *,
                tiles_per_batch):
    i, e, f = pl.program_id(0), pl.program_id(1), pl.program_id(2)
    tm = x_ref.shape[0]
    batch, row0 = i // tiles_per_batch, (i % tiles_per_batch) * tm

    @pl.when((e == 0) & (f == 0))
    def _():
        _norm_modulate(x_ref, sh_ref, sc_ref, h_ref, batch, row0, tm)
        o_ref[...] = jnp.zeros_like(o_ref)

    lane = lax.broadcasted_iota(jnp.int32, rg_ref.shape, 1)
    w_e = jnp.sum(jnp.where(lane == e, rg_ref[...], 0.0), axis=-1, keepdims=True)
    h = h_ref[...]
    g = jnp.dot(h, wg_ref[...], preferred_element_type=F32)
    u = jnp.dot(h, wu_ref[...], preferred_element_type=F32)
    act = (g * jax.nn.sigmoid(g) * u).astype(BF16)
    o_ref[...] += w_e * jnp.dot(act, wd_ref[...], preferred_element_type=F32)

    @pl.when((e == pl.num_programs(1) - 1) & (f == pl.num_programs(2) - 1))
    def _():
        o_ref[...] = x_ref[...] + _gate_rows(gate_ref, batch, row0, tm) * o_ref[...]


def _moe_call(xs, mod, route, w_gate_up, w_down, idx, layer, tpb):
    t, d = xs.shape
    ff = w_down.shape[2]
    tm = _largest_divisor(tpb, (768, 384, 256, 128))
    tf = _largest_divisor(ff, (512, 256, 128))
    n_f = ff // tf
    mod_spec = lambda k: pl.BlockSpec((None, MOD_ROWS, d), lambda i, e, f: (layer, 0, k))
    kern = functools.partial(_moe_kernel, tiles_per_batch=tpb // tm)
    return pl.pallas_call(
        kern,
        grid=(t // tm, N_EXPERTS, n_f),
        in_specs=[pl.BlockSpec((tm, d), lambda i, e, f: (i, 0)), mod_spec(3), mod_spec(4), mod_spec(5),
                  pl.BlockSpec((tm, LANES), lambda i, e, f: (i, 0)),
                  pl.BlockSpec((None, None, d, tf), lambda i, e, f: (idx, e, 0, f)),
                  pl.BlockSpec((None, None, d, tf), lambda i, e, f: (idx, e, 0, f + n_f)),
                  pl.BlockSpec((None, None, tf, d), lambda i, e, f: (idx, e, f, 0))],
        out_specs=pl.BlockSpec((tm, d), lambda i, e, f: (i, 0)),
        out_shape=jax.ShapeDtypeStruct((t, d), F32),
        scratch_shapes=[pltpu.VMEM((tm, d), BF16)],
        input_output_aliases={0: 0},
        compiler_params=_params(3),
        name="moe_swiglu",
    )(xs, mod, mod, mod, route, w_gate_up, w_gate_up, w_down)


def _rope_tables(seq, dim):
    rows = seq // GRID_W
    row = jnp.repeat(jnp.arange(rows), GRID_W).astype(F32)
    col = jnp.tile(jnp.arange(GRID_W), rows).astype(F32)
    quarter = dim // 4
    freqs = ROPE_THETA ** (-jnp.arange(quarter, dtype=F32) / quarter)
    ang_r = row[:, None] * freqs
    ang_c = col[:, None] * freqs
    ang = jnp.concatenate([ang_r, ang_r, ang_c, ang_c], axis=-1)
    cos, sin = jnp.cos(ang), jnp.sin(ang)
    zero = jnp.zeros_like(ang_r)
    sin_a = jnp.concatenate([-sin[:, :quarter], zero, -sin[:, 2 * quarter:3 * quarter], zero], axis=-1)
    sin_b = jnp.concatenate([zero, sin[:, quarter:2 * quarter], zero, sin[:, 3 * quarter:]], axis=-1)

    def full(tab, fill):
        tab = jnp.pad(tab, ((0, 0), (0, LANES - dim)), constant_values=fill)
        return jnp.concatenate([jnp.full((CTX_LEN, LANES), fill, F32), tab], axis=0)

    return full(cos, 1.0), full(sin_a, 0.0), full(sin_b, 0.0)


def _layout_w_in(w_in):
    sizes = (WIN_HEADS * HEAD_DIM, WIN_KV_HEADS * HEAD_DIM, WIN_KV_HEADS * HEAD_DIM,
             MLA_Q_RANK, MLA_KV_RANK, MLA_ROPE,
             GLB_HEADS * HEAD_DIM, GLB_KV_HEADS * HEAD_DIM, GLB_KV_HEADS * HEAD_DIM)
    offs = np.concatenate([[0], np.cumsum(sizes)])
    wq, wk, wv, mq, mkv, mkr, gq, gk, gv = (w_in[..., int(offs[k]):int(offs[k + 1])] for k in range(9))
    gates = w_in[..., int(offs[9]):]
    pad = jnp.zeros(w_in.shape[:-1] + (C_GATE - C_MKR - MLA_ROPE,), w_in.dtype)
    return jnp.concatenate([wq, wk, wv, gq, gk, gv, mq, mkv, mkr, pad, gates], axis=-1).astype(BF16)


def _pad_heads(w, width):
    w = w.reshape(w.shape[:-1] + (MLA_HEADS, width))
    w = jnp.pad(w, [(0, 0)] * (w.ndim - 1) + [(0, MLA_HEAD_PAD - width)])
    return w.reshape(w.shape[:-2] + (MLA_HEADS * MLA_HEAD_PAD,))


def kernel(x, c, ctx, c_ctx, w_mod, b_mod, w_in, win_q_norm, win_k_norm, win_sink, mla_q_lat_norm,
           mla_kv_lat_norm, mla_w_q_up, mla_w_kv_up, mla_q_norm, mla_k_norm, glb_q_norm, glb_k_norm,
           w_branch, w_out, ffn_w_gate_up, ffn_w_down, moe_router, moe_w_gate_up, moe_w_down):
    n_batch, seq, d = x.shape
    depth = w_mod.shape[0]
    tpb = CTX_LEN + seq
    assert ctx.shape[1] == CTX_LEN and seq % GRID_W == 0 and seq % BLOCK == 0 and seq >= 3 * BLOCK
    assert n_batch <= MOD_ROWS - 8 and C_MKR + LANES <= C_GATE

    xs = jnp.concatenate([ctx, x], axis=1).reshape(n_batch * tpb, d)
    cvec = jnp.zeros((MOD_ROWS, d), F32).at[:n_batch].set(c).at[MOD_ROWS - 8].set(c_ctx)
    mod = _mod_call(cvec, w_mod, b_mod)

    rope_h = _rope_tables(seq, HEAD_DIM)
    rope_r = _rope_tables(seq, MLA_ROPE)
    w_in_p = _layout_w_in(w_in)
    wq_up = _pad_heads(mla_w_q_up, MLA_NOPE + MLA_ROPE).astype(BF16)
    wkv_up = mla_w_kv_up.astype(BF16)
    pad_gain = lambda g: jnp.pad(g, (0, MLA_HEAD_PAD - g.shape[0]))[None]
    w_branch_b, w_out_b = w_branch.astype(BF16), w_out.astype(BF16)
    ffn_gu, ffn_dn = ffn_w_gate_up.astype(BF16), ffn_w_down.astype(BF16)
    moe_gu, moe_dn = moe_w_gate_up.astype(BF16), moe_w_down.astype(BF16)

    for layer in range(depth):
        last = False
        proj = _inproj_call(xs, mod, w_in_p, layer, tpb)
        k_win, k_glb = _kprep_call(proj, win_k_norm[layer][None], glb_k_norm[layer][None], rope_h, tpb)
        q_mla, k_mla, v_mla = _mla_prep_call(
            proj, wq_up, wkv_up, mla_q_lat_norm[layer][None], mla_kv_lat_norm[layer][None],
            pad_gain(mla_q_norm[layer]), pad_gain(mla_k_norm[layer]), rope_r, layer, tpb)
        o_win = _attention_call("win", (proj, C_WQ), (k_win, 0), (proj, C_WV),
                                dict(sink=win_sink[layer], q_gain=win_q_norm[layer][None], rope=rope_h),
                                n_batch, tpb, last)
        o_mla = _attention_call("mla", (q_mla, 0), (k_mla, 0), (v_mla, 0), {}, n_batch, tpb, last)
        o_glb = _attention_call("glb", (proj, C_GQ), (k_glb, 0), (proj, C_GV),
                                dict(q_gain=glb_q_norm[layer][None], rope=rope_h), n_batch, tpb, last)
        xs = _merge_call(xs, o_win, o_mla, o_glb, proj, mod, w_branch_b, w_out_b, layer, tpb)
        if layer % 2 == 0:
            xs = _ffn_call(xs, mod, ffn_gu, ffn_dn, layer // 2, layer, tpb)
        else:
            router_p = jnp.pad(moe_router[layer // 2], ((0, 0), (0, LANES - N_EXPERTS)))
            route = _router_call(xs, mod, router_p, layer, tpb)
            xs = _moe_call(xs, mod, route, moe_gu, moe_dn, layer // 2, layer, tpb)
    return xs.reshape(n_batch, tpb, d)[:, CTX_LEN:, :]
```

```python
import functools

import jax
import jax.numpy as jnp
import numpy as np
from jax import lax
from jax.experimental import pallas as pl
from jax.experimental.pallas import tpu as pltpu

CTX_LEN = 256
GRID_W = 64
HEAD_DIM = 128
WIN_HEADS = 8
WIN_KV_HEADS = 2
WINDOW = 128
BLOCK = 128
MLA_HEADS = 8
MLA_NOPE = 128
MLA_ROPE = 64
MLA_V = 128
MLA_Q_RANK = 512
MLA_KV_RANK = 256
GLB_HEADS = 8
GLB_KV_HEADS = 2
BRANCH_W = 1024
N_BRANCH = 3
N_EXPERTS = 8
ROPE_THETA = 10000.0
EPS = 1e-6
NEG_INF = -1e30

LANES = 128
MLA_HEAD_PAD = 256
MOD_ROWS = 24
VMEM_LIMIT = 60 * 1024 * 1024

C_WQ = 0
C_WK = C_WQ + WIN_HEADS * HEAD_DIM
C_WV = C_WK + WIN_KV_HEADS * HEAD_DIM
C_GQ = C_WV + WIN_KV_HEADS * HEAD_DIM
C_GK = C_GQ + GLB_HEADS * HEAD_DIM
C_GV = C_GK + GLB_KV_HEADS * HEAD_DIM
C_MQ = C_GV + GLB_KV_HEADS * HEAD_DIM
C_MKV = C_MQ + MLA_Q_RANK
C_MKR = C_MKV + MLA_KV_RANK
C_GATE = 4096

BF16 = jnp.bfloat16
F32 = jnp.float32


def _largest_divisor(n, candidates):
    for cand in candidates:
        if n % cand == 0:
            return cand
    raise ValueError(f"no tile in {candidates} divides {n}")


def _params(n_axes):
    return pltpu.CompilerParams(dimension_semantics=("arbitrary",) * n_axes,
                                vmem_limit_bytes=VMEM_LIMIT)


def _rms(x, width):
    ms = jnp.sum(x * x, axis=-1, keepdims=True) * (1.0 / width)
    return x * lax.rsqrt(ms + EPS)


def _rope(y, cos, sin_a, sin_b, quarter):
    return (y * cos + pltpu.roll(y, LANES - quarter, 1) * sin_a + pltpu.roll(y, quarter, 1) * sin_b)


def _norm_modulate(x_ref, sh_ref, sc_ref, h_ref, batch, row0, n_rows):
    chunk = 16

    def body(ci, carry):
        r0 = pl.multiple_of(ci * chunk, chunk)
        xf = x_ref[pl.ds(r0, chunk), :]
        y = _rms(xf, xf.shape[-1])
        row = jnp.where(row0 + r0 < CTX_LEN, MOD_ROWS - 8, batch)
        sh = sh_ref[pl.ds(row, 1), :]
        sc = sc_ref[pl.ds(row, 1), :]
        h_ref[pl.ds(r0, chunk), :] = (y * (1.0 + sc) + sh).astype(h_ref.dtype)
        return carry

    lax.fori_loop(0, n_rows // chunk, body, 0)


def _gate_rows(g_ref, batch, row0, n_rows):
    rows = lax.broadcasted_iota(jnp.int32, (n_rows, 1), 0) + row0
    g_lat = g_ref[pl.ds(batch, 1), :]
    g_ctx = g_ref[pl.ds(MOD_ROWS - 8, 1), :]
    return jnp.where(rows < CTX_LEN, g_ctx, g_lat)


def _mod_kernel(c_ref, w_ref, b_ref, o_ref):
    cv = c_ref[...]
    sc = (cv * jax.nn.sigmoid(cv)).astype(BF16)
    o_ref[...] = jnp.dot(sc, w_ref[...].astype(BF16), preferred_element_type=F32) + b_ref[...]


def _mod_call(cvec, w_mod, b_mod):
    depth, d, n = w_mod.shape
    tn = _largest_divisor(n, (1024, 512, 256, 128))
    return pl.pallas_call(
        _mod_kernel,
        grid=(depth, n // tn),
        in_specs=[pl.BlockSpec((MOD_ROWS, d), lambda l, j: (0, 0)),
                  pl.BlockSpec((None, d, tn), lambda l, j: (l, 0, j)),
                  pl.BlockSpec((None, 1, tn), lambda l, j: (l, 0, j))],
        out_specs=pl.BlockSpec((None, MOD_ROWS, tn), lambda l, j: (l, 0, j)),
        out_shape=jax.ShapeDtypeStruct((depth, MOD_ROWS, n), F32),
        compiler_params=_params(2),
        name="mod_vectors",
    )(cvec, w_mod, b_mod.reshape(depth, 1, n))


def _inproj_kernel(x_ref, sh_ref, sc_ref, w_ref, o_ref, h_ref, *, tiles_per_batch):
    i = pl.program_id(0)
    tm = x_ref.shape[0]

    @pl.when(pl.program_id(1) == 0)
    def _():
        _norm_modulate(x_ref, sh_ref, sc_ref, h_ref, i // tiles_per_batch,
                       (i % tiles_per_batch) * tm, tm)

    o_ref[...] = jnp.dot(h_ref[...], w_ref[...], preferred_element_type=F32).astype(o_ref.dtype)


def _inproj_call(xs, mod, w_in_p, layer, tpb):
    t, d = xs.shape
    n = w_in_p.shape[-1]
    tm = _largest_divisor(tpb, (1152, 768, 384, 256, 128))
    tn = _largest_divisor(n, (1024, 512, 256, 128))
    kern = functools.partial(_inproj_kernel, tiles_per_batch=tpb // tm)
    return pl.pallas_call(
        kern,
        grid=(t // tm, n // tn),
        in_specs=[pl.BlockSpec((tm, d), lambda i, j: (i, 0)),
                  pl.BlockSpec((None, MOD_ROWS, d), lambda i, j: (layer, 0, 0)),
                  pl.BlockSpec((None, MOD_ROWS, d), lambda i, j: (layer, 0, 1)),
                  pl.BlockSpec((None, d, tn), lambda i, j: (layer, 0, j))],
        out_specs=pl.BlockSpec((tm, tn), lambda i, j: (i, j)),
        out_shape=jax.ShapeDtypeStruct((t, n), BF16),
        scratch_shapes=[pltpu.VMEM((tm, d), BF16)],
        compiler_params=_params(2),
        name="in_projection",
    )(xs, mod, mod, w_in_p)


def _kprep_kernel(wk_ref, gk_ref, wg_ref, gg_ref, cos_ref, sa_ref, sb_ref, ka_ref, kc_ref):
    cos, sa, sb = cos_ref[...], sa_ref[...], sb_ref[...]
    for src, gain, dst in ((wk_ref, wg_ref, ka_ref), (gk_ref, gg_ref, kc_ref)):
        for hd in range(src.shape[1] // HEAD_DIM):
            cols = slice(hd * HEAD_DIM, (hd + 1) * HEAD_DIM)
            y = _rms(src[:, cols].astype(F32), HEAD_DIM) * gain[...]
            dst[:, cols] = _rope(y, cos, sa, sb, HEAD_DIM // 4).astype(dst.dtype)


def _kprep_call(proj, win_k_gain, glb_k_gain, rope_h, tpb):
    t = proj.shape[0]
    tm = _largest_divisor(tpb, (768, 384, 256, 128))
    per_batch = tpb // tm
    kw = WIN_KV_HEADS * HEAD_DIM
    row_spec = lambda col: pl.BlockSpec((tm, kw), lambda i: (i, col))
    gain_spec = pl.BlockSpec((1, HEAD_DIM), lambda i: (0, 0))
    rope_spec = pl.BlockSpec((tm, LANES), lambda i: (i % per_batch, 0))
    out = jax.ShapeDtypeStruct((t, kw), BF16)
    return pl.pallas_call(
        _kprep_kernel,
        grid=(t // tm,),
        in_specs=[row_spec(C_WK // kw), row_spec(C_GK // kw), gain_spec, gain_spec,
                  rope_spec, rope_spec, rope_spec],
        out_specs=[pl.BlockSpec((tm, kw), lambda i: (i, 0))] * 2,
        out_shape=[out, out],
        compiler_params=_params(1),
        name="gqa_key_prep",
    )(proj, proj, win_k_gain, glb_k_gain, *rope_h)


def _mla_prep_kernel(mq_ref, mkv_ref, mkr_ref, wq_ref, wkv_ref, qln_ref, kvln_ref, qn_ref, kn_ref,
                     cos_ref, sa_ref, sb_ref, q_out, k_out, v_out):
    width = MLA_NOPE + MLA_ROPE
    quarter = MLA_ROPE // 4
    cos, sa, sb = cos_ref[...], sa_ref[...], sb_ref[...]
    q_lat = (_rms(mq_ref[...].astype(F32), MLA_Q_RANK) * qln_ref[...]).astype(BF16)
    kv_lat = (_rms(mkv_ref[...].astype(F32), MLA_KV_RANK) * kvln_ref[...]).astype(BF16)
    qu = jnp.dot(q_lat, wq_ref[...], preferred_element_type=F32)
    kvu = jnp.dot(kv_lat, wkv_ref[...], preferred_element_type=F32)
    kr = mkr_ref[...].astype(F32)
    kr_sq = jnp.sum(kr * kr, axis=-1, keepdims=True)
    q_scale = width ** -0.5
    for hd in range(MLA_HEADS):
        lo = hd * MLA_HEAD_PAD
        qa = qu[:, lo:lo + LANES]
        qb = qu[:, lo + LANES:lo + 2 * LANES]
        ms = (jnp.sum(qa * qa, axis=-1, keepdims=True) + jnp.sum(qb * qb, axis=-1, keepdims=True)) / width
        rstd = lax.rsqrt(ms + EPS) * q_scale
        q_out[:, lo:lo + LANES] = (qa * rstd * qn_ref[:, :LANES]).astype(q_out.dtype)
        q_out[:, lo + LANES:lo + 2 * LANES] = _rope(qb * rstd * qn_ref[:, LANES:], cos, sa, sb,
                                                     quarter).astype(q_out.dtype)
        kn = kvu[:, lo:lo + MLA_NOPE]
        ms = (jnp.sum(kn * kn, axis=-1, keepdims=True) + kr_sq) / width
        rstd = lax.rsqrt(ms + EPS)
        k_out[:, lo:lo + LANES] = (kn * rstd * kn_ref[:, :LANES]).astype(k_out.dtype)
        k_out[:, lo + LANES:lo + 2 * LANES] = _rope(kr * rstd * kn_ref[:, LANES:], cos, sa, sb,
                                                     quarter).astype(k_out.dtype)
        v_out[:, hd * MLA_V:(hd + 1) * MLA_V] = kvu[:, lo + MLA_NOPE:lo + MLA_NOPE + MLA_V].astype(v_out.dtype)


def _mla_prep_call(proj, wq_p, wkv, qln, kvln, qn_p, kn_p, rope_r, layer, tpb):
    t = proj.shape[0]
    tm = _largest_divisor(tpb, (768, 384, 256, 128))
    per_batch = tpb // tm
    hp = MLA_HEADS * MLA_HEAD_PAD
    rope_spec = pl.BlockSpec((tm, LANES), lambda i: (i % per_batch, 0))
    vec = lambda w: pl.BlockSpec((1, w), lambda i: (0, 0))
    return pl.pallas_call(
        _mla_prep_kernel,
        grid=(t // tm,),
        in_specs=[pl.BlockSpec((tm, MLA_Q_RANK), lambda i: (i, C_MQ // MLA_Q_RANK)),
                  pl.BlockSpec((tm, MLA_KV_RANK), lambda i: (i, C_MKV // MLA_KV_RANK)),
                  pl.BlockSpec((tm, LANES), lambda i: (i, C_MKR // LANES)),
                  pl.BlockSpec((None, MLA_Q_RANK, hp), lambda i: (layer, 0, 0)),
                  pl.BlockSpec((None, MLA_KV_RANK, hp), lambda i: (layer, 0, 0)),
                  vec(MLA_Q_RANK), vec(MLA_KV_RANK), vec(MLA_HEAD_PAD), vec(MLA_HEAD_PAD),
                  rope_spec, rope_spec, rope_spec],
        out_specs=[pl.BlockSpec((tm, hp), lambda i: (i, 0)),
                   pl.BlockSpec((tm, hp), lambda i: (i, 0)),
                   pl.BlockSpec((tm, MLA_HEADS * MLA_V), lambda i: (i, 0))],
        out_shape=[jax.ShapeDtypeStruct((t, hp), BF16), jax.ShapeDtypeStruct((t, hp), BF16),
                   jax.ShapeDtypeStruct((t, MLA_HEADS * MLA_V), BF16)],
        compiler_params=_params(1),
        name="mla_prep",
    )(proj, proj, proj, wq_p, wkv, qln, kvln, qn_p, kn_p, *rope_r)


def _qk_scores(q, k):
    return lax.dot_general(q, k, (((1,), (1,)), ((), ())), preferred_element_type=F32)


def _attention_kernel(*refs, mode, group, tq, seq, tile0):
    if mode == "win":
        sink_ref, q_ref, qg_ref, cos_ref, sa_ref, sb_ref, k_ref, v_ref, o_ref = refs
    elif mode == "glb":
        q_ref, qg_ref, cos_ref, sa_ref, sb_ref, k_ref, v_ref, o_ref = refs
    else:
        q_ref, k_ref, v_ref, o_ref = refs
    head = pl.program_id(1)
    tile = pl.program_id(2) + tile0
    dv = v_ref.shape[-1]

    if mode == "mla":
        q = q_ref[...]
    else:
        cos, sa, sb = cos_ref[...], sa_ref[...], sb_ref[...]
        parts = []
        for g in range(group):
            y = _rms(q_ref[:, g * HEAD_DIM:(g + 1) * HEAD_DIM].astype(F32), HEAD_DIM) * qg_ref[...]
            y = _rope(y, cos, sa, sb, HEAD_DIM // 4) * (HEAD_DIM ** -0.5)
            parts.append(y.astype(BF16))
        q = jnp.concatenate(parts, axis=0)

    if mode == "win":
        sink = jnp.concatenate(
            [jnp.full((tq, 1), sink_ref[head * group + g], F32) for g in range(group)], axis=0)
    else:
        sink = None

    def finish(exps, vals, row_max):
        den = sum(jnp.sum(e, axis=-1, keepdims=True) for e in exps)
        if sink is not None:
            den = den + jnp.exp(sink - row_max)
        acc = sum(jnp.dot(e.astype(BF16), v, preferred_element_type=F32) for e, v in zip(exps, vals))
        out = acc / den
        for g in range(group):
            o_ref[:, g * dv:(g + 1) * dv] = out[g * tq:(g + 1) * tq, :].astype(o_ref.dtype)

    def softmax_over(scores, vals):
        row_max = functools.reduce(jnp.maximum, [jnp.max(s, axis=-1, keepdims=True) for s in scores])
        if sink is not None:
            row_max = jnp.maximum(row_max, sink)
        finish([jnp.exp(s - row_max) for s in scores], vals, row_max)

    @pl.when(tile < CTX_LEN // tq)
    def _():
        softmax_over([_qk_scores(q, k_ref[:CTX_LEN, :])], [v_ref[:CTX_LEN, :]])

    @pl.when(tile >= CTX_LEN // tq)
    def _():
        if mode == "win":
            blk = tile - CTX_LEN // tq
            start = jnp.clip((blk - 1) * BLOCK, 0, seq - 3 * BLOCK)
            row0 = pl.multiple_of(CTX_LEN + start, BLOCK)
            s_loc = _qk_scores(q, k_ref[pl.ds(row0, 3 * BLOCK), :])
            k_pos = start + lax.broadcasted_iota(jnp.int32, s_loc.shape, 1)
            q_pos = blk * BLOCK + lax.broadcasted_iota(jnp.int32, s_loc.shape, 0) % tq
            s_loc = jnp.where(jnp.abs(q_pos - k_pos) <= WINDOW, s_loc, NEG_INF)
            s_ctx = _qk_scores(q, k_ref[:CTX_LEN, :])
            softmax_over([s_loc, s_ctx], [v_ref[pl.ds(row0, 3 * BLOCK), :], v_ref[:CTX_LEN, :]])
        else:
            softmax_over([_qk_scores(q, k_ref[...])], [v_ref[...]])


def _attention_call(mode, q_src, k_src, v_src, extras, n_batch, tpb, skip_ctx):
    t = q_src[0].shape[0]
    if mode == "mla":
        n_kv, group, dq, tq = MLA_HEADS, 1, MLA_HEAD_PAD, 256
    else:
        n_kv, group, dq, tq = WIN_KV_HEADS, WIN_HEADS // WIN_KV_HEADS, HEAD_DIM, BLOCK
    dv = HEAD_DIM
    tiles = tpb // tq
    tile0 = CTX_LEN // tq if skip_ctx else 0
    qw = group * dq
    q_blk, k_blk, v_blk = q_src[1] // qw, k_src[1] // dq, v_src[1] // dv
    q_spec = pl.BlockSpec((tq, qw), lambda b, h, i: (b * tiles + i + tile0, q_blk + h))
    k_spec = pl.BlockSpec((tpb, dq), lambda b, h, i: (b, k_blk + h))
    v_spec = pl.BlockSpec((tpb, dv), lambda b, h, i: (b, v_blk + h))
    in_specs, args = [], []
    if mode == "win":
        in_specs.append(pl.BlockSpec(memory_space=pltpu.SMEM))
        args.append(extras["sink"])
    in_specs.append(q_spec)
    args.append(q_src[0])
    if mode != "mla":
        rope_spec = pl.BlockSpec((tq, LANES), lambda b, h, i: (i + tile0, 0))
        in_specs += [pl.BlockSpec((1, HEAD_DIM), lambda b, h, i: (0, 0)), rope_spec, rope_spec, rope_spec]
        args += [extras["q_gain"], *extras["rope"]]
    in_specs += [k_spec, v_spec]
    args += [k_src[0], v_src[0]]
    kern = functools.partial(_attention_kernel, mode=mode, group=group, tq=tq, seq=tpb - CTX_LEN,
                             tile0=tile0)
    return pl.pallas_call(
        kern,
        grid=(n_batch, n_kv, tiles - tile0),
        in_specs=in_specs,
        out_specs=pl.BlockSpec((tq, group * dv), lambda b, h, i: (b * tiles + i + tile0, h)),
        out_shape=jax.ShapeDtypeStruct((t, n_kv * group * dv), BF16),
        compiler_params=_params(3),
        name=f"attention_{mode}",
    )(*args)


def _merge_kernel(ow_ref, om_ref, og_ref, gw_ref, gm_ref, gg_ref, wb_ref, wo_ref, x_ref, gate_ref,
                  o_ref, *, tiles_per_batch):
    i, j = pl.program_id(0), pl.program_id(1)
    tm = x_ref.shape[0]

    @pl.when(j == 0)
    def _():
        o_ref[...] = jnp.zeros_like(o_ref)

    y = None
    for br, (o_br, g_br) in enumerate(((ow_ref, gw_ref), (om_ref, gm_ref), (og_ref, gg_ref))):
        term = jax.nn.sigmoid(g_br[...].astype(F32)) * jnp.dot(o_br[...], wb_ref[br],
                                                                preferred_element_type=F32)
        y = term if y is None else y + term
    o_ref[...] += jnp.dot(y.astype(BF16), wo_ref[...], preferred_element_type=F32)

    @pl.when(j == pl.num_programs(1) - 1)
    def _():
        gate = _gate_rows(gate_ref, i // tiles_per_batch, (i % tiles_per_batch) * tm, tm)
        o_ref[...] = x_ref[...] + gate * o_ref[...]


def _merge_call(xs, o_win, o_mla, o_glb, proj, mod, w_branch, w_out, layer, tpb):
    t, d = xs.shape
    tm = _largest_divisor(tpb, (768, 384, 256, 128))
    tn = _largest_divisor(d, (512, 256, 128))
    o_spec = pl.BlockSpec((tm, BRANCH_W), lambda i, j: (i, 0))
    gate_spec = lambda br: pl.BlockSpec((tm, tn), lambda i, j: (i, (C_GATE + br * d) // tn + j))
    kern = functools.partial(_merge_kernel, tiles_per_batch=tpb // tm)
    return pl.pallas_call(
        kern,
        grid=(t // tm, d // tn),
        in_specs=[o_spec, o_spec, o_spec, gate_spec(0), gate_spec(1), gate_spec(2),
                  pl.BlockSpec((None, N_BRANCH, BRANCH_W, tn), lambda i, j: (layer, 0, 0, j)),
                  pl.BlockSpec((None, tn, d), lambda i, j: (layer, j, 0)),
                  pl.BlockSpec((tm, d), lambda i, j: (i, 0)),
                  pl.BlockSpec((None, MOD_ROWS, d), lambda i, j: (layer, 0, 2))],
        out_specs=pl.BlockSpec((tm, d), lambda i, j: (i, 0)),
        out_shape=jax.ShapeDtypeStruct((t, d), F32),
        input_output_aliases={8: 0},
        compiler_params=_params(2),
        name="merge_out_projection",
    )(o_win, o_mla, o_glb, proj, proj, proj, w_branch, w_out, xs, mod)


def _ffn_kernel(x_ref, sh_ref, sc_ref, gate_ref, wg_ref, wu_ref, wd_ref, o_ref, h_ref, *, tiles_per_batch):
    i, f = pl.program_id(0), pl.program_id(1)
    tm = x_ref.shape[0]
    batch, row0 = i // tiles_per_batch, (i % tiles_per_batch) * tm

    @pl.when(f == 0)
    def _():
        _norm_modulate(x_ref, sh_ref, sc_ref, h_ref, batch, row0, tm)
        o_ref[...] = jnp.zeros_like(o_ref)

    h = h_ref[...]
    g = jnp.dot(h, wg_ref[...], preferred_element_type=F32)
    u = jnp.dot(h, wu_ref[...], preferred_element_type=F32)
    act = (g * jax.nn.sigmoid(g) * u).astype(BF16)
    o_ref[...] += jnp.dot(act, wd_ref[...], preferred_element_type=F32)

    @pl.when(f == pl.num_programs(1) - 1)
    def _():
        o_ref[...] = x_ref[...] + _gate_rows(gate_ref, batch, row0, tm) * o_ref[...]


def _ffn_call(xs, mod, w_gate_up, w_down, idx, layer, tpb):
    t, d = xs.shape
    ff = w_down.shape[1]
    tm = _largest_divisor(tpb, (768, 384, 256, 128))
    tf = _largest_divisor(ff, (512, 256, 128))
    n_f = ff // tf
    mod_spec = lambda k: pl.BlockSpec((None, MOD_ROWS, d), lambda i, f: (layer, 0, k))
    kern = functools.partial(_ffn_kernel, tiles_per_batch=tpb // tm)
    return pl.pallas_call(
        kern,
        grid=(t // tm, n_f),
        in_specs=[pl.BlockSpec((tm, d), lambda i, f: (i, 0)), mod_spec(3), mod_spec(4), mod_spec(5),
                  pl.BlockSpec((None, d, tf), lambda i, f: (idx, 0, f)),
                  pl.BlockSpec((None, d, tf), lambda i, f: (idx, 0, f + n_f)),
                  pl.BlockSpec((None, tf, d), lambda i, f: (idx, f, 0))],
        out_specs=pl.BlockSpec((tm, d), lambda i, f: (i, 0)),
        out_shape=jax.ShapeDtypeStruct((t, d), F32),
        scratch_shapes=[pltpu.VMEM((tm, d), BF16)],
        input_output_aliases={0: 0},
        compiler_params=_params(2),
        name="swiglu_ffn",
    )(xs, mod, mod, mod, w_gate_up, w_gate_up, w_down)


def _router_kernel(x_ref, sh_ref, sc_ref, r_ref, h_ref, o_ref, *, tiles_per_batch):
    i = pl.program_id(0)
    tm = x_ref.shape[0]
    _norm_modulate(x_ref, sh_ref, sc_ref, h_ref, i // tiles_per_batch, (i % tiles_per_batch) * tm, tm)
    logits = jnp.dot(h_ref[...], r_ref[...], preferred_element_type=F32, precision=lax.Precision.HIGHEST)
    lane = lax.broadcasted_iota(jnp.int32, logits.shape, 1).astype(F32)
    logits = jnp.where(lane < N_EXPERTS, logits, -jnp.inf)
    top1 = jnp.max(logits, axis=-1, keepdims=True)
    idx1 = jnp.min(jnp.where(logits == top1, lane, float(LANES)), axis=-1, keepdims=True)
    rest = jnp.where(lane == idx1, -jnp.inf, logits)
    top2 = jnp.max(rest, axis=-1, keepdims=True)
    idx2 = jnp.min(jnp.where(rest == top2, lane, float(LANES)), axis=-1, keepdims=True)
    e2 = jnp.exp(top2 - top1)
    den = 1.0 + e2
    o_ref[...] = jnp.where(lane == 0, 1.0 / den, jnp.where(lane == 1, e2 / den, jnp.where(
        lane == 2, idx1, jnp.where(lane == 3, idx2, 0.0))))


def _router_call(xs, mod, router_p, layer, tpb):
    t, d = xs.shape
    tm = _largest_divisor(tpb, (768, 384, 256, 128))
    mod_spec = lambda k: pl.BlockSpec((None, MOD_ROWS, d), lambda i: (layer, 0, k))
    kern = functools.partial(_router_kernel, tiles_per_batch=tpb // tm)
    return pl.pallas_call(
        kern,
        grid=(t // tm,),
        in_specs=[pl.BlockSpec((tm, d), lambda i: (i, 0)), mod_spec(3), mod_spec(4),
                  pl.BlockSpec((d, LANES), lambda i: (0, 0))],
        out_specs=[pl.BlockSpec((tm, d), lambda i: (i, 0)), pl.BlockSpec((tm, LANES), lambda i: (i, 0))],
        out_shape=[jax.ShapeDtypeStruct((t, d), F32), jax.ShapeDtypeStruct((t, LANES), F32)],
        compiler_params=_params(1),
        name="moe_router",
    )(xs, mod, mod, router_p)


def _moe_plan(route, tile):
    n_pairs = 2 * route.shape[0]
    n_tiles = -(-n_pairs // tile) + N_EXPERTS
    expert = route[:, 2:4].astype(jnp.int32).reshape(-1)
    onehot = (expert[:, None] == jnp.arange(N_EXPERTS, dtype=jnp.int32)[None]).astype(jnp.int32)
    csum = jnp.cumsum(onehot, axis=0)
    count = csum[-1]
    padded = (count + tile - 1) // tile * tile
    group_end = jnp.cumsum(padded)
    group_start = group_end - padded
    dest = jnp.sum(onehot * (group_start[None] + csum - 1), axis=1).astype(jnp.int32)
    tile_row = jnp.arange(n_tiles, dtype=jnp.int32) * tile
    tile_expert = jnp.minimum(jnp.sum((tile_row[:, None] >= group_end[None]).astype(jnp.int32), axis=1),
                              N_EXPERTS - 1).astype(jnp.int32)
    n_active = (group_end[-1] // tile).astype(jnp.int32).reshape(1)
    pad_start = jnp.concatenate([group_start + count, group_end[-1:]]).astype(jnp.int32)
    pad_end = jnp.concatenate([group_end, jnp.array([n_tiles * tile], jnp.int32)]).astype(jnp.int32)
    return dest, tile_expert, n_active, pad_start, pad_end, n_tiles


MOE_TOKEN_TILE = 256


def _dispatch_kernel(pad_lo_ref, pad_hi_ref, dest_ref, h_ref, rows_hbm, zero_ref, sem):
    tm = h_ref.shape[0]

    def row_copy(src_row, dst):
        return pltpu.make_async_copy(src_row, rows_hbm.at[pl.ds(dst, 1), :], sem.at[0])

    @pl.when(pl.program_id(0) == 0)
    def _():
        zero_ref[...] = jnp.zeros_like(zero_ref)
        zero_row = zero_ref.at[pl.ds(0, 1), :]
        for e in range(N_EXPERTS + 1):
            lo, hi = pad_lo_ref[e], pad_hi_ref[e]

            def start_zero(r, carry):
                row_copy(zero_row, r).start()
                return carry

            def wait_zero(r, carry):
                row_copy(zero_row, r).wait()
                return carry

            lax.fori_loop(lo, hi, start_zero, 0)
            lax.fori_loop(lo, hi, wait_zero, 0)

    def start(r, carry):
        for k in range(2):
            row_copy(h_ref.at[pl.ds(r, 1), :], dest_ref[0, 2 * r + k]).start()
        return carry

    def wait(r, carry):
        for k in range(2):
            row_copy(h_ref.at[pl.ds(r, 1), :], dest_ref[0, 2 * r + k]).wait()
        return carry

    lax.fori_loop(0, tm, start, 0)
    lax.fori_loop(0, tm, wait, 0)


def _dispatch_call(h, dest, pad_lo, pad_hi, n_rows):
    t, d = h.shape
    tm = MOE_TOKEN_TILE
    grid_spec = pltpu.PrefetchScalarGridSpec(
        num_scalar_prefetch=2,
        grid=(t // tm,),
        in_specs=[pl.BlockSpec((None, 1, 2 * tm), lambda i, lo, hi: (i, 0, 0), memory_space=pltpu.SMEM),
                  pl.BlockSpec((tm, d), lambda i, lo, hi: (i, 0))],
        out_specs=pl.BlockSpec(memory_space=pl.ANY),
        scratch_shapes=[pltpu.VMEM((8, d), F32), pltpu.SemaphoreType.DMA((1,))],
    )
    return pl.pallas_call(
        _dispatch_kernel,
        grid_spec=grid_spec,
        out_shape=jax.ShapeDtypeStruct((n_rows, d), F32),
        compiler_params=_params(1),
        name="moe_dispatch",
    )(pad_lo, pad_hi, dest.reshape(t // tm, 1, 2 * tm), h)


def _moe_group_kernel(expert_ref, n_active_ref, x_ref, wg_ref, wu_ref, wd_ref, o_ref, h_ref):
    m, f = pl.program_id(0), pl.program_id(1)

    @pl.when(f == 0)
    def _():
        h_ref[...] = x_ref[...].astype(BF16)
        o_ref[...] = jnp.zeros_like(o_ref)

    @pl.when(m < n_active_ref[0])
    def _():
        h = h_ref[...]
        g = jnp.dot(h, wg_ref[...], preferred_element_type=F32)
        u = jnp.dot(h, wu_ref[...], preferred_element_type=F32)
        act = (g * jax.nn.sigmoid(g) * u).astype(BF16)
        o_ref[...] += jnp.dot(act, wd_ref[...], preferred_element_type=F32)


def _moe_group_call(rows, tile_expert, n_active, w_gate_up, w_down, idx, tile):
    n_rows, d = rows.shape
    ff = w_down.shape[2]
    tf = _largest_divisor(ff, (512, 256, 128))
    n_f = ff // tf

    def f_eff(m, f, na):
        return jnp.where(m < na[0], f, n_f - 1)

    grid_spec = pltpu.PrefetchScalarGridSpec(
        num_scalar_prefetch=2,
        grid=(n_rows // tile, n_f),
        in_specs=[pl.BlockSpec((tile, d), lambda m, f, te, na: (m, 0)),
                  pl.BlockSpec((None, None, d, tf), lambda m, f, te, na: (idx, te[m], 0, f_eff(m, f, na))),
                  pl.BlockSpec((None, None, d, tf),
                               lambda m, f, te, na: (idx, te[m], 0, f_eff(m, f, na) + n_f)),
                  pl.BlockSpec((None, None, tf, d), lambda m, f, te, na: (idx, te[m], f_eff(m, f, na), 0))],
        out_specs=pl.BlockSpec((tile, d), lambda m, f, te, na: (m, 0)),
        scratch_shapes=[pltpu.VMEM((tile, d), BF16)],
    )
    return pl.pallas_call(
        _moe_group_kernel,
        grid_spec=grid_spec,
        out_shape=jax.ShapeDtypeStruct((n_rows, d), F32),
        compiler_params=_params(2),
        name="moe_experts",
    )(tile_expert, n_active, rows, w_gate_up, w_gate_up, w_down)


def _combine_kernel(dest_ref, x_ref, route_ref, gate_ref, y_hbm, o_ref, buf, sem, *, tiles_per_batch):
    i = pl.program_id(0)
    tm = x_ref.shape[0]

    def row_copy(r, k):
        return pltpu.make_async_copy(y_hbm.at[pl.ds(dest_ref[0, 2 * r + k], 1), :],
                                     buf.at[k, pl.ds(r, 1), :], sem.at[k])

    def start(r, carry):
        for k in range(2):
            row_copy(r, k).start()
        return carry

    def wait(r, carry):
        for k in range(2):
            row_copy(r, k).wait()
        return carry

    lax.fori_loop(0, tm, start, 0)
    lax.fori_loop(0, tm, wait, 0)
    route = route_ref[...]
    mix = route[:, 0:1] * buf[0] + route[:, 1:2] * buf[1]
    gate = _gate_rows(gate_ref, i // tiles_per_batch, (i % tiles_per_batch) * tm, tm)
    o_ref[...] = x_ref[...] + gate * mix


def _combine_call(xs, route, mod, y, dest, layer, tpb):
    t, d = xs.shape
    tm = MOE_TOKEN_TILE
    kern = functools.partial(_combine_kernel, tiles_per_batch=tpb // tm)
    return pl.pallas_call(
        kern,
        grid=(t // tm,),
        in_specs=[pl.BlockSpec((None, 1, 2 * tm), lambda i: (i, 0, 0), memory_space=pltpu.SMEM),
                  pl.BlockSpec((tm, d), lambda i: (i, 0)),
                  pl.BlockSpec((tm, LANES), lambda i: (i, 0)),
                  pl.BlockSpec((None, MOD_ROWS, d), lambda i: (layer, 0, 5)),
                  pl.BlockSpec(memory_space=pl.ANY)],
        out_specs=pl.BlockSpec((tm, d), lambda i: (i, 0)),
        out_shape=jax.ShapeDtypeStruct((t, d), F32),
        scratch_shapes=[pltpu.VMEM((2, tm, d), F32), pltpu.SemaphoreType.DMA((2,))],
        input_output_aliases={1: 0},
        compiler_params=_params(1),
        name="moe_combine",
    )(dest.reshape(t // tm, 1, 2 * tm), xs, route, mod, y)


def _rope_tables(seq, dim):
    rows = seq // GRID_W
    row = jnp.repeat(jnp.arange(rows), GRID_W).astype(F32)
    col = jnp.tile(jnp.arange(GRID_W), rows).astype(F32)
    quarter = dim // 4
    freqs = ROPE_THETA ** (-jnp.arange(quarter, dtype=F32) / quarter)
    ang_r = row[:, None] * freqs
    ang_c = col[:, None] * freqs
    ang = jnp.concatenate([ang_r, ang_r, ang_c, ang_c], axis=-1)
    cos, sin = jnp.cos(ang), jnp.sin(ang)
    zero = jnp.zeros_like(ang_r)
    sin_a = jnp.concatenate([-sin[:, :quarter], zero, -sin[:, 2 * quarter:3 * quarter], zero], axis=-1)
    sin_b = jnp.concatenate([zero, sin[:, quarter:2 * quarter], zero, sin[:, 3 * quarter:]], axis=-1)

    def full(tab, fill):
        tab = jnp.pad(tab, ((0, 0), (0, LANES - dim)), constant_values=fill)
        return jnp.concatenate([jnp.full((CTX_LEN, LANES), fill, F32), tab], axis=0)

    return full(cos, 1.0), full(sin_a, 0.0), full(sin_b, 0.0)


def _layout_w_in(w_in):
    sizes = (WIN_HEADS * HEAD_DIM, WIN_KV_HEADS * HEAD_DIM, WIN_KV_HEADS * HEAD_DIM,
             MLA_Q_RANK, MLA_KV_RANK, MLA_ROPE,
             GLB_HEADS * HEAD_DIM, GLB_KV_HEADS * HEAD_DIM, GLB_KV_HEADS * HEAD_DIM)
    offs = np.concatenate([[0], np.cumsum(sizes)])
    wq, wk, wv, mq, mkv, mkr, gq, gk, gv = (w_in[..., int(offs[k]):int(offs[k + 1])] for k in range(9))
    gates = w_in[..., int(offs[9]):]
    pad = jnp.zeros(w_in.shape[:-1] + (C_GATE - C_MKR - MLA_ROPE,), w_in.dtype)
    return jnp.concatenate([wq, wk, wv, gq, gk, gv, mq, mkv, mkr, pad, gates], axis=-1).astype(BF16)


def _pad_heads(w, width):
    w = w.reshape(w.shape[:-1] + (MLA_HEADS, width))
    w = jnp.pad(w, [(0, 0)] * (w.ndim - 1) + [(0, MLA_HEAD_PAD - width)])
    return w.reshape(w.shape[:-2] + (MLA_HEADS * MLA_HEAD_PAD,))


def kernel(x, c, ctx, c_ctx, w_mod, b_mod, w_in, win_q_norm, win_k_norm, win_sink, mla_q_lat_norm,
           mla_kv_lat_norm, mla_w_q_up, mla_w_kv_up, mla_q_norm, mla_k_norm, glb_q_norm, glb_k_norm,
           w_branch, w_out, ffn_w_gate_up, ffn_w_down, moe_router, moe_w_gate_up, moe_w_down):
    n_batch, seq, d = x.shape
    depth = w_mod.shape[0]
    tpb = CTX_LEN + seq
    assert ctx.shape[1] == CTX_LEN and seq % GRID_W == 0 and seq % BLOCK == 0 and seq >= 3 * BLOCK
    assert n_batch <= MOD_ROWS - 8 and C_MKR + LANES <= C_GATE

    xs = jnp.concatenate([ctx, x], axis=1).reshape(n_batch * tpb, d)
    cvec = jnp.zeros((MOD_ROWS, d), F32).at[:n_batch].set(c).at[MOD_ROWS - 8].set(c_ctx)
    mod = _mod_call(cvec, w_mod, b_mod)

    rope_h = _rope_tables(seq, HEAD_DIM)
    rope_r = _rope_tables(seq, MLA_ROPE)
    w_in_p = _layout_w_in(w_in)
    wq_up = _pad_heads(mla_w_q_up, MLA_NOPE + MLA_ROPE).astype(BF16)
    wkv_up = mla_w_kv_up.astype(BF16)
    pad_gain = lambda g: jnp.pad(g, (0, MLA_HEAD_PAD - g.shape[0]))[None]
    w_branch_b, w_out_b = w_branch.astype(BF16), w_out.astype(BF16)
    ffn_gu, ffn_dn = ffn_w_gate_up.astype(BF16), ffn_w_down.astype(BF16)
    moe_gu, moe_dn = moe_w_gate_up.astype(BF16), moe_w_down.astype(BF16)

    for layer in range(depth):
        last = False
        proj = _inproj_call(xs, mod, w_in_p, layer, tpb)
        k_win, k_glb = _kprep_call(proj, win_k_norm[layer][None], glb_k_norm[layer][None], rope_h, tpb)
        q_mla, k_mla, v_mla = _mla_prep_call(
            proj, wq_up, wkv_up, mla_q_lat_norm[layer][None], mla_kv_lat_norm[layer][None],
            pad_gain(mla_q_norm[layer]), pad_gain(mla_k_norm[layer]), rope_r, layer, tpb)
        o_win = _attention_call("win", (proj, C_WQ), (k_win, 0), (proj, C_WV),
                                dict(sink=win_sink[layer], q_gain=win_q_norm[layer][None], rope=rope_h),
                                n_batch, tpb, last)
        o_mla = _attention_call("mla", (q_mla, 0), (k_mla, 0), (v_mla, 0), {}, n_batch, tpb, last)
        o_glb = _attention_call("glb", (proj, C_GQ), (k_glb, 0), (proj, C_GV),
                                dict(q_gain=glb_q_norm[layer][None], rope=rope_h), n_batch, tpb, last)
        xs = _merge_call(xs, o_win, o_mla, o_glb, proj, mod, w_branch_b, w_out_b, layer, tpb)
        if layer % 2 == 0:
            xs = _ffn_call(xs, mod, ffn_gu, ffn_dn, layer // 2, layer, tpb)
        else:
            router_p = jnp.pad(moe_router[layer // 2], ((0, 0), (0, LANES - N_EXPERTS)))
            h, route = _router_call(xs, mod, router_p, layer, tpb)
            tile = 768 if 2 * h.shape[0] >= 16 * 768 else 128
            dest, tile_expert, n_active, pad_lo, pad_hi, n_tiles = _moe_plan(route, tile)
            rows = _dispatch_call(h, dest, pad_lo, pad_hi, n_tiles * tile)
            y = _moe_group_call(rows, tile_expert, n_active, moe_gu, moe_dn, layer // 2, tile)
            xs = _combine_call(xs, route, mod, y, dest, layer, tpb)
    return xs.reshape(n_batch, tpb, d)[:, CTX_LEN:, :]
```

```python
import functools

import jax
import jax.numpy as jnp
import numpy as np
from jax import lax
from jax.experimental import pallas as pl
from jax.experimental.pallas import tpu as pltpu

CTX_LEN = 256
GRID_W = 64
HEAD_DIM = 128
WIN_HEADS = 8
WIN_KV_HEADS = 2
WINDOW = 128
BLOCK = 128
MLA_HEADS = 8
MLA_NOPE = 128
MLA_ROPE = 64
MLA_V = 128
MLA_Q_RANK = 512
MLA_KV_RANK = 256
GLB_HEADS = 8
GLB_KV_HEADS = 2
BRANCH_W = 1024
N_BRANCH = 3
N_EXPERTS = 8
ROPE_THETA = 10000.0
EPS = 1e-6
NEG_INF = -1e30

LANES = 128
MLA_HEAD_PAD = 256
MOD_ROWS = 24
VMEM_LIMIT = 60 * 1024 * 1024

C_WQ = 0
C_GQ = C_WQ + WIN_HEADS * HEAD_DIM
C_WK = C_GQ + GLB_HEADS * HEAD_DIM
C_WV = C_WK + WIN_KV_HEADS * HEAD_DIM
C_GK = C_WV + WIN_KV_HEADS * HEAD_DIM
C_GV = C_GK + GLB_KV_HEADS * HEAD_DIM
C_MQ = C_GV + GLB_KV_HEADS * HEAD_DIM
C_MKV = C_MQ + MLA_Q_RANK
C_MKR = C_MKV + MLA_KV_RANK
C_GATE = 4096

BF16 = jnp.bfloat16
F32 = jnp.float32


def _largest_divisor(n, candidates):
    for cand in candidates:
        if n % cand == 0:
            return cand
    raise ValueError(f"no tile in {candidates} divides {n}")


def _params(n_axes):
    return pltpu.CompilerParams(dimension_semantics=("arbitrary",) * n_axes,
                                vmem_limit_bytes=VMEM_LIMIT)


def _rms(x, width):
    ms = jnp.sum(x * x, axis=-1, keepdims=True) * (1.0 / width)
    return x * lax.rsqrt(ms + EPS)


def _rope(y, cos, sin_a, sin_b, quarter):
    return (y * cos + pltpu.roll(y, LANES - quarter, 1) * sin_a + pltpu.roll(y, quarter, 1) * sin_b)


def _norm_modulate(x_ref, sh_ref, sc_ref, h_ref, batch, row0, n_rows):
    chunk = 16

    def body(ci, carry):
        r0 = pl.multiple_of(ci * chunk, chunk)
        xf = x_ref[pl.ds(r0, chunk), :]
        y = _rms(xf, xf.shape[-1])
        row = jnp.where(row0 + r0 < CTX_LEN, MOD_ROWS - 8, batch)
        sh = sh_ref[pl.ds(row, 1), :]
        sc = sc_ref[pl.ds(row, 1), :]
        h_ref[pl.ds(r0, chunk), :] = (y * (1.0 + sc) + sh).astype(h_ref.dtype)
        return carry

    lax.fori_loop(0, n_rows // chunk, body, 0)


def _gate_rows(g_ref, batch, row0, n_rows):
    rows = lax.broadcasted_iota(jnp.int32, (n_rows, 1), 0) + row0
    g_lat = g_ref[pl.ds(batch, 1), :]
    g_ctx = g_ref[pl.ds(MOD_ROWS - 8, 1), :]
    return jnp.where(rows < CTX_LEN, g_ctx, g_lat)


def _mod_kernel(c_ref, w_ref, b_ref, o_ref):
    cv = c_ref[...]
    sc = (cv * jax.nn.sigmoid(cv)).astype(BF16)
    o_ref[...] = jnp.dot(sc, w_ref[...].astype(BF16), preferred_element_type=F32) + b_ref[...]


def _mod_call(cvec, w_mod, b_mod):
    depth, d, n = w_mod.shape
    tn = _largest_divisor(n, (1024, 512, 256, 128))
    return pl.pallas_call(
        _mod_kernel,
        grid=(depth, n // tn),
        in_specs=[pl.BlockSpec((MOD_ROWS, d), lambda l, j: (0, 0)),
                  pl.BlockSpec((None, d, tn), lambda l, j: (l, 0, j)),
                  pl.BlockSpec((None, 1, tn), lambda l, j: (l, 0, j))],
        out_specs=pl.BlockSpec((None, MOD_ROWS, tn), lambda l, j: (l, 0, j)),
        out_shape=jax.ShapeDtypeStruct((depth, MOD_ROWS, n), F32),
        compiler_params=_params(2),
        name="mod_vectors",
    )(cvec, w_mod, b_mod.reshape(depth, 1, n))


def _inproj_kernel(x_ref, sh_ref, sc_ref, w_ref, o_ref, h_ref, *, tiles_per_batch):
    i = pl.program_id(0)
    tm = x_ref.shape[0]

    @pl.when(pl.program_id(1) == 0)
    def _():
        _norm_modulate(x_ref, sh_ref, sc_ref, h_ref, i // tiles_per_batch,
                       (i % tiles_per_batch) * tm, tm)

    o_ref[...] = jnp.dot(h_ref[...], w_ref[...], preferred_element_type=F32).astype(o_ref.dtype)


def _inproj_call(xs, mod, w_in_p, layer, tpb):
    t, d = xs.shape
    n = w_in_p.shape[-1]
    tm = _largest_divisor(tpb, (1152, 768, 384, 256, 128))
    tn = _largest_divisor(n, (1024, 512, 256, 128))
    kern = functools.partial(_inproj_kernel, tiles_per_batch=tpb // tm)
    return pl.pallas_call(
        kern,
        grid=(t // tm, n // tn),
        in_specs=[pl.BlockSpec((tm, d), lambda i, j: (i, 0)),
                  pl.BlockSpec((None, MOD_ROWS, d), lambda i, j: (layer, 0, 0)),
                  pl.BlockSpec((None, MOD_ROWS, d), lambda i, j: (layer, 0, 1)),
                  pl.BlockSpec((None, d, tn), lambda i, j: (layer, 0, j))],
        out_specs=pl.BlockSpec((tm, tn), lambda i, j: (i, j)),
        out_shape=jax.ShapeDtypeStruct((t, n), BF16),
        scratch_shapes=[pltpu.VMEM((tm, d), BF16)],
        compiler_params=_params(2),
        name="in_projection",
    )(xs, mod, mod, w_in_p)


def _gqa_prep_kernel(wq_ref, wk_ref, gq_ref, gk_ref, wqg_ref, wkg_ref, gqg_ref, gkg_ref,
                     cos_ref, sa_ref, sb_ref, qa_ref, ka_ref, qc_ref, kc_ref):
    cos, sa, sb = cos_ref[...], sa_ref[...], sb_ref[...]
    q_scale = HEAD_DIM ** -0.5 * LOG2E
    for src, gain, dst, scale in ((wq_ref, wqg_ref, qa_ref, q_scale), (wk_ref, wkg_ref, ka_ref, None),
                                  (gq_ref, gqg_ref, qc_ref, q_scale), (gk_ref, gkg_ref, kc_ref, None)):
        g = gain[...] if scale is None else gain[...] * scale
        for hd in range(src.shape[1] // HEAD_DIM):
            cols = slice(hd * HEAD_DIM, (hd + 1) * HEAD_DIM)
            y = _rms(src[:, cols].astype(F32), HEAD_DIM) * g
            dst[:, cols] = _rope(y, cos, sa, sb, HEAD_DIM // 4).astype(dst.dtype)


def _gqa_prep_call(proj, gains, rope_h, tpb):
    t = proj.shape[0]
    tm = _largest_divisor(tpb, (384, 256, 128))
    per_batch = tpb // tm
    qw, kw = WIN_HEADS * HEAD_DIM, WIN_KV_HEADS * HEAD_DIM
    row_spec = lambda width, col: pl.BlockSpec((tm, width), lambda i: (i, col // width))
    gain_spec = pl.BlockSpec((1, HEAD_DIM), lambda i: (0, 0))
    rope_spec = pl.BlockSpec((tm, LANES), lambda i: (i % per_batch, 0))
    q_out, k_out = jax.ShapeDtypeStruct((t, qw), BF16), jax.ShapeDtypeStruct((t, kw), BF16)
    return pl.pallas_call(
        _gqa_prep_kernel,
        grid=(t // tm,),
        in_specs=[row_spec(qw, C_WQ), row_spec(kw, C_WK), row_spec(qw, C_GQ), row_spec(kw, C_GK),
                  gain_spec, gain_spec, gain_spec, gain_spec, rope_spec, rope_spec, rope_spec],
        out_specs=[pl.BlockSpec((tm, qw), lambda i: (i, 0)), pl.BlockSpec((tm, kw), lambda i: (i, 0)),
                   pl.BlockSpec((tm, qw), lambda i: (i, 0)), pl.BlockSpec((tm, kw), lambda i: (i, 0))],
        out_shape=[q_out, k_out, q_out, k_out],
        compiler_params=_params(1),
        name="gqa_prep",
    )(proj, proj, proj, proj, *gains, *rope_h)


def _mla_prep_kernel(mq_ref, mkv_ref, mkr_ref, wq_ref, wkv_ref, qln_ref, kvln_ref, qn_ref, kn_ref,
                     cos_ref, sa_ref, sb_ref, q_out, k_out, v_out):
    width = MLA_NOPE + MLA_ROPE
    quarter = MLA_ROPE // 4
    cos, sa, sb = cos_ref[...], sa_ref[...], sb_ref[...]
    q_lat = (_rms(mq_ref[...].astype(F32), MLA_Q_RANK) * qln_ref[...]).astype(BF16)
    kv_lat = (_rms(mkv_ref[...].astype(F32), MLA_KV_RANK) * kvln_ref[...]).astype(BF16)
    qu = jnp.dot(q_lat, wq_ref[...], preferred_element_type=F32)
    kvu = jnp.dot(kv_lat, wkv_ref[...], preferred_element_type=F32)
    kr = mkr_ref[...].astype(F32)
    kr_sq = jnp.sum(kr * kr, axis=-1, keepdims=True)
    q_scale = width ** -0.5 * LOG2E
    for hd in range(MLA_HEADS):
        lo = hd * MLA_HEAD_PAD
        qa = qu[:, lo:lo + LANES]
        qb = qu[:, lo + LANES:lo + 2 * LANES]
        ms = (jnp.sum(qa * qa, axis=-1, keepdims=True) + jnp.sum(qb * qb, axis=-1, keepdims=True)) / width
        rstd = lax.rsqrt(ms + EPS) * q_scale
        q_out[:, lo:lo + LANES] = (qa * rstd * qn_ref[:, :LANES]).astype(q_out.dtype)
        q_out[:, lo + LANES:lo + 2 * LANES] = _rope(qb * rstd * qn_ref[:, LANES:], cos, sa, sb,
                                                     quarter).astype(q_out.dtype)
        kn = kvu[:, lo:lo + MLA_NOPE]
        ms = (jnp.sum(kn * kn, axis=-1, keepdims=True) + kr_sq) / width
        rstd = lax.rsqrt(ms + EPS)
        k_out[:, lo:lo + LANES] = (kn * rstd * kn_ref[:, :LANES]).astype(k_out.dtype)
        k_out[:, lo + LANES:lo + 2 * LANES] = _rope(kr * rstd * kn_ref[:, LANES:], cos, sa, sb,
                                                     quarter).astype(k_out.dtype)
        v_out[:, hd * MLA_V:(hd + 1) * MLA_V] = kvu[:, lo + MLA_NOPE:lo + MLA_NOPE + MLA_V].astype(v_out.dtype)


def _mla_prep_call(proj, wq_p, wkv, qln, kvln, qn_p, kn_p, rope_r, layer, tpb):
    t = proj.shape[0]
    tm = _largest_divisor(tpb, (768, 384, 256, 128))
    per_batch = tpb // tm
    hp = MLA_HEADS * MLA_HEAD_PAD
    rope_spec = pl.BlockSpec((tm, LANES), lambda i: (i % per_batch, 0))
    vec = lambda w: pl.BlockSpec((1, w), lambda i: (0, 0))
    return pl.pallas_call(
        _mla_prep_kernel,
        grid=(t // tm,),
        in_specs=[pl.BlockSpec((tm, MLA_Q_RANK), lambda i: (i, C_MQ // MLA_Q_RANK)),
                  pl.BlockSpec((tm, MLA_KV_RANK), lambda i: (i, C_MKV // MLA_KV_RANK)),
                  pl.BlockSpec((tm, LANES), lambda i: (i, C_MKR // LANES)),
                  pl.BlockSpec((None, MLA_Q_RANK, hp), lambda i: (layer, 0, 0)),
                  pl.BlockSpec((None, MLA_KV_RANK, hp), lambda i: (layer, 0, 0)),
                  vec(MLA_Q_RANK), vec(MLA_KV_RANK), vec(MLA_HEAD_PAD), vec(MLA_HEAD_PAD),
                  rope_spec, rope_spec, rope_spec],
        out_specs=[pl.BlockSpec((tm, hp), lambda i: (i, 0)),
                   pl.BlockSpec((tm, hp), lambda i: (i, 0)),
                   pl.BlockSpec((tm, MLA_HEADS * MLA_V), lambda i: (i, 0))],
        out_shape=[jax.ShapeDtypeStruct((t, hp), BF16), jax.ShapeDtypeStruct((t, hp), BF16),
                   jax.ShapeDtypeStruct((t, MLA_HEADS * MLA_V), BF16)],
        compiler_params=_params(1),
        name="mla_prep",
    )(proj, proj, proj, wq_p, wkv, qln, kvln, qn_p, kn_p, *rope_r)


LOG2E = 1.4426950408889634
KEY_CHUNK = 768


def _attention_kernel(*refs, mode, group, tq, seq, tile0):
    if mode == "win":
        sink_ref, q_ref, k_ref, v_ref, o_ref, vt_ref = refs
    else:
        q_ref, k_ref, v_ref, o_ref, vt_ref = refs
    tile = pl.program_id(2) + tile0
    hp, dv, tpb = vt_ref.shape
    dq = k_ref.shape[-1] // hp
    nq = group * tq
    heads = range(hp)

    @pl.when(pl.program_id(2) == 0)
    def _():
        for hh in heads:
            vt_ref[hh] = v_ref[:, hh * dv:(hh + 1) * dv].astype(F32).T.astype(BF16)

    q = [jnp.concatenate([q_ref[:, (hh * group + g) * dq:(hh * group + g + 1) * dq] for g in range(group)],
                         axis=0) for hh in heads]

    def init(hh):
        if mode == "win":
            first = (pl.program_id(1) * hp + hh) * group
            sink = jnp.concatenate(
                [jnp.full((1, tq), sink_ref[first + g] * LOG2E, F32) for g in range(group)], axis=1)
            return sink, jnp.ones((1, nq), F32), jnp.zeros((dv, nq), F32)
        return jnp.full((1, nq), NEG_INF, F32), jnp.zeros((1, nq), F32), jnp.zeros((dv, nq), F32)

    def scores(hh, rows):
        return lax.dot_general(k_ref[rows, hh * dq:(hh + 1) * dq], q[hh], (((1,), (1,)), ((), ())),
                               preferred_element_type=F32)

    def update(hh, carry, s, cols, mask=None):
        m, l, acc = carry
        if mask is not None:
            s = jnp.where(mask(s.shape), s, NEG_INF)
        m_new = jnp.maximum(m, jnp.max(s, axis=0, keepdims=True))
        alpha = jnp.exp2(m - m_new)
        p = jnp.exp2(s - m_new)
        l = alpha * l + jnp.sum(p, axis=0, keepdims=True)
        acc = alpha * acc + jnp.dot(vt_ref[hh, :, cols], p.astype(BF16), preferred_element_type=F32)
        return m_new, l, acc

    def write(hh, carry):
        _, l, acc = carry
        out = acc / l
        for g in range(group):
            col = (hh * group + g) * dv
            o_ref[:, col:col + dv] = out[:, g * tq:(g + 1) * tq].T.astype(o_ref.dtype)

    ctx_keys = slice(0, CTX_LEN)

    @pl.when(tile < CTX_LEN // tq)
    def _():
        s = [scores(hh, ctx_keys) for hh in heads]
        for hh in heads:
            write(hh, update(hh, init(hh), s[hh], ctx_keys))

    @pl.when(tile >= CTX_LEN // tq)
    def _():
        if mode == "win":
            blk = tile - CTX_LEN // tq
            start = jnp.clip((blk - 1) * BLOCK, 0, seq - 3 * BLOCK)
            local = pl.ds(pl.multiple_of(CTX_LEN + start, BLOCK), 3 * BLOCK)

            def band(shape):
                k_pos = start + lax.broadcasted_iota(jnp.int32, shape, 0)
                q_pos = blk * BLOCK + lax.broadcasted_iota(jnp.int32, shape, 1) % tq
                return jnp.abs(q_pos - k_pos) <= WINDOW

            s_ctx = [scores(hh, ctx_keys) for hh in heads]
            s_loc = [scores(hh, local) for hh in heads]
            carry = [update(hh, init(hh), s_ctx[hh], ctx_keys) for hh in heads]
            carry = [update(hh, carry[hh], s_loc[hh], local, band) for hh in heads]
        else:
            chunk = lambda c: slice(c * KEY_CHUNK, (c + 1) * KEY_CHUNK)
            n_chunks = tpb // KEY_CHUNK
            carry = [init(hh) for hh in heads]
            s = [scores(hh, chunk(0)) for hh in heads]
            for c in range(n_chunks):
                s_next = [scores(hh, chunk(c + 1)) for hh in heads] if c + 1 < n_chunks else None
                carry = [update(hh, carry[hh], s[hh], chunk(c)) for hh in heads]
                s = s_next
        for hh in heads:
            write(hh, carry[hh])


def _attention_call(mode, q_src, k_src, v_src, extras, n_batch, tpb, skip_ctx):
    t = q_src[0].shape[0]
    if mode == "mla":
        n_kv, group, dq, tq, hp = MLA_HEADS, 1, MLA_HEAD_PAD, 256, 4
    else:
        n_kv, group, dq, tq, hp = WIN_KV_HEADS, WIN_HEADS // WIN_KV_HEADS, HEAD_DIM, BLOCK, 2
    dv = HEAD_DIM
    tiles = tpb // tq
    tile0 = CTX_LEN // tq if skip_ctx else 0
    qw, kw, vw = hp * group * dq, hp * dq, hp * dv
    assert q_src[1] % qw == 0 and k_src[1] % kw == 0 and v_src[1] % vw == 0 and n_kv % hp == 0
    q_blk, k_blk, v_blk = q_src[1] // qw, k_src[1] // kw, v_src[1] // vw
    q_spec = pl.BlockSpec((tq, qw), lambda b, h, i: (b * tiles + i + tile0, q_blk + h))
    k_spec = pl.BlockSpec((tpb, kw), lambda b, h, i: (b, k_blk + h))
    v_spec = pl.BlockSpec((tpb, vw), lambda b, h, i: (b, v_blk + h))
    in_specs, args = [], []
    if mode == "win":
        in_specs.append(pl.BlockSpec(memory_space=pltpu.SMEM))
        args.append(extras["sink"])
    in_specs += [q_spec, k_spec, v_spec]
    args += [q_src[0], k_src[0], v_src[0]]
    kern = functools.partial(_attention_kernel, mode=mode, group=group, tq=tq, seq=tpb - CTX_LEN,
                             tile0=tile0)
    return pl.pallas_call(
        kern,
        grid=(n_batch, n_kv // hp, tiles - tile0),
        in_specs=in_specs,
        out_specs=pl.BlockSpec((tq, hp * group * dv), lambda b, h, i: (b * tiles + i + tile0, h)),
        out_shape=jax.ShapeDtypeStruct((t, n_kv * group * dv), BF16),
        scratch_shapes=[pltpu.VMEM((hp, dv, tpb), BF16)],
        compiler_params=_params(3),
        name=f"attention_{mode}",
    )(*args)


def _merge_kernel(ow_ref, om_ref, og_ref, gw_ref, gm_ref, gg_ref, wb_ref, wo_ref, x_ref, gate_ref,
                  o_ref, *, tiles_per_batch):
    i, j = pl.program_id(0), pl.program_id(1)
    tm = x_ref.shape[0]

    @pl.when(j == 0)
    def _():
        o_ref[...] = jnp.zeros_like(o_ref)

    y = None
    for br, (o_br, g_br) in enumerate(((ow_ref, gw_ref), (om_ref, gm_ref), (og_ref, gg_ref))):
        term = jax.nn.sigmoid(g_br[...].astype(F32)) * jnp.dot(o_br[...], wb_ref[br],
                                                                preferred_element_type=F32)
        y = term if y is None else y + term
    o_ref[...] += jnp.dot(y.astype(BF16), wo_ref[...], preferred_element_type=F32)

    @pl.when(j == pl.num_programs(1) - 1)
    def _():
        gate = _gate_rows(gate_ref, i // tiles_per_batch, (i % tiles_per_batch) * tm, tm)
        o_ref[...] = x_ref[...] + gate * o_ref[...]


def _merge_call(xs, o_win, o_mla, o_glb, proj, mod, w_branch, w_out, layer, tpb):
    t, d = xs.shape
    tm = _largest_divisor(tpb, (768, 384, 256, 128))
    tn = _largest_divisor(d, (512, 256, 128))
    o_spec = pl.BlockSpec((tm, BRANCH_W), lambda i, j: (i, 0))
    gate_spec = lambda br: pl.BlockSpec((tm, tn), lambda i, j: (i, (C_GATE + br * d) // tn + j))
    kern = functools.partial(_merge_kernel, tiles_per_batch=tpb // tm)
    return pl.pallas_call(
        kern,
        grid=(t // tm, d // tn),
        in_specs=[o_spec, o_spec, o_spec, gate_spec(0), gate_spec(1), gate_spec(2),
                  pl.BlockSpec((None, N_BRANCH, BRANCH_W, tn), lambda i, j: (layer, 0, 0, j)),
                  pl.BlockSpec((None, tn, d), lambda i, j: (layer, j, 0)),
                  pl.BlockSpec((tm, d), lambda i, j: (i, 0)),
                  pl.BlockSpec((None, MOD_ROWS, d), lambda i, j: (layer, 0, 2))],
        out_specs=pl.BlockSpec((tm, d), lambda i, j: (i, 0)),
        out_shape=jax.ShapeDtypeStruct((t, d), F32),
        input_output_aliases={8: 0},
        compiler_params=_params(2),
        name="merge_out_projection",
    )(o_win, o_mla, o_glb, proj, proj, proj, w_branch, w_out, xs, mod)


def _ffn_kernel(x_ref, sh_ref, sc_ref, gate_ref, wg_ref, wu_ref, wd_ref, o_ref, h_ref, *, tiles_per_batch):
    i, f = pl.program_id(0), pl.program_id(1)
    tm = x_ref.shape[0]
    batch, row0 = i // tiles_per_batch, (i % tiles_per_batch) * tm

    @pl.when(f == 0)
    def _():
        _norm_modulate(x_ref, sh_ref, sc_ref, h_ref, batch, row0, tm)
        o_ref[...] = jnp.zeros_like(o_ref)

    h = h_ref[...]
    g = jnp.dot(h, wg_ref[...], preferred_element_type=F32)
    u = jnp.dot(h, wu_ref[...], preferred_element_type=F32)
    act = (g * jax.nn.sigmoid(g) * u).astype(BF16)
    o_ref[...] += jnp.dot(act, wd_ref[...], preferred_element_type=F32)

    @pl.when(f == pl.num_programs(1) - 1)
    def _():
        o_ref[...] = x_ref[...] + _gate_rows(gate_ref, batch, row0, tm) * o_ref[...]


def _ffn_call(xs, mod, w_gate_up, w_down, idx, layer, tpb):
    t, d = xs.shape
    ff = w_down.shape[1]
    tm = _largest_divisor(tpb, (768, 384, 256, 128))
    tf = _largest_divisor(ff, (512, 256, 128))
    n_f = ff // tf
    mod_spec = lambda k: pl.BlockSpec((None, MOD_ROWS, d), lambda i, f: (layer, 0, k))
    kern = functools.partial(_ffn_kernel, tiles_per_batch=tpb // tm)
    return pl.pallas_call(
        kern,
        grid=(t // tm, n_f),
        in_specs=[pl.BlockSpec((tm, d), lambda i, f: (i, 0)), mod_spec(3), mod_spec(4), mod_spec(5),
                  pl.BlockSpec((None, d, tf), lambda i, f: (idx, 0, f)),
                  pl.BlockSpec((None, d, tf), lambda i, f: (idx, 0, f + n_f)),
                  pl.BlockSpec((None, tf, d), lambda i, f: (idx, f, 0))],
        out_specs=pl.BlockSpec((tm, d), lambda i, f: (i, 0)),
        out_shape=jax.ShapeDtypeStruct((t, d), F32),
        scratch_shapes=[pltpu.VMEM((tm, d), BF16)],
        input_output_aliases={0: 0},
        compiler_params=_params(2),
        name="swiglu_ffn",
    )(xs, mod, mod, mod, w_gate_up, w_gate_up, w_down)


def _router_kernel(x_ref, sh_ref, sc_ref, r_ref, h_ref, o_ref, *, tiles_per_batch):
    i = pl.program_id(0)
    tm = x_ref.shape[0]
    _norm_modulate(x_ref, sh_ref, sc_ref, h_ref, i // tiles_per_batch, (i % tiles_per_batch) * tm, tm)
    logits = jnp.dot(h_ref[...], r_ref[...], preferred_element_type=F32, precision=lax.Precision.HIGHEST)
    lane = lax.broadcasted_iota(jnp.int32, logits.shape, 1).astype(F32)
    logits = jnp.where(lane < N_EXPERTS, logits, -jnp.inf)
    top1 = jnp.max(logits, axis=-1, keepdims=True)
    idx1 = jnp.min(jnp.where(logits == top1, lane, float(LANES)), axis=-1, keepdims=True)
    rest = jnp.where(lane == idx1, -jnp.inf, logits)
    top2 = jnp.max(rest, axis=-1, keepdims=True)
    idx2 = jnp.min(jnp.where(rest == top2, lane, float(LANES)), axis=-1, keepdims=True)
    e2 = jnp.exp(top2 - top1)
    den = 1.0 + e2
    o_ref[...] = jnp.where(lane == 0, 1.0 / den, jnp.where(lane == 1, e2 / den, jnp.where(
        lane == 2, idx1, jnp.where(lane == 3, idx2, 0.0))))


def _router_call(xs, mod, router_p, layer, tpb):
    t, d = xs.shape
    tm = _largest_divisor(tpb, (768, 384, 256, 128))
    mod_spec = lambda k: pl.BlockSpec((None, MOD_ROWS, d), lambda i: (layer, 0, k))
    kern = functools.partial(_router_kernel, tiles_per_batch=tpb // tm)
    return pl.pallas_call(
        kern,
        grid=(t // tm,),
        in_specs=[pl.BlockSpec((tm, d), lambda i: (i, 0)), mod_spec(3), mod_spec(4),
                  pl.BlockSpec((d, LANES), lambda i: (0, 0))],
        out_specs=[pl.BlockSpec((tm, d), lambda i: (i, 0)), pl.BlockSpec((tm, LANES), lambda i: (i, 0))],
        out_shape=[jax.ShapeDtypeStruct((t, d), F32), jax.ShapeDtypeStruct((t, LANES), F32)],
        compiler_params=_params(1),
        name="moe_router",
    )(xs, mod, mod, router_p)


def _moe_plan(route, tile):
    n_pairs = 2 * route.shape[0]
    n_tiles = -(-n_pairs // tile) + N_EXPERTS
    expert = route[:, 2:4].astype(jnp.int32).reshape(-1)
    onehot = (expert[:, None] == jnp.arange(N_EXPERTS, dtype=jnp.int32)[None]).astype(jnp.int32)
    csum = jnp.cumsum(onehot, axis=0)
    count = csum[-1]
    padded = (count + tile - 1) // tile * tile
    group_end = jnp.cumsum(padded)
    group_start = group_end - padded
    dest = jnp.sum(onehot * (group_start[None] + csum - 1), axis=1).astype(jnp.int32)
    tile_row = jnp.arange(n_tiles, dtype=jnp.int32) * tile
    tile_expert = jnp.minimum(jnp.sum((tile_row[:, None] >= group_end[None]).astype(jnp.int32), axis=1),
                              N_EXPERTS - 1).astype(jnp.int32)
    n_active = (group_end[-1] // tile).astype(jnp.int32).reshape(1)
    pad_start = jnp.concatenate([group_start + count, group_end[-1:]]).astype(jnp.int32)
    pad_end = jnp.concatenate([group_end, jnp.array([n_tiles * tile], jnp.int32)]).astype(jnp.int32)
    return dest, tile_expert, n_active, pad_start, pad_end, n_tiles


MOE_TOKEN_TILE = 256


def _dispatch_kernel(pad_lo_ref, pad_hi_ref, dest_ref, h_ref, rows_hbm, zero_ref, sem):
    tm = h_ref.shape[0]

    def row_copy(src_row, dst):
        return pltpu.make_async_copy(src_row, rows_hbm.at[pl.ds(dst, 1), :], sem.at[0])

    @pl.when(pl.program_id(0) == 0)
    def _():
        zero_ref[...] = jnp.zeros_like(zero_ref)
        zero_row = zero_ref.at[pl.ds(0, 1), :]
        for e in range(N_EXPERTS + 1):
            lo, hi = pad_lo_ref[e], pad_hi_ref[e]

            def start_zero(r, carry):
                row_copy(zero_row, r).start()
                return carry

            def wait_zero(r, carry):
                row_copy(zero_row, r).wait()
                return carry

            lax.fori_loop(lo, hi, start_zero, 0)
            lax.fori_loop(lo, hi, wait_zero, 0)

    def start(r, carry):
        for k in range(2):
            row_copy(h_ref.at[pl.ds(r, 1), :], dest_ref[0, 2 * r + k]).start()
        return carry

    def wait(r, carry):
        for k in range(2):
            row_copy(h_ref.at[pl.ds(r, 1), :], dest_ref[0, 2 * r + k]).wait()
        return carry

    lax.fori_loop(0, tm, start, 0)
    lax.fori_loop(0, tm, wait, 0)


def _dispatch_call(h, dest, pad_lo, pad_hi, n_rows):
    t, d = h.shape
    tm = MOE_TOKEN_TILE
    grid_spec = pltpu.PrefetchScalarGridSpec(
        num_scalar_prefetch=2,
        grid=(t // tm,),
        in_specs=[pl.BlockSpec((None, 1, 2 * tm), lambda i, lo, hi: (i, 0, 0), memory_space=pltpu.SMEM),
                  pl.BlockSpec((tm, d), lambda i, lo, hi: (i, 0))],
        out_specs=pl.BlockSpec(memory_space=pl.ANY),
        scratch_shapes=[pltpu.VMEM((8, d), F32), pltpu.SemaphoreType.DMA((1,))],
    )
    return pl.pallas_call(
        _dispatch_kernel,
        grid_spec=grid_spec,
        out_shape=jax.ShapeDtypeStruct((n_rows, d), F32),
        compiler_params=_params(1),
        name="moe_dispatch",
    )(pad_lo, pad_hi, dest.reshape(t // tm, 1, 2 * tm), h)


def _moe_group_kernel(expert_ref, n_active_ref, x_ref, wg_ref, wu_ref, wd_ref, o_ref, h_ref):
    m, f = pl.program_id(0), pl.program_id(1)

    @pl.when(f == 0)
    def _():
        h_ref[...] = x_ref[...].astype(BF16)
        o_ref[...] = jnp.zeros_like(o_ref)

    @pl.when(m < n_active_ref[0])
    def _():
        h = h_ref[...]
        g = jnp.dot(h, wg_ref[...], preferred_element_type=F32)
        u = jnp.dot(h, wu_ref[...], preferred_element_type=F32)
        act = (g * jax.nn.sigmoid(g) * u).astype(BF16)
        o_ref[...] += jnp.dot(act, wd_ref[...], preferred_element_type=F32)


def _moe_group_call(rows, tile_expert, n_active, w_gate_up, w_down, idx, tile):
    n_rows, d = rows.shape
    ff = w_down.shape[2]
    tf = _largest_divisor(ff, (512, 256, 128))
    n_f = ff // tf

    def f_eff(m, f, na):
        return jnp.where(m < na[0], f, n_f - 1)

    grid_spec = pltpu.PrefetchScalarGridSpec(
        num_scalar_prefetch=2,
        grid=(n_rows // tile, n_f),
        in_specs=[pl.BlockSpec((tile, d), lambda m, f, te, na: (m, 0)),
                  pl.BlockSpec((None, None, d, tf), lambda m, f, te, na: (idx, te[m], 0, f_eff(m, f, na))),
                  pl.BlockSpec((None, None, d, tf),
                               lambda m, f, te, na: (idx, te[m], 0, f_eff(m, f, na) + n_f)),
                  pl.BlockSpec((None, None, tf, d), lambda m, f, te, na: (idx, te[m], f_eff(m, f, na), 0))],
        out_specs=pl.BlockSpec((tile, d), lambda m, f, te, na: (m, 0)),
        scratch_shapes=[pltpu.VMEM((tile, d), BF16)],
    )
    return pl.pallas_call(
        _moe_group_kernel,
        grid_spec=grid_spec,
        out_shape=jax.ShapeDtypeStruct((n_rows, d), F32),
        compiler_params=_params(2),
        name="moe_experts",
    )(tile_expert, n_active, rows, w_gate_up, w_gate_up, w_down)


def _combine_kernel(dest_ref, x_ref, route_ref, gate_ref, y_hbm, o_ref, buf, sem, *, tiles_per_batch):
    i = pl.program_id(0)
    tm = x_ref.shape[0]

    def row_copy(r, k):
        return pltpu.make_async_copy(y_hbm.at[pl.ds(dest_ref[0, 2 * r + k], 1), :],
                                     buf.at[k, pl.ds(r, 1), :], sem.at[k])

    def start(r, carry):
        for k in range(2):
            row_copy(r, k).start()
        return carry

    def wait(r, carry):
        for k in range(2):
            row_copy(r, k).wait()
        return carry

    lax.fori_loop(0, tm, start, 0)
    lax.fori_loop(0, tm, wait, 0)
    route = route_ref[...]
    mix = route[:, 0:1] * buf[0] + route[:, 1:2] * buf[1]
    gate = _gate_rows(gate_ref, i // tiles_per_batch, (i % tiles_per_batch) * tm, tm)
    o_ref[...] = x_ref[...] + gate * mix


def _combine_call(xs, route, mod, y, dest, layer, tpb):
    t, d = xs.shape
    tm = MOE_TOKEN_TILE
    kern = functools.partial(_combine_kernel, tiles_per_batch=tpb // tm)
    return pl.pallas_call(
        kern,
        grid=(t // tm,),
        in_specs=[pl.BlockSpec((None, 1, 2 * tm), lambda i: (i, 0, 0), memory_space=pltpu.SMEM),
                  pl.BlockSpec((tm, d), lambda i: (i, 0)),
                  pl.BlockSpec((tm, LANES), lambda i: (i, 0)),
                  pl.BlockSpec((None, MOD_ROWS, d), lambda i: (layer, 0, 5)),
                  pl.BlockSpec(memory_space=pl.ANY)],
        out_specs=pl.BlockSpec((tm, d), lambda i: (i, 0)),
        out_shape=jax.ShapeDtypeStruct((t, d), F32),
        scratch_shapes=[pltpu.VMEM((2, tm, d), F32), pltpu.SemaphoreType.DMA((2,))],
        input_output_aliases={1: 0},
        compiler_params=_params(1),
        name="moe_combine",
    )(dest.reshape(t // tm, 1, 2 * tm), xs, route, mod, y)


def _rope_tables(seq, dim):
    rows = seq // GRID_W
    row = jnp.repeat(jnp.arange(rows), GRID_W).astype(F32)
    col = jnp.tile(jnp.arange(GRID_W), rows).astype(F32)
    quarter = dim // 4
    freqs = ROPE_THETA ** (-jnp.arange(quarter, dtype=F32) / quarter)
    ang_r = row[:, None] * freqs
    ang_c = col[:, None] * freqs
    ang = jnp.concatenate([ang_r, ang_r, ang_c, ang_c], axis=-1)
    cos, sin = jnp.cos(ang), jnp.sin(ang)
    zero = jnp.zeros_like(ang_r)
    sin_a = jnp.concatenate([-sin[:, :quarter], zero, -sin[:, 2 * quarter:3 * quarter], zero], axis=-1)
    sin_b = jnp.concatenate([zero, sin[:, quarter:2 * quarter], zero, sin[:, 3 * quarter:]], axis=-1)

    def full(tab, fill):
        tab = jnp.pad(tab, ((0, 0), (0, LANES - dim)), constant_values=fill)
        return jnp.concatenate([jnp.full((CTX_LEN, LANES), fill, F32), tab], axis=0)

    return full(cos, 1.0), full(sin_a, 0.0), full(sin_b, 0.0)


def _layout_w_in(w_in):
    sizes = (WIN_HEADS * HEAD_DIM, WIN_KV_HEADS * HEAD_DIM, WIN_KV_HEADS * HEAD_DIM,
             MLA_Q_RANK, MLA_KV_RANK, MLA_ROPE,
             GLB_HEADS * HEAD_DIM, GLB_KV_HEADS * HEAD_DIM, GLB_KV_HEADS * HEAD_DIM)
    offs = np.concatenate([[0], np.cumsum(sizes)])
    wq, wk, wv, mq, mkv, mkr, gq, gk, gv = (w_in[..., int(offs[k]):int(offs[k + 1])] for k in range(9))
    gates = w_in[..., int(offs[9]):]
    pad = jnp.zeros(w_in.shape[:-1] + (C_GATE - C_MKR - MLA_ROPE,), w_in.dtype)
    return jnp.concatenate([wq, gq, wk, wv, gk, gv, mq, mkv, mkr, pad, gates], axis=-1).astype(BF16)


def _pad_heads(w, width):
    w = w.reshape(w.shape[:-1] + (MLA_HEADS, width))
    w = jnp.pad(w, [(0, 0)] * (w.ndim - 1) + [(0, MLA_HEAD_PAD - width)])
    return w.reshape(w.shape[:-2] + (MLA_HEADS * MLA_HEAD_PAD,))


def kernel(x, c, ctx, c_ctx, w_mod, b_mod, w_in, win_q_norm, win_k_norm, win_sink, mla_q_lat_norm,
           mla_kv_lat_norm, mla_w_q_up, mla_w_kv_up, mla_q_norm, mla_k_norm, glb_q_norm, glb_k_norm,
           w_branch, w_out, ffn_w_gate_up, ffn_w_down, moe_router, moe_w_gate_up, moe_w_down):
    n_batch, seq, d = x.shape
    depth = w_mod.shape[0]
    tpb = CTX_LEN + seq
    assert ctx.shape[1] == CTX_LEN and seq % GRID_W == 0 and seq % BLOCK == 0 and seq >= 3 * BLOCK
    assert n_batch <= MOD_ROWS - 8 and C_MKR + LANES <= C_GATE

    xs = jnp.concatenate([ctx, x], axis=1).reshape(n_batch * tpb, d)
    cvec = jnp.zeros((MOD_ROWS, d), F32).at[:n_batch].set(c).at[MOD_ROWS - 8].set(c_ctx)
    mod = _mod_call(cvec, w_mod, b_mod)

    rope_h = _rope_tables(seq, HEAD_DIM)
    rope_r = _rope_tables(seq, MLA_ROPE)
    w_in_p = _layout_w_in(w_in)
    wq_up = _pad_heads(mla_w_q_up, MLA_NOPE + MLA_ROPE).astype(BF16)
    wkv_up = mla_w_kv_up.astype(BF16)
    pad_gain = lambda g: jnp.pad(g, (0, MLA_HEAD_PAD - g.shape[0]))[None]
    w_branch_b, w_out_b = w_branch.astype(BF16), w_out.astype(BF16)
    ffn_gu, ffn_dn = ffn_w_gate_up.astype(BF16), ffn_w_down.astype(BF16)
    moe_gu, moe_dn = moe_w_gate_up.astype(BF16), moe_w_down.astype(BF16)

    for layer in range(depth):
        last = False
        proj = _inproj_call(xs, mod, w_in_p, layer, tpb)
        q_win, k_win, q_glb, k_glb = _gqa_prep_call(
            proj, [g[layer][None] for g in (win_q_norm, win_k_norm, glb_q_norm, glb_k_norm)], rope_h, tpb)
        q_mla, k_mla, v_mla = _mla_prep_call(
            proj, wq_up, wkv_up, mla_q_lat_norm[layer][None], mla_kv_lat_norm[layer][None],
            pad_gain(mla_q_norm[layer]), pad_gain(mla_k_norm[layer]), rope_r, layer, tpb)
        o_win = _attention_call("win", (q_win, 0), (k_win, 0), (proj, C_WV), dict(sink=win_sink[layer]),
                                n_batch, tpb, last)
        o_mla = _attention_call("mla", (q_mla, 0), (k_mla, 0), (v_mla, 0), {}, n_batch, tpb, last)
        o_glb = _attention_call("glb", (q_glb, 0), (k_glb, 0), (proj, C_GV), {}, n_batch, tpb, last)
        xs = _merge_call(xs, o_win, o_mla, o_glb, proj, mod, w_branch_b, w_out_b, layer, tpb)
        if layer % 2 == 0:
            xs = _ffn_call(xs, mod, ffn_gu, ffn_dn, layer // 2, layer, tpb)
        else:
            router_p = jnp.pad(moe_router[layer // 2], ((0, 0), (0, LANES - N_EXPERTS)))
            h, route = _router_call(xs, mod, router_p, layer, tpb)
            tile = 768 if 2 * h.shape[0] >= 16 * 768 else 128
            dest, tile_expert, n_active, pad_lo, pad_hi, n_tiles = _moe_plan(route, tile)
            rows = _dispatch_call(h, dest, pad_lo, pad_hi, n_tiles * tile)
            y = _moe_group_call(rows, tile_expert, n_active, moe_gu, moe_dn, layer // 2, tile)
            xs = _combine_call(xs, route, mod, y, dest, layer, tpb)
    return xs.reshape(n_batch, tpb, d)[:, CTX_LEN:, :]
```

```python
import functools

import jax
import jax.numpy as jnp
import numpy as np
from jax import lax
from jax.experimental import pallas as pl
from jax.experimental.pallas import tpu as pltpu

CTX_LEN = 256
GRID_W = 64
HEAD_DIM = 128
WIN_HEADS = 8
WIN_KV_HEADS = 2
WINDOW = 128
BLOCK = 128
MLA_HEADS = 8
MLA_NOPE = 128
MLA_ROPE = 64
MLA_V = 128
MLA_Q_RANK = 512
MLA_KV_RANK = 256
GLB_HEADS = 8
GLB_KV_HEADS = 2
BRANCH_W = 1024
N_BRANCH = 3
N_EXPERTS = 8
ROPE_THETA = 10000.0
EPS = 1e-6
NEG_INF = -1e30

LANES = 128
MLA_HEAD_PAD = 256
MOD_ROWS = 24
VMEM_LIMIT = 60 * 1024 * 1024

C_WQ = 0
C_GQ = C_WQ + WIN_HEADS * HEAD_DIM
C_WK = C_GQ + GLB_HEADS * HEAD_DIM
C_WV = C_WK + WIN_KV_HEADS * HEAD_DIM
C_GK = C_WV + WIN_KV_HEADS * HEAD_DIM
C_GV = C_GK + GLB_KV_HEADS * HEAD_DIM
C_MQ = C_GV + GLB_KV_HEADS * HEAD_DIM
C_MKV = C_MQ + MLA_Q_RANK
C_MKR = C_MKV + MLA_KV_RANK
C_GATE = 4096

BF16 = jnp.bfloat16
F32 = jnp.float32


def _largest_divisor(n, candidates):
    for cand in candidates:
        if n % cand == 0:
            return cand
    raise ValueError(f"no tile in {candidates} divides {n}")


def _params(n_axes):
    return pltpu.CompilerParams(dimension_semantics=("arbitrary",) * n_axes,
                                vmem_limit_bytes=VMEM_LIMIT)


def _rms(x, width):
    ms = jnp.sum(x * x, axis=-1, keepdims=True) * (1.0 / width)
    return x * lax.rsqrt(ms + EPS)


def _rope(y, cos, sin):
    return y * cos + pltpu.roll(y, LANES // 2, 1) * sin


def _norm_modulate(x_ref, sh_ref, sc_ref, h_ref, batch, row0, n_rows):
    chunk = 16

    def body(ci, carry):
        r0 = pl.multiple_of(ci * chunk, chunk)
        xf = x_ref[pl.ds(r0, chunk), :]
        y = _rms(xf, xf.shape[-1])
        row = jnp.where(row0 + r0 < CTX_LEN, MOD_ROWS - 8, batch)
        sh = sh_ref[pl.ds(row, 1), :]
        sc = sc_ref[pl.ds(row, 1), :]
        h_ref[pl.ds(r0, chunk), :] = (y * (1.0 + sc) + sh).astype(h_ref.dtype)
        return carry

    lax.fori_loop(0, n_rows // chunk, body, 0)


def _gate_rows(g_ref, batch, row0, n_rows):
    rows = lax.broadcasted_iota(jnp.int32, (n_rows, 1), 0) + row0
    g_lat = g_ref[pl.ds(batch, 1), :]
    g_ctx = g_ref[pl.ds(MOD_ROWS - 8, 1), :]
    return jnp.where(rows < CTX_LEN, g_ctx, g_lat)


def _mod_kernel(c_ref, w_ref, b_ref, o_ref):
    cv = c_ref[...]
    sc = (cv * jax.nn.sigmoid(cv)).astype(BF16)
    o_ref[...] = jnp.dot(sc, w_ref[...].astype(BF16), preferred_element_type=F32) + b_ref[...]


def _mod_call(cvec, w_mod, b_mod):
    depth, d, n = w_mod.shape
    tn = _largest_divisor(n, (1024, 512, 256, 128))
    return pl.pallas_call(
        _mod_kernel,
        grid=(depth, n // tn),
        in_specs=[pl.BlockSpec((MOD_ROWS, d), lambda l, j: (0, 0)),
                  pl.BlockSpec((None, d, tn), lambda l, j: (l, 0, j)),
                  pl.BlockSpec((None, 1, tn), lambda l, j: (l, 0, j))],
        out_specs=pl.BlockSpec((None, MOD_ROWS, tn), lambda l, j: (l, 0, j)),
        out_shape=jax.ShapeDtypeStruct((depth, MOD_ROWS, n), F32),
        compiler_params=_params(2),
        name="mod_vectors",
    )(cvec, w_mod, b_mod.reshape(depth, 1, n))


def _inproj_kernel(x_ref, sh_ref, sc_ref, w_ref, o_ref, h_ref, *, tiles_per_batch):
    i = pl.program_id(0)
    tm = x_ref.shape[0]

    @pl.when(pl.program_id(1) == 0)
    def _():
        _norm_modulate(x_ref, sh_ref, sc_ref, h_ref, i // tiles_per_batch,
                       (i % tiles_per_batch) * tm, tm)

    o_ref[...] = jnp.dot(h_ref[...], w_ref[...], preferred_element_type=F32).astype(o_ref.dtype)


def _inproj_call(xs, mod, w_in_p, layer, tpb):
    t, d = xs.shape
    n = w_in_p.shape[-1]
    tm = _largest_divisor(tpb, (768, 384, 256, 128))
    tn = _largest_divisor(n, (1024, 512, 256, 128))
    kern = functools.partial(_inproj_kernel, tiles_per_batch=tpb // tm)
    return pl.pallas_call(
        kern,
        grid=(t // tm, n // tn),
        in_specs=[pl.BlockSpec((tm, d), lambda i, j: (i, 0)),
                  pl.BlockSpec((None, MOD_ROWS, d), lambda i, j: (layer, 0, 0)),
                  pl.BlockSpec((None, MOD_ROWS, d), lambda i, j: (layer, 0, 1)),
                  pl.BlockSpec((None, d, tn), lambda i, j: (layer, 0, j))],
        out_specs=pl.BlockSpec((tm, tn), lambda i, j: (i, j)),
        out_shape=jax.ShapeDtypeStruct((t, n), BF16),
        scratch_shapes=[pltpu.VMEM((tm, d), BF16)],
        compiler_params=_params(2),
        name="in_projection",
    )(xs, mod, mod, w_in_p)


def _gqa_prep_kernel(wq_ref, wk_ref, gq_ref, gk_ref, wqg_ref, wkg_ref, gqg_ref, gkg_ref,
                     cos_ref, sin_ref, qa_ref, ka_ref, qc_ref, kc_ref):
    cos, sin = cos_ref[...], sin_ref[...]
    q_scale = HEAD_DIM ** -0.5 * LOG2E
    for src, gain, dst, scale in ((wq_ref, wqg_ref, qa_ref, q_scale), (wk_ref, wkg_ref, ka_ref, None),
                                  (gq_ref, gqg_ref, qc_ref, q_scale), (gk_ref, gkg_ref, kc_ref, None)):
        g = gain[...] if scale is None else gain[...] * scale
        for hd in range(src.shape[1] // HEAD_DIM):
            cols = slice(hd * HEAD_DIM, (hd + 1) * HEAD_DIM)
            y = _rms(src[:, cols].astype(F32), HEAD_DIM) * g
            dst[:, cols] = _rope(y, cos, sin).astype(dst.dtype)


def _gqa_prep_call(proj, gains, rope_h, tpb):
    t = proj.shape[0]
    tm = _largest_divisor(tpb, (384, 256, 128))
    per_batch = tpb // tm
    qw, kw = WIN_HEADS * HEAD_DIM, WIN_KV_HEADS * HEAD_DIM
    row_spec = lambda width, col: pl.BlockSpec((tm, width), lambda i: (i, col // width))
    gain_spec = pl.BlockSpec((1, HEAD_DIM), lambda i: (0, 0))
    rope_spec = pl.BlockSpec((tm, LANES), lambda i: (i % per_batch, 0))
    q_out, k_out = jax.ShapeDtypeStruct((t, qw), BF16), jax.ShapeDtypeStruct((t, kw), BF16)
    return pl.pallas_call(
        _gqa_prep_kernel,
        grid=(t // tm,),
        in_specs=[row_spec(qw, C_WQ), row_spec(kw, C_WK), row_spec(qw, C_GQ), row_spec(kw, C_GK),
                  gain_spec, gain_spec, gain_spec, gain_spec, rope_spec, rope_spec],
        out_specs=[pl.BlockSpec((tm, qw), lambda i: (i, 0)), pl.BlockSpec((tm, kw), lambda i: (i, 0)),
                   pl.BlockSpec((tm, qw), lambda i: (i, 0)), pl.BlockSpec((tm, kw), lambda i: (i, 0))],
        out_shape=[q_out, k_out, q_out, k_out],
        compiler_params=_params(1),
        name="gqa_prep",
    )(proj, proj, proj, proj, *gains, *rope_h)


def _mla_prep_kernel(mq_ref, mkv_ref, mkr_ref, wq_ref, wkv_ref, qln_ref, kvln_ref, qn_ref, kn_ref,
                     cos_ref, sin_ref, q_out, k_out, v_out):
    width = MLA_NOPE + MLA_ROPE
    cos, sin = cos_ref[...], sin_ref[...]
    q_lat = (_rms(mq_ref[...].astype(F32), MLA_Q_RANK) * qln_ref[...]).astype(BF16)
    kv_lat = (_rms(mkv_ref[...].astype(F32), MLA_KV_RANK) * kvln_ref[...]).astype(BF16)
    qu = jnp.dot(q_lat, wq_ref[...], preferred_element_type=F32)
    kvu = jnp.dot(kv_lat, wkv_ref[...], preferred_element_type=F32)
    kr = mkr_ref[...].astype(F32)
    kr_sq = jnp.sum(kr * kr, axis=-1, keepdims=True)
    kr_rot = _rope(kr * kn_ref[:, LANES:], cos, sin)
    q_scale = width ** -0.5 * LOG2E
    for hd in range(MLA_HEADS):
        lo = hd * MLA_HEAD_PAD
        qa = qu[:, lo:lo + LANES]
        qb = qu[:, lo + LANES:lo + 2 * LANES]
        ms = jnp.sum(qa * qa + qb * qb, axis=-1, keepdims=True) / width
        rstd = lax.rsqrt(ms + EPS) * q_scale
        q_out[:, lo:lo + LANES] = (qa * rstd * qn_ref[:, :LANES]).astype(q_out.dtype)
        q_out[:, lo + LANES:lo + 2 * LANES] = _rope(qb * rstd * qn_ref[:, LANES:], cos, sin).astype(q_out.dtype)
        kn = kvu[:, lo:lo + MLA_NOPE]
        ms = (jnp.sum(kn * kn, axis=-1, keepdims=True) + kr_sq) / width
        rstd = lax.rsqrt(ms + EPS)
        k_out[:, lo:lo + LANES] = (kn * rstd * kn_ref[:, :LANES]).astype(k_out.dtype)
        k_out[:, lo + LANES:lo + 2 * LANES] = (kr_rot * rstd).astype(k_out.dtype)
        v_out[:, hd * MLA_V:(hd + 1) * MLA_V] = kvu[:, lo + MLA_NOPE:lo + MLA_NOPE + MLA_V].astype(v_out.dtype)


def _mla_prep_call(proj, wq_p, wkv, qln, kvln, qn_p, kn_p, rope_r, layer, tpb):
    t = proj.shape[0]
    tm = _largest_divisor(tpb, (768, 384, 256, 128))
    per_batch = tpb // tm
    hp = MLA_HEADS * MLA_HEAD_PAD
    rope_spec = pl.BlockSpec((tm, LANES), lambda i: (i % per_batch, 0))
    vec = lambda w: pl.BlockSpec((1, w), lambda i: (0, 0))
    return pl.pallas_call(
        _mla_prep_kernel,
        grid=(t // tm,),
        in_specs=[pl.BlockSpec((tm, MLA_Q_RANK), lambda i: (i, C_MQ // MLA_Q_RANK)),
                  pl.BlockSpec((tm, MLA_KV_RANK), lambda i: (i, C_MKV // MLA_KV_RANK)),
                  pl.BlockSpec((tm, LANES), lambda i: (i, C_MKR // LANES)),
                  pl.BlockSpec((None, MLA_Q_RANK, hp), lambda i: (layer, 0, 0)),
                  pl.BlockSpec((None, MLA_KV_RANK, hp), lambda i: (layer, 0, 0)),
                  vec(MLA_Q_RANK), vec(MLA_KV_RANK), vec(MLA_HEAD_PAD), vec(MLA_HEAD_PAD),
                  rope_spec, rope_spec],
        out_specs=[pl.BlockSpec((tm, hp), lambda i: (i, 0)),
                   pl.BlockSpec((tm, hp), lambda i: (i, 0)),
                   pl.BlockSpec((tm, MLA_HEADS * MLA_V), lambda i: (i, 0))],
        out_shape=[jax.ShapeDtypeStruct((t, hp), BF16), jax.ShapeDtypeStruct((t, hp), BF16),
                   jax.ShapeDtypeStruct((t, MLA_HEADS * MLA_V), BF16)],
        compiler_params=_params(1),
        name="mla_prep",
    )(proj, proj, proj, wq_p, wkv, qln, kvln, qn_p, kn_p, *rope_r)


LOG2E = 1.4426950408889634
KEY_CHUNK = 768


def _attention_kernel(*refs, mode, group, tq, seq, tile0):
    if mode == "win":
        sink_ref, q_ref, k_ref, v_ref, o_ref, vt_ref = refs
    else:
        q_ref, k_ref, v_ref, o_ref, vt_ref = refs
    tile = pl.program_id(2) + tile0
    hp, dv, tpb = vt_ref.shape
    dq = k_ref.shape[-1] // hp
    nq = group * tq
    heads = range(hp)

    @pl.when(pl.program_id(2) == 0)
    def _():
        for hh in heads:
            vt_ref[hh] = v_ref[:, hh * dv:(hh + 1) * dv].astype(F32).T.astype(BF16)

    q = [jnp.concatenate([q_ref[:, (hh * group + g) * dq:(hh * group + g + 1) * dq] for g in range(group)],
                         axis=0) for hh in heads]

    def init(hh):
        if mode == "win":
            first = (pl.program_id(1) * hp + hh) * group
            sink = jnp.concatenate(
                [jnp.full((1, tq), sink_ref[first + g] * LOG2E, F32) for g in range(group)], axis=1)
            return sink, jnp.ones((1, nq), F32), jnp.zeros((dv, nq), F32)
        return jnp.full((1, nq), NEG_INF, F32), jnp.zeros((1, nq), F32), jnp.zeros((dv, nq), F32)

    def scores(hh, rows):
        return lax.dot_general(k_ref[rows, hh * dq:(hh + 1) * dq], q[hh], (((1,), (1,)), ((), ())),
                               preferred_element_type=F32)

    def update(hh, carry, s, cols, mask=None):
        m, l, acc = carry
        if mask is not None:
            s = jnp.where(mask(s.shape), s, NEG_INF)
        m_new = jnp.maximum(m, jnp.max(s, axis=0, keepdims=True))
        alpha = jnp.exp2(m - m_new)
        p = jnp.exp2(s - m_new)
        l = alpha * l + jnp.sum(p, axis=0, keepdims=True)
        acc = alpha * acc + jnp.dot(vt_ref[hh, :, cols], p.astype(BF16), preferred_element_type=F32)
        return m_new, l, acc

    def write(hh, carry):
        _, l, acc = carry
        out = acc / l
        for g in range(group):
            col = (hh * group + g) * dv
            o_ref[:, col:col + dv] = out[:, g * tq:(g + 1) * tq].T.astype(o_ref.dtype)

    ctx_keys = slice(0, CTX_LEN)

    @pl.when(tile < CTX_LEN // tq)
    def _():
        s = [scores(hh, ctx_keys) for hh in heads]
        for hh in heads:
            write(hh, update(hh, init(hh), s[hh], ctx_keys))

    @pl.when(tile >= CTX_LEN // tq)
    def _():
        if mode == "win":
            blk = tile - CTX_LEN // tq
            start = jnp.clip((blk - 1) * BLOCK, 0, seq - 3 * BLOCK)
            local = pl.ds(pl.multiple_of(CTX_LEN + start, BLOCK), 3 * BLOCK)

            def band(shape):
                k_pos = start + lax.broadcasted_iota(jnp.int32, shape, 0)
                q_pos = blk * BLOCK + lax.broadcasted_iota(jnp.int32, shape, 1) % tq
                return jnp.abs(q_pos - k_pos) <= WINDOW

            s_ctx = [scores(hh, ctx_keys) for hh in heads]
            s_loc = [scores(hh, local) for hh in heads]
            carry = [update(hh, init(hh), s_ctx[hh], ctx_keys) for hh in heads]
            carry = [update(hh, carry[hh], s_loc[hh], local, band) for hh in heads]
        else:
            chunk = lambda c: slice(c * KEY_CHUNK, (c + 1) * KEY_CHUNK)
            n_chunks = tpb // KEY_CHUNK
            carry = [init(hh) for hh in heads]
            s = [scores(hh, chunk(0)) for hh in heads]
            for c in range(n_chunks):
                s_next = [scores(hh, chunk(c + 1)) for hh in heads] if c + 1 < n_chunks else None
                carry = [update(hh, carry[hh], s[hh], chunk(c)) for hh in heads]
                s = s_next
        for hh in heads:
            write(hh, carry[hh])


def _attention_call(mode, q_src, k_src, v_src, extras, n_batch, tpb, skip_ctx):
    t = q_src[0].shape[0]
    if mode == "mla":
        n_kv, group, dq, tq, hp = MLA_HEADS, 1, MLA_HEAD_PAD, 256, 4
    else:
        n_kv, group, dq, tq, hp = WIN_KV_HEADS, WIN_HEADS // WIN_KV_HEADS, HEAD_DIM, BLOCK, 2
    dv = HEAD_DIM
    tiles = tpb // tq
    tile0 = CTX_LEN // tq if skip_ctx else 0
    qw, kw, vw = hp * group * dq, hp * dq, hp * dv
    assert q_src[1] % qw == 0 and k_src[1] % kw == 0 and v_src[1] % vw == 0 and n_kv % hp == 0
    q_blk, k_blk, v_blk = q_src[1] // qw, k_src[1] // kw, v_src[1] // vw
    q_spec = pl.BlockSpec((tq, qw), lambda b, h, i: (b * tiles + i + tile0, q_blk + h))
    k_spec = pl.BlockSpec((tpb, kw), lambda b, h, i: (b, k_blk + h))
    v_spec = pl.BlockSpec((tpb, vw), lambda b, h, i: (b, v_blk + h))
    in_specs, args = [], []
    if mode == "win":
        in_specs.append(pl.BlockSpec(memory_space=pltpu.SMEM))
        args.append(extras["sink"])
    in_specs += [q_spec, k_spec, v_spec]
    args += [q_src[0], k_src[0], v_src[0]]
    kern = functools.partial(_attention_kernel, mode=mode, group=group, tq=tq, seq=tpb - CTX_LEN,
                             tile0=tile0)
    return pl.pallas_call(
        kern,
        grid=(n_batch, n_kv // hp, tiles - tile0),
        in_specs=in_specs,
        out_specs=pl.BlockSpec((tq, hp * group * dv), lambda b, h, i: (b * tiles + i + tile0, h)),
        out_shape=jax.ShapeDtypeStruct((t, n_kv * group * dv), BF16),
        scratch_shapes=[pltpu.VMEM((hp, dv, tpb), BF16)],
        compiler_params=_params(3),
        name=f"attention_{mode}",
    )(*args)


def _merge_kernel(ow_ref, om_ref, og_ref, gw_ref, gm_ref, gg_ref, wb_ref, wo_ref, x_ref, gate_ref,
                  o_ref, *, tiles_per_batch):
    i, j = pl.program_id(0), pl.program_id(1)
    tm = x_ref.shape[0]

    @pl.when(j == 0)
    def _():
        o_ref[...] = jnp.zeros_like(o_ref)

    y = None
    for br, (o_br, g_br) in enumerate(((ow_ref, gw_ref), (om_ref, gm_ref), (og_ref, gg_ref))):
        term = jax.nn.sigmoid(g_br[...].astype(F32)) * jnp.dot(o_br[...], wb_ref[br],
                                                                preferred_element_type=F32)
        y = term if y is None else y + term
    o_ref[...] += jnp.dot(y.astype(BF16), wo_ref[...], preferred_element_type=F32)

    @pl.when(j == pl.num_programs(1) - 1)
    def _():
        gate = _gate_rows(gate_ref, i // tiles_per_batch, (i % tiles_per_batch) * tm, tm)
        o_ref[...] = x_ref[...] + gate * o_ref[...]


def _merge_call(xs, o_win, o_mla, o_glb, proj, mod, w_branch, w_out, layer, tpb):
    t, d = xs.shape
    tm = _largest_divisor(tpb, (768, 384, 256, 128))
    tn = _largest_divisor(d, (512, 256, 128))
    o_spec = pl.BlockSpec((tm, BRANCH_W), lambda i, j: (i, 0))
    gate_spec = lambda br: pl.BlockSpec((tm, tn), lambda i, j: (i, (C_GATE + br * d) // tn + j))
    kern = functools.partial(_merge_kernel, tiles_per_batch=tpb // tm)
    return pl.pallas_call(
        kern,
        grid=(t // tm, d // tn),
        in_specs=[o_spec, o_spec, o_spec, gate_spec(0), gate_spec(1), gate_spec(2),
                  pl.BlockSpec((None, N_BRANCH, BRANCH_W, tn), lambda i, j: (layer, 0, 0, j)),
                  pl.BlockSpec((None, tn, d), lambda i, j: (layer, j, 0)),
                  pl.BlockSpec((tm, d), lambda i, j: (i, 0)),
                  pl.BlockSpec((None, MOD_ROWS, d), lambda i, j: (layer, 0, 2))],
        out_specs=pl.BlockSpec((tm, d), lambda i, j: (i, 0)),
        out_shape=jax.ShapeDtypeStruct((t, d), F32),
        input_output_aliases={8: 0},
        compiler_params=_params(2),
        name="merge_out_projection",
    )(o_win, o_mla, o_glb, proj, proj, proj, w_branch, w_out, xs, mod)


def _ffn_kernel(x_ref, sh_ref, sc_ref, gate_ref, wg_ref, wu_ref, wd_ref, o_ref, h_ref, *, tiles_per_batch):
    i, f = pl.program_id(0), pl.program_id(1)
    tm = x_ref.shape[0]
    batch, row0 = i // tiles_per_batch, (i % tiles_per_batch) * tm

    @pl.when(f == 0)
    def _():
        _norm_modulate(x_ref, sh_ref, sc_ref, h_ref, batch, row0, tm)
        o_ref[...] = jnp.zeros_like(o_ref)

    h = h_ref[...]
    g = jnp.dot(h, wg_ref[...], preferred_element_type=F32)
    u = jnp.dot(h, wu_ref[...], preferred_element_type=F32)
    act = (g * jax.nn.sigmoid(g) * u).astype(BF16)
    o_ref[...] += jnp.dot(act, wd_ref[...], preferred_element_type=F32)

    @pl.when(f == pl.num_programs(1) - 1)
    def _():
        o_ref[...] = x_ref[...] + _gate_rows(gate_ref, batch, row0, tm) * o_ref[...]


def _ffn_call(xs, mod, w_gate_up, w_down, idx, layer, tpb):
    t, d = xs.shape
    ff = w_down.shape[1]
    tm = _largest_divisor(tpb, (768, 384, 256, 128))
    tf = _largest_divisor(ff, (512, 256, 128))
    n_f = ff // tf
    mod_spec = lambda k: pl.BlockSpec((None, MOD_ROWS, d), lambda i, f: (layer, 0, k))
    kern = functools.partial(_ffn_kernel, tiles_per_batch=tpb // tm)
    return pl.pallas_call(
        kern,
        grid=(t // tm, n_f),
        in_specs=[pl.BlockSpec((tm, d), lambda i, f: (i, 0)), mod_spec(3), mod_spec(4), mod_spec(5),
                  pl.BlockSpec((None, d, tf), lambda i, f: (idx, 0, f)),
                  pl.BlockSpec((None, d, tf), lambda i, f: (idx, 0, f + n_f)),
                  pl.BlockSpec((None, tf, d), lambda i, f: (idx, f, 0))],
        out_specs=pl.BlockSpec((tm, d), lambda i, f: (i, 0)),
        out_shape=jax.ShapeDtypeStruct((t, d), F32),
        scratch_shapes=[pltpu.VMEM((tm, d), BF16)],
        input_output_aliases={0: 0},
        compiler_params=_params(2),
        name="swiglu_ffn",
    )(xs, mod, mod, mod, w_gate_up, w_gate_up, w_down)


MOE_TOKEN_TILE = 256


def _moe_tile_map(step, tiles_per_batch, skip):
    per_batch = tiles_per_batch - skip
    batch, inner = step // per_batch, step % per_batch + skip
    return batch * tiles_per_batch + inner, batch, inner * MOE_TOKEN_TILE


def _router_kernel(x_ref, sh_ref, sc_ref, rhi_ref, rlo_ref, h_ref, o_ref, *, tiles_per_batch, skip):
    tm = x_ref.shape[0]
    _, batch, row0 = _moe_tile_map(pl.program_id(0), tiles_per_batch, skip)
    _norm_modulate(x_ref, sh_ref, sc_ref, h_ref, batch, row0, tm)
    h = h_ref[...]
    h_hi = h.astype(BF16)
    h_lo = (h - h_hi.astype(F32)).astype(BF16)
    logits = (jnp.dot(h_hi, rhi_ref[...], preferred_element_type=F32)
              + jnp.dot(h_hi, rlo_ref[...], preferred_element_type=F32)
              + jnp.dot(h_lo, rhi_ref[...], preferred_element_type=F32))
    lane = lax.broadcasted_iota(jnp.int32, logits.shape, 1).astype(F32)
    logits = jnp.where(lane < N_EXPERTS, logits, -jnp.inf)
    top1 = jnp.max(logits, axis=-1, keepdims=True)
    idx1 = jnp.min(jnp.where(logits == top1, lane, float(LANES)), axis=-1, keepdims=True)
    rest = jnp.where(lane == idx1, -jnp.inf, logits)
    top2 = jnp.max(rest, axis=-1, keepdims=True)
    idx2 = jnp.min(jnp.where(rest == top2, lane, float(LANES)), axis=-1, keepdims=True)
    e2 = jnp.exp(top2 - top1)
    den = 1.0 + e2
    o_ref[...] = jnp.where(lane == 0, 1.0 / den, jnp.where(lane == 1, e2 / den, jnp.where(
        lane == 2, idx1, jnp.where(lane == 3, idx2, 0.0))))


def _router_call(xs, mod, router, layer, tpb, skip):
    t, d = xs.shape
    tm = MOE_TOKEN_TILE
    tiles_per_batch = tpb // tm
    n_tok = t // tiles_per_batch * (tiles_per_batch - skip)
    r_hi = jnp.pad(router, ((0, 0), (0, LANES - N_EXPERTS))).astype(BF16)
    r_lo = jnp.pad(router - r_hi[:, :N_EXPERTS].astype(F32), ((0, 0), (0, LANES - N_EXPERTS))).astype(BF16)
    mod_spec = lambda k: pl.BlockSpec((None, MOD_ROWS, d), lambda i: (layer, 0, k))
    r_spec = pl.BlockSpec((d, LANES), lambda i: (0, 0))
    kern = functools.partial(_router_kernel, tiles_per_batch=tiles_per_batch, skip=skip)
    return pl.pallas_call(
        kern,
        grid=(n_tok // tm,),
        in_specs=[pl.BlockSpec((tm, d), lambda i: (_moe_tile_map(i, tiles_per_batch, skip)[0], 0)),
                  mod_spec(3), mod_spec(4), r_spec, r_spec],
        out_specs=[pl.BlockSpec((tm, d), lambda i: (i, 0)), pl.BlockSpec((tm, LANES), lambda i: (i, 0))],
        out_shape=[jax.ShapeDtypeStruct((n_tok, d), F32), jax.ShapeDtypeStruct((n_tok, LANES), F32)],
        compiler_params=_params(1),
        name="moe_router",
    )(xs, mod, mod, r_hi, r_lo)


def _moe_plan(route, tile):
    n_pairs = 2 * route.shape[0]
    n_tiles = -(-n_pairs // tile) + N_EXPERTS
    expert = route[:, 2:4].astype(jnp.int32).reshape(-1)
    onehot = (expert[:, None] == jnp.arange(N_EXPERTS, dtype=jnp.int32)[None]).astype(jnp.int32)
    csum = jnp.cumsum(onehot, axis=0)
    count = csum[-1]
    padded = (count + tile - 1) // tile * tile
    group_end = jnp.cumsum(padded)
    group_start = group_end - padded
    dest = jnp.sum(onehot * (group_start[None] + csum - 1), axis=1).astype(jnp.int32)
    tile_row = jnp.arange(n_tiles, dtype=jnp.int32) * tile
    tile_expert = jnp.minimum(jnp.sum((tile_row[:, None] >= group_end[None]).astype(jnp.int32), axis=1),
                              N_EXPERTS - 1).astype(jnp.int32)
    n_active = (group_end[-1] // tile).astype(jnp.int32).reshape(1)
    pad_start = jnp.concatenate([group_start + count, group_end[-1:]]).astype(jnp.int32)
    pad_end = jnp.concatenate([group_end, jnp.array([n_tiles * tile], jnp.int32)]).astype(jnp.int32)
    return dest, tile_expert, n_active, pad_start, pad_end, n_tiles


DISPATCH_TILE = 512
DMA_UNROLL = 8


def _dispatch_kernel(pad_lo_ref, pad_hi_ref, dest_ref, h_ref, rows_hbm, zero_ref, sem):
    tm = h_ref.shape[0]

    def row_copy(src_row, dst):
        return pltpu.make_async_copy(src_row, rows_hbm.at[pl.ds(dst, 1), :], sem.at[0])

    @pl.when(pl.program_id(0) == 0)
    def _():
        zero_ref[...] = jnp.zeros_like(zero_ref)
        zero_row = zero_ref.at[pl.ds(0, 1), :]
        for e in range(N_EXPERTS + 1):
            lo, hi = pad_lo_ref[e], pad_hi_ref[e]

            def start_zero(r, carry):
                row_copy(zero_row, r).start()
                return carry

            def wait_zero(r, carry):
                row_copy(zero_row, r).wait()
                return carry

            lax.fori_loop(lo, hi, start_zero, 0)
            lax.fori_loop(lo, hi, wait_zero, 0)

    def start(r, carry):
        for k in range(2):
            row_copy(h_ref.at[pl.ds(r, 1), :], dest_ref[0, 2 * r + k]).start()
        return carry

    def wait(r, carry):
        for k in range(2):
            row_copy(h_ref.at[pl.ds(r, 1), :], dest_ref[0, 2 * r + k]).wait()
        return carry

    lax.fori_loop(0, tm, start, 0, unroll=DMA_UNROLL)
    lax.fori_loop(0, tm, wait, 0, unroll=DMA_UNROLL)


def _dispatch_call(h, dest, pad_lo, pad_hi, n_rows):
    t, d = h.shape
    tm = DISPATCH_TILE
    grid_spec = pltpu.PrefetchScalarGridSpec(
        num_scalar_prefetch=2,
        grid=(t // tm,),
        in_specs=[pl.BlockSpec((None, 1, 2 * tm), lambda i, lo, hi: (i, 0, 0), memory_space=pltpu.SMEM),
                  pl.BlockSpec((tm, d), lambda i, lo, hi: (i, 0))],
        out_specs=pl.BlockSpec(memory_space=pl.ANY),
        scratch_shapes=[pltpu.VMEM((8, d), F32), pltpu.SemaphoreType.DMA((1,))],
    )
    return pl.pallas_call(
        _dispatch_kernel,
        grid_spec=grid_spec,
        out_shape=jax.ShapeDtypeStruct((n_rows, d), F32),
        compiler_params=_params(1),
        name="moe_dispatch",
    )(pad_lo, pad_hi, dest.reshape(t // tm, 1, 2 * tm), h)


def _moe_group_kernel(expert_ref, n_active_ref, x_ref, wg_ref, wu_ref, wd_ref, o_ref, h_ref):
    m, f = pl.program_id(0), pl.program_id(1)

    @pl.when(f == 0)
    def _():
        h_ref[...] = x_ref[...].astype(BF16)
        o_ref[...] = jnp.zeros_like(o_ref)

    @pl.when(m < n_active_ref[0])
    def _():
        h = h_ref[...]
        g = jnp.dot(h, wg_ref[...], preferred_element_type=F32)
        u = jnp.dot(h, wu_ref[...], preferred_element_type=F32)
        act = (g * jax.nn.sigmoid(g) * u).astype(BF16)
        o_ref[...] += jnp.dot(act, wd_ref[...], preferred_element_type=F32)


def _moe_group_call(rows, tile_expert, n_active, w_gate_up, w_down, idx, tile):
    n_rows, d = rows.shape
    ff = w_down.shape[2]
    tf = _largest_divisor(ff, (512, 256, 128))
    n_f = ff // tf

    def f_eff(m, f, na):
        return jnp.where(m < na[0], f, n_f - 1)

    grid_spec = pltpu.PrefetchScalarGridSpec(
        num_scalar_prefetch=2,
        grid=(n_rows // tile, n_f),
        in_specs=[pl.BlockSpec((tile, d), lambda m, f, te, na: (m, 0)),
                  pl.BlockSpec((None, None, d, tf), lambda m, f, te, na: (idx, te[m], 0, f_eff(m, f, na))),
                  pl.BlockSpec((None, None, d, tf),
                               lambda m, f, te, na: (idx, te[m], 0, f_eff(m, f, na) + n_f)),
                  pl.BlockSpec((None, None, tf, d), lambda m, f, te, na: (idx, te[m], f_eff(m, f, na), 0))],
        out_specs=pl.BlockSpec((tile, d), lambda m, f, te, na: (m, 0)),
        scratch_shapes=[pltpu.VMEM((tile, d), BF16)],
    )
    return pl.pallas_call(
        _moe_group_kernel,
        grid_spec=grid_spec,
        out_shape=jax.ShapeDtypeStruct((n_rows, d), F32),
        compiler_params=_params(2),
        name="moe_experts",
    )(tile_expert, n_active, rows, w_gate_up, w_gate_up, w_down)


def _combine_kernel(dest_ref, next_ref, x_ref, route_ref, gate_ref, y_hbm, o_ref, buf, sem, *,
                    tiles_per_batch, skip):
    i, n = pl.program_id(0), pl.num_programs(0)
    tm = x_ref.shape[0]
    slot = i % 2

    def row_copy(dests, to, r, k):
        return pltpu.make_async_copy(y_hbm.at[pl.ds(dests[0, 2 * r + k], 1), :],
                                     buf.at[to, k, pl.ds(r, 1), :], sem.at[to, k])

    def gather(dests, to, wait):
        def body(r, carry):
            for k in range(2):
                copy = row_copy(dests, to, r, k)
                copy.wait() if wait else copy.start()
            return carry
        lax.fori_loop(0, tm, body, 0, unroll=DMA_UNROLL)

    @pl.when(i == 0)
    def _():
        gather(dest_ref, slot, wait=False)

    @pl.when(i + 1 < n)
    def _():
        gather(next_ref, 1 - slot, wait=False)

    gather(dest_ref, slot, wait=True)
    route = route_ref[...]
    mix = route[:, 0:1] * buf[slot, 0] + route[:, 1:2] * buf[slot, 1]
    _, batch, row0 = _moe_tile_map(i, tiles_per_batch, skip)
    o_ref[...] = x_ref[...] + _gate_rows(gate_ref, batch, row0, tm) * mix


def _combine_call(xs, route, mod, y, dest, layer, tpb, skip):
    t, d = xs.shape
    tm = MOE_TOKEN_TILE
    tiles_per_batch = tpb // tm
    n_tok = route.shape[0]
    n_steps = n_tok // tm
    dest = dest.reshape(n_steps, 1, 2 * tm)
    kern = functools.partial(_combine_kernel, tiles_per_batch=tiles_per_batch, skip=skip)
    smem_spec = lambda index: pl.BlockSpec((None, 1, 2 * tm), index, memory_space=pltpu.SMEM)
    return pl.pallas_call(
        kern,
        grid=(n_steps,),
        in_specs=[smem_spec(lambda i: (i, 0, 0)),
                  smem_spec(lambda i: (jnp.minimum(i + 1, n_steps - 1), 0, 0)),
                  pl.BlockSpec((tm, d), lambda i: (_moe_tile_map(i, tiles_per_batch, skip)[0], 0)),
                  pl.BlockSpec((tm, LANES), lambda i: (i, 0)),
                  pl.BlockSpec((None, MOD_ROWS, d), lambda i: (layer, 0, 5)),
                  pl.BlockSpec(memory_space=pl.ANY)],
        out_specs=pl.BlockSpec((tm, d), lambda i: (i, 0)),
        out_shape=jax.ShapeDtypeStruct((n_tok, d), F32),
        scratch_shapes=[pltpu.VMEM((2, 2, tm, d), F32), pltpu.SemaphoreType.DMA((2, 2))],
        input_output_aliases={} if skip else {2: 0},
        compiler_params=_params(1),
        name="moe_combine",
    )(dest, dest, xs, route, mod, y)


def _rope_tables(seq, dim):
    rows = seq // GRID_W
    row = jnp.repeat(jnp.arange(rows), GRID_W).astype(F32)
    col = jnp.tile(jnp.arange(GRID_W), rows).astype(F32)
    quarter = dim // 4
    freqs = ROPE_THETA ** (-jnp.arange(quarter, dtype=F32) / quarter)
    ang_r = row[:, None] * freqs
    ang_c = col[:, None] * freqs
    fill = LANES // 2 - 2 * quarter
    cos_half = [jnp.cos(ang_r), jnp.cos(ang_c), jnp.ones((seq, fill), F32)]
    sin_half = [jnp.sin(ang_r), jnp.sin(ang_c), jnp.zeros((seq, fill), F32)]
    cos = jnp.concatenate(cos_half + cos_half, axis=-1)
    sin = jnp.concatenate([-t for t in sin_half] + sin_half, axis=-1)
    ctx_rows = lambda value: jnp.full((CTX_LEN, LANES), value, F32)
    return jnp.concatenate([ctx_rows(1.0), cos], axis=0), jnp.concatenate([ctx_rows(0.0), sin], axis=0)


def _pair_layout(v):
    quarter = v.shape[-1] // 4
    a, b, c, d = (v[..., k * quarter:(k + 1) * quarter] for k in range(4))
    zeros = jnp.zeros(v.shape[:-1] + (LANES // 2 - 2 * quarter,), v.dtype)
    return jnp.concatenate([a, c, zeros, b, d, zeros], axis=-1)


def _pair_layout_heads(w, n_heads):
    w = w.reshape(w.shape[:-1] + (n_heads, HEAD_DIM))
    return _pair_layout(w).reshape(w.shape[:-2] + (n_heads * HEAD_DIM,))


def _layout_w_in(w_in):
    sizes = (WIN_HEADS * HEAD_DIM, WIN_KV_HEADS * HEAD_DIM, WIN_KV_HEADS * HEAD_DIM,
             MLA_Q_RANK, MLA_KV_RANK, MLA_ROPE,
             GLB_HEADS * HEAD_DIM, GLB_KV_HEADS * HEAD_DIM, GLB_KV_HEADS * HEAD_DIM)
    offs = np.concatenate([[0], np.cumsum(sizes)])
    wq, wk, wv, mq, mkv, mkr, gq, gk, gv = (w_in[..., int(offs[k]):int(offs[k + 1])] for k in range(9))
    gates = w_in[..., int(offs[9]):]
    wq, gq = _pair_layout_heads(wq, WIN_HEADS), _pair_layout_heads(gq, GLB_HEADS)
    wk, gk = _pair_layout_heads(wk, WIN_KV_HEADS), _pair_layout_heads(gk, GLB_KV_HEADS)
    pad = jnp.zeros(w_in.shape[:-1] + (C_GATE - C_MKR - LANES,), w_in.dtype)
    return jnp.concatenate([wq, gq, wk, wv, gk, gv, mq, mkv, _pair_layout(mkr), pad, gates],
                           axis=-1).astype(BF16)


def _mla_head_layout(w):
    w = w.reshape(w.shape[:-1] + (MLA_HEADS, MLA_NOPE + MLA_ROPE))
    w = jnp.concatenate([w[..., :MLA_NOPE], _pair_layout(w[..., MLA_NOPE:])], axis=-1)
    return w.reshape(w.shape[:-2] + (MLA_HEADS * MLA_HEAD_PAD,))


def kernel(x, c, ctx, c_ctx, w_mod, b_mod, w_in, win_q_norm, win_k_norm, win_sink, mla_q_lat_norm,
           mla_kv_lat_norm, mla_w_q_up, mla_w_kv_up, mla_q_norm, mla_k_norm, glb_q_norm, glb_k_norm,
           w_branch, w_out, ffn_w_gate_up, ffn_w_down, moe_router, moe_w_gate_up, moe_w_down):
    n_batch, seq, d = x.shape
    depth = w_mod.shape[0]
    tpb = CTX_LEN + seq
    assert ctx.shape[1] == CTX_LEN and seq % GRID_W == 0 and seq % BLOCK == 0 and seq >= 3 * BLOCK
    assert n_batch <= MOD_ROWS - 8 and C_MKR + LANES <= C_GATE

    xs = jnp.concatenate([ctx, x], axis=1).reshape(n_batch * tpb, d)
    cvec = jnp.zeros((MOD_ROWS, d), F32).at[:n_batch].set(c).at[MOD_ROWS - 8].set(c_ctx)
    mod = _mod_call(cvec, w_mod, b_mod)

    rope_h = _rope_tables(seq, HEAD_DIM)
    rope_r = _rope_tables(seq, MLA_ROPE)
    w_in_p = _layout_w_in(w_in)
    wq_up = _mla_head_layout(mla_w_q_up).astype(BF16)
    wkv_up = mla_w_kv_up.astype(BF16)
    mla_gain = lambda g: jnp.concatenate([g[:MLA_NOPE], _pair_layout(g[MLA_NOPE:])])[None]
    gqa_gain = lambda g: _pair_layout(g)[None]
    w_branch_b, w_out_b = w_branch.astype(BF16), w_out.astype(BF16)
    ffn_gu, ffn_dn = ffn_w_gate_up.astype(BF16), ffn_w_down.astype(BF16)
    moe_gu, moe_dn = moe_w_gate_up.astype(BF16), moe_w_down.astype(BF16)

    latent_only = False
    for layer in range(depth):
        proj = _inproj_call(xs, mod, w_in_p, layer, tpb)
        q_win, k_win, q_glb, k_glb = _gqa_prep_call(
            proj, [gqa_gain(g[layer]) for g in (win_q_norm, win_k_norm, glb_q_norm, glb_k_norm)], rope_h, tpb)
        q_mla, k_mla, v_mla = _mla_prep_call(
            proj, wq_up, wkv_up, mla_q_lat_norm[layer][None], mla_kv_lat_norm[layer][None],
            mla_gain(mla_q_norm[layer]), mla_gain(mla_k_norm[layer]), rope_r, layer, tpb)
        o_win = _attention_call("win", (q_win, 0), (k_win, 0), (proj, C_WV), dict(sink=win_sink[layer]),
                                n_batch, tpb, False)
        o_mla = _attention_call("mla", (q_mla, 0), (k_mla, 0), (v_mla, 0), {}, n_batch, tpb, False)
        o_glb = _attention_call("glb", (q_glb, 0), (k_glb, 0), (proj, C_GV), {}, n_batch, tpb, False)
        xs = _merge_call(xs, o_win, o_mla, o_glb, proj, mod, w_branch_b, w_out_b, layer, tpb)
        if layer % 2 == 0:
            xs = _ffn_call(xs, mod, ffn_gu, ffn_dn, layer // 2, layer, tpb)
        else:
            skip = 1 if layer == depth - 1 else 0
            h, route = _router_call(xs, mod, moe_router[layer // 2], layer, tpb, skip)
            tile = 768 if 2 * h.shape[0] >= 16 * 768 else 128
            dest, tile_expert, n_active, pad_lo, pad_hi, n_tiles = _moe_plan(route, tile)
            rows = _dispatch_call(h, dest, pad_lo, pad_hi, n_tiles * tile)
            y = _moe_group_call(rows, tile_expert, n_active, moe_gu, moe_dn, layer // 2, tile)
            xs = _combine_call(xs, route, mod, y, dest, layer, tpb, skip)
            latent_only = bool(skip)
    if latent_only:
        return xs.reshape(n_batch, seq, d)
    return xs.reshape(n_batch, tpb, d)[:, CTX_LEN:, :]
```

```python
import functools

import jax
import jax.numpy as jnp
import numpy as np
from jax import lax
from jax.experimental import pallas as pl
from jax.experimental.pallas import tpu as pltpu

CTX_LEN = 256
GRID_W = 64
HEAD_DIM = 128
WIN_HEADS = 8
WIN_KV_HEADS = 2
WINDOW = 128
BLOCK = 128
MLA_HEADS = 8
MLA_NOPE = 128
MLA_ROPE = 64
MLA_V = 128
MLA_Q_RANK = 512
MLA_KV_RANK = 256
GLB_HEADS = 8
GLB_KV_HEADS = 2
BRANCH_W = 1024
N_BRANCH = 3
N_EXPERTS = 8
ROPE_THETA = 10000.0
EPS = 1e-6
NEG_INF = -1e30

LANES = 128
MLA_HEAD_PAD = 256
MOD_ROWS = 24
VMEM_LIMIT = 60 * 1024 * 1024

C_WQ = 0
C_GQ = C_WQ + WIN_HEADS * HEAD_DIM
C_WK = C_GQ + GLB_HEADS * HEAD_DIM
C_WV = C_WK + WIN_KV_HEADS * HEAD_DIM
C_GK = C_WV + WIN_KV_HEADS * HEAD_DIM
C_GV = C_GK + GLB_KV_HEADS * HEAD_DIM
C_MQ = C_GV + GLB_KV_HEADS * HEAD_DIM
C_MKV = C_MQ + MLA_Q_RANK
C_MKR = C_MKV + MLA_KV_RANK
C_GATE = 4096

BF16 = jnp.bfloat16
F32 = jnp.float32


def _largest_divisor(n, candidates):
    for cand in candidates:
        if n % cand == 0:
            return cand
    raise ValueError(f"no tile in {candidates} divides {n}")


def _params(n_axes):
    return pltpu.CompilerParams(dimension_semantics=("arbitrary",) * n_axes,
                                vmem_limit_bytes=VMEM_LIMIT)


def _rms(x, width):
    ms = jnp.sum(x * x, axis=-1, keepdims=True) * (1.0 / width)
    return x * lax.rsqrt(ms + EPS)


def _rope(y, cos, sin):
    return y * cos + pltpu.roll(y, LANES // 2, 1) * sin


def _norm_modulate(x_ref, sh_ref, sc_ref, h_ref, batch, row0, n_rows):
    block = 128
    assert n_rows % block == 0 and CTX_LEN % block == 0
    pick = lambda ref, row: ref[pl.ds(row, 1), :]
    sh_lat, sc_lat = pick(sh_ref, batch), 1.0 + pick(sc_ref, batch)
    sh_ctx, sc_ctx = pick(sh_ref, MOD_ROWS - 8), 1.0 + pick(sc_ref, MOD_ROWS - 8)
    for blk in range(n_rows // block):
        rows = slice(blk * block, (blk + 1) * block)
        is_ctx = row0 + blk * block < CTX_LEN
        sh = jnp.where(is_ctx, sh_ctx, sh_lat)
        sc = jnp.where(is_ctx, sc_ctx, sc_lat)
        xf = x_ref[rows, :]
        h_ref[rows, :] = (_rms(xf, xf.shape[-1]) * sc + sh).astype(h_ref.dtype)


def _gate_rows(g_ref, batch, row0, n_rows):
    rows = lax.broadcasted_iota(jnp.int32, (n_rows, 1), 0) + row0
    g_lat = g_ref[pl.ds(batch, 1), :]
    g_ctx = g_ref[pl.ds(MOD_ROWS - 8, 1), :]
    return jnp.where(rows < CTX_LEN, g_ctx, g_lat)


def _mod_kernel(c_ref, w_ref, b_ref, o_ref):
    cv = c_ref[...]
    sc = (cv * jax.nn.sigmoid(cv)).astype(BF16)
    o_ref[...] = jnp.dot(sc, w_ref[...].astype(BF16), preferred_element_type=F32) + b_ref[...]


def _mod_call(cvec, w_mod, b_mod):
    depth, d, n = w_mod.shape
    tn = _largest_divisor(n, (1024, 512, 256, 128))
    return pl.pallas_call(
        _mod_kernel,
        grid=(depth, n // tn),
        in_specs=[pl.BlockSpec((MOD_ROWS, d), lambda l, j: (0, 0)),
                  pl.BlockSpec((None, d, tn), lambda l, j: (l, 0, j)),
                  pl.BlockSpec((None, 1, tn), lambda l, j: (l, 0, j))],
        out_specs=pl.BlockSpec((None, MOD_ROWS, tn), lambda l, j: (l, 0, j)),
        out_shape=jax.ShapeDtypeStruct((depth, MOD_ROWS, n), F32),
        compiler_params=_params(2),
        name="mod_vectors",
    )(cvec, w_mod, b_mod.reshape(depth, 1, n))


def _inproj_kernel(x_ref, sh_ref, sc_ref, w_ref, o_ref, h_ref, *, tiles_per_batch):
    i = pl.program_id(0)
    tm = x_ref.shape[0]

    @pl.when(pl.program_id(1) == 0)
    def _():
        _norm_modulate(x_ref, sh_ref, sc_ref, h_ref, i // tiles_per_batch,
                       (i % tiles_per_batch) * tm, tm)

    o_ref[...] = jnp.dot(h_ref[...], w_ref[...], preferred_element_type=F32).astype(o_ref.dtype)


def _inproj_call(xs, mod, w_in_p, layer, tpb):
    t, d = xs.shape
    n = w_in_p.shape[-1]
    tm = _largest_divisor(tpb, (768, 384, 256, 128))
    tn = _largest_divisor(n, (1024, 512, 256, 128))
    kern = functools.partial(_inproj_kernel, tiles_per_batch=tpb // tm)
    return pl.pallas_call(
        kern,
        grid=(t // tm, n // tn),
        in_specs=[pl.BlockSpec((tm, d), lambda i, j: (i, 0)),
                  pl.BlockSpec((None, MOD_ROWS, d), lambda i, j: (layer, 0, 0)),
                  pl.BlockSpec((None, MOD_ROWS, d), lambda i, j: (layer, 0, 1)),
                  pl.BlockSpec((None, d, tn), lambda i, j: (layer, 0, j))],
        out_specs=pl.BlockSpec((tm, tn), lambda i, j: (i, j)),
        out_shape=jax.ShapeDtypeStruct((t, n), BF16),
        scratch_shapes=[pltpu.VMEM((tm, d), BF16)],
        compiler_params=_params(2),
        name="in_projection",
    )(xs, mod, mod, w_in_p)


def _gqa_prep_kernel(wq_ref, wk_ref, gq_ref, gk_ref, wqg_ref, wkg_ref, gqg_ref, gkg_ref,
                     cos_ref, sin_ref, qa_ref, ka_ref, qc_ref, kc_ref):
    cos, sin = cos_ref[...], sin_ref[...]
    q_scale = HEAD_DIM ** -0.5 * LOG2E
    for src, gain, dst, scale in ((wq_ref, wqg_ref, qa_ref, q_scale), (wk_ref, wkg_ref, ka_ref, None),
                                  (gq_ref, gqg_ref, qc_ref, q_scale), (gk_ref, gkg_ref, kc_ref, None)):
        g = gain[...] if scale is None else gain[...] * scale
        for hd in range(src.shape[1] // HEAD_DIM):
            cols = slice(hd * HEAD_DIM, (hd + 1) * HEAD_DIM)
            y = _rms(src[:, cols].astype(F32), HEAD_DIM) * g
            dst[:, cols] = _rope(y, cos, sin).astype(dst.dtype)


def _gqa_prep_call(proj, gains, rope_h, tpb):
    t = proj.shape[0]
    tm = _largest_divisor(tpb, (384, 256, 128))
    per_batch = tpb // tm
    qw, kw = WIN_HEADS * HEAD_DIM, WIN_KV_HEADS * HEAD_DIM
    row_spec = lambda width, col: pl.BlockSpec((tm, width), lambda i: (i, col // width))
    gain_spec = pl.BlockSpec((1, HEAD_DIM), lambda i: (0, 0))
    rope_spec = pl.BlockSpec((tm, LANES), lambda i: (i % per_batch, 0))
    q_out, k_out = jax.ShapeDtypeStruct((t, qw), BF16), jax.ShapeDtypeStruct((t, kw), BF16)
    return pl.pallas_call(
        _gqa_prep_kernel,
        grid=(t // tm,),
        in_specs=[row_spec(qw, C_WQ), row_spec(kw, C_WK), row_spec(qw, C_GQ), row_spec(kw, C_GK),
                  gain_spec, gain_spec, gain_spec, gain_spec, rope_spec, rope_spec],
        out_specs=[pl.BlockSpec((tm, qw), lambda i: (i, 0)), pl.BlockSpec((tm, kw), lambda i: (i, 0)),
                   pl.BlockSpec((tm, qw), lambda i: (i, 0)), pl.BlockSpec((tm, kw), lambda i: (i, 0))],
        out_shape=[q_out, k_out, q_out, k_out],
        compiler_params=_params(1),
        name="gqa_prep",
    )(proj, proj, proj, proj, *gains, *rope_h)


def _mla_prep_kernel(mq_ref, mkv_ref, mkr_ref, wq_ref, wkv_ref, qln_ref, kvln_ref, qn_ref, kn_ref,
                     cos_ref, sin_ref, q_out, k_out, v_out):
    width = MLA_NOPE + MLA_ROPE
    cos, sin = cos_ref[...], sin_ref[...]
    q_lat = (_rms(mq_ref[...].astype(F32), MLA_Q_RANK) * qln_ref[...]).astype(BF16)
    kv_lat = (_rms(mkv_ref[...].astype(F32), MLA_KV_RANK) * kvln_ref[...]).astype(BF16)
    qu = jnp.dot(q_lat, wq_ref[...], preferred_element_type=F32)
    kvu = jnp.dot(kv_lat, wkv_ref[...], preferred_element_type=F32)
    kr = mkr_ref[...].astype(F32)
    kr_sq = jnp.sum(kr * kr, axis=-1, keepdims=True)
    kr_rot = _rope(kr * kn_ref[:, LANES:], cos, sin)
    q_scale = width ** -0.5 * LOG2E
    for hd in range(MLA_HEADS):
        lo = hd * MLA_HEAD_PAD
        qa = qu[:, lo:lo + LANES]
        qb = qu[:, lo + LANES:lo + 2 * LANES]
        ms = jnp.sum(qa * qa + qb * qb, axis=-1, keepdims=True) / width
        rstd = lax.rsqrt(ms + EPS) * q_scale
        q_out[:, lo:lo + LANES] = (qa * rstd * qn_ref[:, :LANES]).astype(q_out.dtype)
        q_out[:, lo + LANES:lo + 2 * LANES] = _rope(qb * rstd * qn_ref[:, LANES:], cos, sin).astype(q_out.dtype)
        kn = kvu[:, lo:lo + MLA_NOPE]
        ms = (jnp.sum(kn * kn, axis=-1, keepdims=True) + kr_sq) / width
        rstd = lax.rsqrt(ms + EPS)
        k_out[:, lo:lo + LANES] = (kn * rstd * kn_ref[:, :LANES]).astype(k_out.dtype)
        k_out[:, lo + LANES:lo + 2 * LANES] = (kr_rot * rstd).astype(k_out.dtype)
        v_out[:, hd * MLA_V:(hd + 1) * MLA_V] = kvu[:, lo + MLA_NOPE:lo + MLA_NOPE + MLA_V].astype(v_out.dtype)


def _mla_prep_call(proj, wq_p, wkv, qln, kvln, qn_p, kn_p, rope_r, layer, tpb):
    t = proj.shape[0]
    tm = _largest_divisor(tpb, (768, 384, 256, 128))
    per_batch = tpb // tm
    hp = MLA_HEADS * MLA_HEAD_PAD
    rope_spec = pl.BlockSpec((tm, LANES), lambda i: (i % per_batch, 0))
    vec = lambda w: pl.BlockSpec((1, w), lambda i: (0, 0))
    return pl.pallas_call(
        _mla_prep_kernel,
        grid=(t // tm,),
        in_specs=[pl.BlockSpec((tm, MLA_Q_RANK), lambda i: (i, C_MQ // MLA_Q_RANK)),
                  pl.BlockSpec((tm, MLA_KV_RANK), lambda i: (i, C_MKV // MLA_KV_RANK)),
                  pl.BlockSpec((tm, LANES), lambda i: (i, C_MKR // LANES)),
                  pl.BlockSpec((None, MLA_Q_RANK, hp), lambda i: (layer, 0, 0)),
                  pl.BlockSpec((None, MLA_KV_RANK, hp), lambda i: (layer, 0, 0)),
                  vec(MLA_Q_RANK), vec(MLA_KV_RANK), vec(MLA_HEAD_PAD), vec(MLA_HEAD_PAD),
                  rope_spec, rope_spec],
        out_specs=[pl.BlockSpec((tm, hp), lambda i: (i, 0)),
                   pl.BlockSpec((tm, hp), lambda i: (i, 0)),
                   pl.BlockSpec((tm, MLA_HEADS * MLA_V), lambda i: (i, 0))],
        out_shape=[jax.ShapeDtypeStruct((t, hp), BF16), jax.ShapeDtypeStruct((t, hp), BF16),
                   jax.ShapeDtypeStruct((t, MLA_HEADS * MLA_V), BF16)],
        compiler_params=_params(1),
        name="mla_prep",
    )(proj, proj, proj, wq_p, wkv, qln, kvln, qn_p, kn_p, *rope_r)


LOG2E = 1.4426950408889634
KEY_CHUNK = 768


def _attention_kernel(*refs, mode, group, tq, seq, tile0):
    if mode == "win":
        sink_ref, q_ref, k_ref, v_ref, o_ref, vt_ref = refs
    else:
        q_ref, k_ref, v_ref, o_ref, vt_ref = refs
    tile = pl.program_id(2) + tile0
    hp, dv, tpb = vt_ref.shape
    dq = k_ref.shape[-1] // hp
    nq = group * tq
    heads = range(hp)

    @pl.when(pl.program_id(2) == 0)
    def _():
        for hh in heads:
            vt_ref[hh] = v_ref[:, hh * dv:(hh + 1) * dv].astype(F32).T.astype(BF16)

    q = [jnp.concatenate([q_ref[:, (hh * group + g) * dq:(hh * group + g + 1) * dq] for g in range(group)],
                         axis=0) for hh in heads]

    def init(hh):
        if mode == "win":
            first = (pl.program_id(1) * hp + hh) * group
            sink = jnp.concatenate(
                [jnp.full((1, tq), sink_ref[first + g] * LOG2E, F32) for g in range(group)], axis=1)
            return sink, jnp.ones((1, nq), F32), jnp.zeros((dv, nq), F32)
        return jnp.full((1, nq), NEG_INF, F32), jnp.zeros((1, nq), F32), jnp.zeros((dv, nq), F32)

    def scores(hh, rows):
        return lax.dot_general(k_ref[rows, hh * dq:(hh + 1) * dq], q[hh], (((1,), (1,)), ((), ())),
                               preferred_element_type=F32)

    def update(hh, carry, s, cols, mask=None):
        m, l, acc = carry
        if mask is not None:
            s = jnp.where(mask(s.shape), s, NEG_INF)
        m_new = jnp.maximum(m, jnp.max(s, axis=0, keepdims=True))
        alpha = jnp.exp2(m - m_new)
        p = jnp.exp2(s - m_new)
        l = alpha * l + jnp.sum(p, axis=0, keepdims=True)
        acc = alpha * acc + jnp.dot(vt_ref[hh, :, cols], p.astype(BF16), preferred_element_type=F32)
        return m_new, l, acc

    def write(hh, carry):
        _, l, acc = carry
        out = acc / l
        for g in range(group):
            col = (hh * group + g) * dv
            o_ref[:, col:col + dv] = out[:, g * tq:(g + 1) * tq].T.astype(o_ref.dtype)

    ctx_keys = slice(0, CTX_LEN)

    @pl.when(tile < CTX_LEN // tq)
    def _():
        s = [scores(hh, ctx_keys) for hh in heads]
        for hh in heads:
            write(hh, update(hh, init(hh), s[hh], ctx_keys))

    @pl.when(tile >= CTX_LEN // tq)
    def _():
        if mode == "win":
            blk = tile - CTX_LEN // tq
            start = jnp.clip((blk - 1) * BLOCK, 0, seq - 3 * BLOCK)
            local = pl.ds(pl.multiple_of(CTX_LEN + start, BLOCK), 3 * BLOCK)

            def band(shape):
                k_pos = start + lax.broadcasted_iota(jnp.int32, shape, 0)
                q_pos = blk * BLOCK + lax.broadcasted_iota(jnp.int32, shape, 1) % tq
                return jnp.abs(q_pos - k_pos) <= WINDOW

            s_ctx = [scores(hh, ctx_keys) for hh in heads]
            s_loc = [scores(hh, local) for hh in heads]
            carry = [update(hh, init(hh), s_ctx[hh], ctx_keys) for hh in heads]
            carry = [update(hh, carry[hh], s_loc[hh], local, band) for hh in heads]
        else:
            chunk = lambda c: slice(c * KEY_CHUNK, (c + 1) * KEY_CHUNK)
            n_chunks = tpb // KEY_CHUNK
            carry = [init(hh) for hh in heads]
            s = [scores(hh, chunk(0)) for hh in heads]
            for c in range(n_chunks):
                s_next = [scores(hh, chunk(c + 1)) for hh in heads] if c + 1 < n_chunks else None
                carry = [update(hh, carry[hh], s[hh], chunk(c)) for hh in heads]
                s = s_next
        for hh in heads:
            write(hh, carry[hh])


def _attention_call(mode, q_src, k_src, v_src, extras, n_batch, tpb, skip_ctx):
    t = q_src[0].shape[0]
    if mode == "mla":
        n_kv, group, dq, tq, hp = MLA_HEADS, 1, MLA_HEAD_PAD, 256, 4
    else:
        n_kv, group, dq, tq, hp = WIN_KV_HEADS, WIN_HEADS // WIN_KV_HEADS, HEAD_DIM, BLOCK, 2
    dv = HEAD_DIM
    tiles = tpb // tq
    tile0 = CTX_LEN // tq if skip_ctx else 0
    qw, kw, vw = hp * group * dq, hp * dq, hp * dv
    assert q_src[1] % qw == 0 and k_src[1] % kw == 0 and v_src[1] % vw == 0 and n_kv % hp == 0
    q_blk, k_blk, v_blk = q_src[1] // qw, k_src[1] // kw, v_src[1] // vw
    q_spec = pl.BlockSpec((tq, qw), lambda b, h, i: (b * tiles + i + tile0, q_blk + h))
    k_spec = pl.BlockSpec((tpb, kw), lambda b, h, i: (b, k_blk + h))
    v_spec = pl.BlockSpec((tpb, vw), lambda b, h, i: (b, v_blk + h))
    in_specs, args = [], []
    if mode == "win":
        in_specs.append(pl.BlockSpec(memory_space=pltpu.SMEM))
        args.append(extras["sink"])
    in_specs += [q_spec, k_spec, v_spec]
    args += [q_src[0], k_src[0], v_src[0]]
    kern = functools.partial(_attention_kernel, mode=mode, group=group, tq=tq, seq=tpb - CTX_LEN,
                             tile0=tile0)
    return pl.pallas_call(
        kern,
        grid=(n_batch, n_kv // hp, tiles - tile0),
        in_specs=in_specs,
        out_specs=pl.BlockSpec((tq, hp * group * dv), lambda b, h, i: (b * tiles + i + tile0, h)),
        out_shape=jax.ShapeDtypeStruct((t, n_kv * group * dv), BF16),
        scratch_shapes=[pltpu.VMEM((hp, dv, tpb), BF16)],
        compiler_params=_params(3),
        name=f"attention_{mode}",
    )(*args)


def _merge_kernel(ow_ref, om_ref, og_ref, gw_ref, gm_ref, gg_ref, wb_ref, wo_ref, x_ref, gate_ref,
                  o_ref, *, tiles_per_batch):
    i, j = pl.program_id(0), pl.program_id(1)
    tm = x_ref.shape[0]

    @pl.when(j == 0)
    def _():
        o_ref[...] = jnp.zeros_like(o_ref)

    y = None
    for br, (o_br, g_br) in enumerate(((ow_ref, gw_ref), (om_ref, gm_ref), (og_ref, gg_ref))):
        term = jax.nn.sigmoid(g_br[...].astype(F32)) * jnp.dot(o_br[...], wb_ref[br],
                                                                preferred_element_type=F32)
        y = term if y is None else y + term
    o_ref[...] += jnp.dot(y.astype(BF16), wo_ref[...], preferred_element_type=F32)

    @pl.when(j == pl.num_programs(1) - 1)
    def _():
        gate = _gate_rows(gate_ref, i // tiles_per_batch, (i % tiles_per_batch) * tm, tm)
        o_ref[...] = x_ref[...] + gate * o_ref[...]


def _merge_call(xs, o_win, o_mla, o_glb, proj, mod, w_branch, w_out, layer, tpb):
    t, d = xs.shape
    tm = _largest_divisor(tpb, (768, 384, 256, 128))
    tn = _largest_divisor(d, (512, 256, 128))
    o_spec = pl.BlockSpec((tm, BRANCH_W), lambda i, j: (i, 0))
    gate_spec = lambda br: pl.BlockSpec((tm, tn), lambda i, j: (i, (C_GATE + br * d) // tn + j))
    kern = functools.partial(_merge_kernel, tiles_per_batch=tpb // tm)
    return pl.pallas_call(
        kern,
        grid=(t // tm, d // tn),
        in_specs=[o_spec, o_spec, o_spec, gate_spec(0), gate_spec(1), gate_spec(2),
                  pl.BlockSpec((None, N_BRANCH, BRANCH_W, tn), lambda i, j: (layer, 0, 0, j)),
                  pl.BlockSpec((None, tn, d), lambda i, j: (layer, j, 0)),
                  pl.BlockSpec((tm, d), lambda i, j: (i, 0)),
                  pl.BlockSpec((None, MOD_ROWS, d), lambda i, j: (layer, 0, 2))],
        out_specs=pl.BlockSpec((tm, d), lambda i, j: (i, 0)),
        out_shape=jax.ShapeDtypeStruct((t, d), F32),
        input_output_aliases={8: 0},
        compiler_params=_params(2),
        name="merge_out_projection",
    )(o_win, o_mla, o_glb, proj, proj, proj, w_branch, w_out, xs, mod)


def _ffn_kernel(x_ref, sh_ref, sc_ref, gate_ref, wg_ref, wu_ref, wd_ref, o_ref, h_ref, *, tiles_per_batch):
    i, f = pl.program_id(0), pl.program_id(1)
    tm = x_ref.shape[0]
    batch, row0 = i // tiles_per_batch, (i % tiles_per_batch) * tm

    @pl.when(f == 0)
    def _():
        _norm_modulate(x_ref, sh_ref, sc_ref, h_ref, batch, row0, tm)
        o_ref[...] = jnp.zeros_like(o_ref)

    h = h_ref[...]
    g = jnp.dot(h, wg_ref[...], preferred_element_type=F32)
    u = jnp.dot(h, wu_ref[...], preferred_element_type=F32)
    act = (g * jax.nn.sigmoid(g) * u).astype(BF16)
    o_ref[...] += jnp.dot(act, wd_ref[...], preferred_element_type=F32)

    @pl.when(f == pl.num_programs(1) - 1)
    def _():
        o_ref[...] = x_ref[...] + _gate_rows(gate_ref, batch, row0, tm) * o_ref[...]


def _ffn_call(xs, mod, w_gate_up, w_down, idx, layer, tpb):
    t, d = xs.shape
    ff = w_down.shape[1]
    tm = _largest_divisor(tpb, (768, 384, 256, 128))
    tf = _largest_divisor(ff, (512, 256, 128))
    n_f = ff // tf
    mod_spec = lambda k: pl.BlockSpec((None, MOD_ROWS, d), lambda i, f: (layer, 0, k))
    kern = functools.partial(_ffn_kernel, tiles_per_batch=tpb // tm)
    return pl.pallas_call(
        kern,
        grid=(t // tm, n_f),
        in_specs=[pl.BlockSpec((tm, d), lambda i, f: (i, 0)), mod_spec(3), mod_spec(4), mod_spec(5),
                  pl.BlockSpec((None, d, tf), lambda i, f: (idx, 0, f)),
                  pl.BlockSpec((None, d, tf), lambda i, f: (idx, 0, f + n_f)),
                  pl.BlockSpec((None, tf, d), lambda i, f: (idx, f, 0))],
        out_specs=pl.BlockSpec((tm, d), lambda i, f: (i, 0)),
        out_shape=jax.ShapeDtypeStruct((t, d), F32),
        scratch_shapes=[pltpu.VMEM((tm, d), BF16)],
        input_output_aliases={0: 0},
        compiler_params=_params(2),
        name="swiglu_ffn",
    )(xs, mod, mod, mod, w_gate_up, w_gate_up, w_down)


MOE_TOKEN_TILE = 256


def _moe_tile_map(step, tiles_per_batch, skip):
    per_batch = tiles_per_batch - skip
    batch, inner = step // per_batch, step % per_batch + skip
    return batch * tiles_per_batch + inner, batch, inner * MOE_TOKEN_TILE


def _router_kernel(x_ref, sh_ref, sc_ref, rhi_ref, rlo_ref, h_ref, o_ref, *, tiles_per_batch, skip):
    tm = x_ref.shape[0]
    _, batch, row0 = _moe_tile_map(pl.program_id(0), tiles_per_batch, skip)
    _norm_modulate(x_ref, sh_ref, sc_ref, h_ref, batch, row0, tm)
    h = h_ref[...]
    h_hi = h.astype(BF16)
    h_lo = (h - h_hi.astype(F32)).astype(BF16)
    logits = (jnp.dot(h_hi, rhi_ref[...], preferred_element_type=F32)
              + jnp.dot(h_hi, rlo_ref[...], preferred_element_type=F32)
              + jnp.dot(h_lo, rhi_ref[...], preferred_element_type=F32))
    lane = lax.broadcasted_iota(jnp.int32, logits.shape, 1).astype(F32)
    logits = jnp.where(lane < N_EXPERTS, logits, -jnp.inf)
    top1 = jnp.max(logits, axis=-1, keepdims=True)
    idx1 = jnp.min(jnp.where(logits == top1, lane, float(LANES)), axis=-1, keepdims=True)
    rest = jnp.where(lane == idx1, -jnp.inf, logits)
    top2 = jnp.max(rest, axis=-1, keepdims=True)
    idx2 = jnp.min(jnp.where(rest == top2, lane, float(LANES)), axis=-1, keepdims=True)
    e2 = jnp.exp(top2 - top1)
    den = 1.0 + e2
    o_ref[...] = jnp.where(lane == 0, 1.0 / den, jnp.where(lane == 1, e2 / den, jnp.where(
        lane == 2, idx1, jnp.where(lane == 3, idx2, 0.0))))


def _router_call(xs, mod, router, layer, tpb, skip):
    t, d = xs.shape
    tm = MOE_TOKEN_TILE
    tiles_per_batch = tpb // tm
    n_tok = t // tiles_per_batch * (tiles_per_batch - skip)
    r_hi = jnp.pad(router, ((0, 0), (0, LANES - N_EXPERTS))).astype(BF16)
    r_lo = jnp.pad(router - r_hi[:, :N_EXPERTS].astype(F32), ((0, 0), (0, LANES - N_EXPERTS))).astype(BF16)
    mod_spec = lambda k: pl.BlockSpec((None, MOD_ROWS, d), lambda i: (layer, 0, k))
    r_spec = pl.BlockSpec((d, LANES), lambda i: (0, 0))
    kern = functools.partial(_router_kernel, tiles_per_batch=tiles_per_batch, skip=skip)
    return pl.pallas_call(
        kern,
        grid=(n_tok // tm,),
        in_specs=[pl.BlockSpec((tm, d), lambda i: (_moe_tile_map(i, tiles_per_batch, skip)[0], 0)),
                  mod_spec(3), mod_spec(4), r_spec, r_spec],
        out_specs=[pl.BlockSpec((tm, d), lambda i: (i, 0)), pl.BlockSpec((tm, LANES), lambda i: (i, 0))],
        out_shape=[jax.ShapeDtypeStruct((n_tok, d), F32), jax.ShapeDtypeStruct((n_tok, LANES), F32)],
        compiler_params=_params(1),
        name="moe_router",
    )(xs, mod, mod, r_hi, r_lo)


def _moe_plan(route, tile):
    n_pairs = 2 * route.shape[0]
    n_tiles = -(-n_pairs // tile) + N_EXPERTS
    expert = route[:, 2:4].astype(jnp.int32).reshape(-1)
    onehot = (expert[:, None] == jnp.arange(N_EXPERTS, dtype=jnp.int32)[None]).astype(jnp.int32)
    csum = jnp.cumsum(onehot, axis=0)
    count = csum[-1]
    padded = (count + tile - 1) // tile * tile
    group_end = jnp.cumsum(padded)
    group_start = group_end - padded
    dest = jnp.sum(onehot * (group_start[None] + csum - 1), axis=1).astype(jnp.int32)
    tile_row = jnp.arange(n_tiles, dtype=jnp.int32) * tile
    tile_expert = jnp.minimum(jnp.sum((tile_row[:, None] >= group_end[None]).astype(jnp.int32), axis=1),
                              N_EXPERTS - 1).astype(jnp.int32)
    n_active = (group_end[-1] // tile).astype(jnp.int32).reshape(1)
    pad_start = jnp.concatenate([group_start + count, group_end[-1:]]).astype(jnp.int32)
    pad_end = jnp.concatenate([group_end, jnp.array([n_tiles * tile], jnp.int32)]).astype(jnp.int32)
    return dest, tile_expert, n_active, pad_start, pad_end, n_tiles


DISPATCH_TILE = 512
DMA_UNROLL = 8


def _dispatch_kernel(pad_lo_ref, pad_hi_ref, dest_ref, h_ref, rows_hbm, zero_ref, sem):
    tm = h_ref.shape[0]

    def row_copy(src_row, dst):
        return pltpu.make_async_copy(src_row, rows_hbm.at[pl.ds(dst, 1), :], sem.at[0])

    @pl.when(pl.program_id(0) == 0)
    def _():
        zero_ref[...] = jnp.zeros_like(zero_ref)
        zero_row = zero_ref.at[pl.ds(0, 1), :]
        for e in range(N_EXPERTS + 1):
            lo, hi = pad_lo_ref[e], pad_hi_ref[e]

            def start_zero(r, carry):
                row_copy(zero_row, r).start()
                return carry

            def wait_zero(r, carry):
                row_copy(zero_row, r).wait()
                return carry

            lax.fori_loop(lo, hi, start_zero, 0)
            lax.fori_loop(lo, hi, wait_zero, 0)

    def start(r, carry):
        for k in range(2):
            row_copy(h_ref.at[pl.ds(r, 1), :], dest_ref[0, 2 * r + k]).start()
        return carry

    def wait(r, carry):
        for k in range(2):
            row_copy(h_ref.at[pl.ds(r, 1), :], dest_ref[0, 2 * r + k]).wait()
        return carry

    lax.fori_loop(0, tm, start, 0, unroll=DMA_UNROLL)
    lax.fori_loop(0, tm, wait, 0, unroll=DMA_UNROLL)


def _dispatch_call(h, dest, pad_lo, pad_hi, n_rows):
    t, d = h.shape
    tm = DISPATCH_TILE
    grid_spec = pltpu.PrefetchScalarGridSpec(
        num_scalar_prefetch=2,
        grid=(t // tm,),
        in_specs=[pl.BlockSpec((None, 1, 2 * tm), lambda i, lo, hi: (i, 0, 0), memory_space=pltpu.SMEM),
                  pl.BlockSpec((tm, d), lambda i, lo, hi: (i, 0))],
        out_specs=pl.BlockSpec(memory_space=pl.ANY),
        scratch_shapes=[pltpu.VMEM((8, d), F32), pltpu.SemaphoreType.DMA((1,))],
    )
    return pl.pallas_call(
        _dispatch_kernel,
        grid_spec=grid_spec,
        out_shape=jax.ShapeDtypeStruct((n_rows, d), F32),
        compiler_params=_params(1),
        name="moe_dispatch",
    )(pad_lo, pad_hi, dest.reshape(t // tm, 1, 2 * tm), h)


def _moe_group_kernel(expert_ref, n_active_ref, x_ref, wg_ref, wu_ref, wd_ref, o_ref, h_ref):
    m, f = pl.program_id(0), pl.program_id(1)

    @pl.when(f == 0)
    def _():
        h_ref[...] = x_ref[...].astype(BF16)
        o_ref[...] = jnp.zeros_like(o_ref)

    @pl.when(m < n_active_ref[0])
    def _():
        h = h_ref[...]
        g = jnp.dot(h, wg_ref[...], preferred_element_type=F32)
        u = jnp.dot(h, wu_ref[...], preferred_element_type=F32)
        act = (g * jax.nn.sigmoid(g) * u).astype(BF16)
        o_ref[...] += jnp.dot(act, wd_ref[...], preferred_element_type=F32)


def _moe_group_call(rows, tile_expert, n_active, w_gate_up, w_down, idx, tile):
    n_rows, d = rows.shape
    ff = w_down.shape[2]
    tf = _largest_divisor(ff, (512, 256, 128))
    n_f = ff // tf

    def f_eff(m, f, na):
        return jnp.where(m < na[0], f, n_f - 1)

    grid_spec = pltpu.PrefetchScalarGridSpec(
        num_scalar_prefetch=2,
        grid=(n_rows // tile, n_f),
        in_specs=[pl.BlockSpec((tile, d), lambda m, f, te, na: (m, 0)),
                  pl.BlockSpec((None, None, d, tf), lambda m, f, te, na: (idx, te[m], 0, f_eff(m, f, na))),
                  pl.BlockSpec((None, None, d, tf),
                               lambda m, f, te, na: (idx, te[m], 0, f_eff(m, f, na) + n_f)),
                  pl.BlockSpec((None, None, tf, d), lambda m, f, te, na: (idx, te[m], f_eff(m, f, na), 0))],
        out_specs=pl.BlockSpec((tile, d), lambda m, f, te, na: (m, 0)),
        scratch_shapes=[pltpu.VMEM((tile, d), BF16)],
    )
    return pl.pallas_call(
        _moe_group_kernel,
        grid_spec=grid_spec,
        out_shape=jax.ShapeDtypeStruct((n_rows, d), F32),
        compiler_params=_params(2),
        name="moe_experts",
    )(tile_expert, n_active, rows, w_gate_up, w_gate_up, w_down)


def _combine_kernel(dest_ref, next_ref, x_ref, route_ref, gate_ref, y_hbm, o_ref, buf, sem, *,
                    tiles_per_batch, skip):
    i, n = pl.program_id(0), pl.num_programs(0)
    tm = x_ref.shape[0]
    slot = i % 2

    def row_copy(dests, to, r, k):
        return pltpu.make_async_copy(y_hbm.at[pl.ds(dests[0, 2 * r + k], 1), :],
                                     buf.at[to, k, pl.ds(r, 1), :], sem.at[to, k])

    def gather(dests, to, wait):
        def body(r, carry):
            for k in range(2):
                copy = row_copy(dests, to, r, k)
                copy.wait() if wait else copy.start()
            return carry
        lax.fori_loop(0, tm, body, 0, unroll=DMA_UNROLL)

    @pl.when(i == 0)
    def _():
        gather(dest_ref, slot, wait=False)

    @pl.when(i + 1 < n)
    def _():
        gather(next_ref, 1 - slot, wait=False)

    gather(dest_ref, slot, wait=True)
    route = route_ref[...]
    mix = route[:, 0:1] * buf[slot, 0] + route[:, 1:2] * buf[slot, 1]
    _, batch, row0 = _moe_tile_map(i, tiles_per_batch, skip)
    o_ref[...] = x_ref[...] + _gate_rows(gate_ref, batch, row0, tm) * mix


def _combine_call(xs, route, mod, y, dest, layer, tpb, skip):
    t, d = xs.shape
    tm = MOE_TOKEN_TILE
    tiles_per_batch = tpb // tm
    n_tok = route.shape[0]
    n_steps = n_tok // tm
    dest = dest.reshape(n_steps, 1, 2 * tm)
    kern = functools.partial(_combine_kernel, tiles_per_batch=tiles_per_batch, skip=skip)
    smem_spec = lambda index: pl.BlockSpec((None, 1, 2 * tm), index, memory_space=pltpu.SMEM)
    return pl.pallas_call(
        kern,
        grid=(n_steps,),
        in_specs=[smem_spec(lambda i: (i, 0, 0)),
                  smem_spec(lambda i: (jnp.minimum(i + 1, n_steps - 1), 0, 0)),
                  pl.BlockSpec((tm, d), lambda i: (_moe_tile_map(i, tiles_per_batch, skip)[0], 0)),
                  pl.BlockSpec((tm, LANES), lambda i: (i, 0)),
                  pl.BlockSpec((None, MOD_ROWS, d), lambda i: (layer, 0, 5)),
                  pl.BlockSpec(memory_space=pl.ANY)],
        out_specs=pl.BlockSpec((tm, d), lambda i: (i, 0)),
        out_shape=jax.ShapeDtypeStruct((n_tok, d), F32),
        scratch_shapes=[pltpu.VMEM((2, 2, tm, d), F32), pltpu.SemaphoreType.DMA((2, 2))],
        input_output_aliases={} if skip else {2: 0},
        compiler_params=_params(1),
        name="moe_combine",
    )(dest, dest, xs, route, mod, y)


def _rope_tables(seq, dim):
    rows = seq // GRID_W
    row = jnp.repeat(jnp.arange(rows), GRID_W).astype(F32)
    col = jnp.tile(jnp.arange(GRID_W), rows).astype(F32)
    quarter = dim // 4
    freqs = ROPE_THETA ** (-jnp.arange(quarter, dtype=F32) / quarter)
    ang_r = row[:, None] * freqs
    ang_c = col[:, None] * freqs
    fill = LANES // 2 - 2 * quarter
    cos_half = [jnp.cos(ang_r), jnp.cos(ang_c), jnp.ones((seq, fill), F32)]
    sin_half = [jnp.sin(ang_r), jnp.sin(ang_c), jnp.zeros((seq, fill), F32)]
    cos = jnp.concatenate(cos_half + cos_half, axis=-1)
    sin = jnp.concatenate([-t for t in sin_half] + sin_half, axis=-1)
    ctx_rows = lambda value: jnp.full((CTX_LEN, LANES), value, F32)
    return jnp.concatenate([ctx_rows(1.0), cos], axis=0), jnp.concatenate([ctx_rows(0.0), sin], axis=0)


def _pair_layout(v):
    quarter = v.shape[-1] // 4
    a, b, c, d = (v[..., k * quarter:(k + 1) * quarter] for k in range(4))
    zeros = jnp.zeros(v.shape[:-1] + (LANES // 2 - 2 * quarter,), v.dtype)
    return jnp.concatenate([a, c, zeros, b, d, zeros], axis=-1)


def _pair_layout_heads(w, n_heads):
    w = w.reshape(w.shape[:-1] + (n_heads, HEAD_DIM))
    return _pair_layout(w).reshape(w.shape[:-2] + (n_heads * HEAD_DIM,))


def _layout_w_in(w_in):
    sizes = (WIN_HEADS * HEAD_DIM, WIN_KV_HEADS * HEAD_DIM, WIN_KV_HEADS * HEAD_DIM,
             MLA_Q_RANK, MLA_KV_RANK, MLA_ROPE,
             GLB_HEADS * HEAD_DIM, GLB_KV_HEADS * HEAD_DIM, GLB_KV_HEADS * HEAD_DIM)
    offs = np.concatenate([[0], np.cumsum(sizes)])
    wq, wk, wv, mq, mkv, mkr, gq, gk, gv = (w_in[..., int(offs[k]):int(offs[k + 1])] for k in range(9))
    gates = w_in[..., int(offs[9]):]
    wq, gq = _pair_layout_heads(wq, WIN_HEADS), _pair_layout_heads(gq, GLB_HEADS)
    wk, gk = _pair_layout_heads(wk, WIN_KV_HEADS), _pair_layout_heads(gk, GLB_KV_HEADS)
    pad = jnp.zeros(w_in.shape[:-1] + (C_GATE - C_MKR - LANES,), w_in.dtype)
    return jnp.concatenate([wq, gq, wk, wv, gk, gv, mq, mkv, _pair_layout(mkr), pad, gates],
                           axis=-1).astype(BF16)


def _mla_head_layout(w):
    w = w.reshape(w.shape[:-1] + (MLA_HEADS, MLA_NOPE + MLA_ROPE))
    w = jnp.concatenate([w[..., :MLA_NOPE], _pair_layout(w[..., MLA_NOPE:])], axis=-1)
    return w.reshape(w.shape[:-2] + (MLA_HEADS * MLA_HEAD_PAD,))


def kernel(x, c, ctx, c_ctx, w_mod, b_mod, w_in, win_q_norm, win_k_norm, win_sink, mla_q_lat_norm,
           mla_kv_lat_norm, mla_w_q_up, mla_w_kv_up, mla_q_norm, mla_k_norm, glb_q_norm, glb_k_norm,
           w_branch, w_out, ffn_w_gate_up, ffn_w_down, moe_router, moe_w_gate_up, moe_w_down):
    n_batch, seq, d = x.shape
    depth = w_mod.shape[0]
    tpb = CTX_LEN + seq
    assert ctx.shape[1] == CTX_LEN and seq % GRID_W == 0 and seq % BLOCK == 0 and seq >= 3 * BLOCK
    assert n_batch <= MOD_ROWS - 8 and C_MKR + LANES <= C_GATE

    xs = jnp.concatenate([ctx, x], axis=1).reshape(n_batch * tpb, d)
    cvec = jnp.zeros((MOD_ROWS, d), F32).at[:n_batch].set(c).at[MOD_ROWS - 8].set(c_ctx)
    mod = _mod_call(cvec, w_mod, b_mod)

    rope_h = _rope_tables(seq, HEAD_DIM)
    rope_r = _rope_tables(seq, MLA_ROPE)
    w_in_p = _layout_w_in(w_in)
    wq_up = _mla_head_layout(mla_w_q_up).astype(BF16)
    wkv_up = mla_w_kv_up.astype(BF16)
    mla_gain = lambda g: jnp.concatenate([g[:MLA_NOPE], _pair_layout(g[MLA_NOPE:])])[None]
    gqa_gain = lambda g: _pair_layout(g)[None]
    w_branch_b, w_out_b = w_branch.astype(BF16), w_out.astype(BF16)
    ffn_gu, ffn_dn = ffn_w_gate_up.astype(BF16), ffn_w_down.astype(BF16)
    moe_gu, moe_dn = moe_w_gate_up.astype(BF16), moe_w_down.astype(BF16)

    latent_only = False
    for layer in range(depth):
        proj = _inproj_call(xs, mod, w_in_p, layer, tpb)
        q_win, k_win, q_glb, k_glb = _gqa_prep_call(
            proj, [gqa_gain(g[layer]) for g in (win_q_norm, win_k_norm, glb_q_norm, glb_k_norm)], rope_h, tpb)
        q_mla, k_mla, v_mla = _mla_prep_call(
            proj, wq_up, wkv_up, mla_q_lat_norm[layer][None], mla_kv_lat_norm[layer][None],
            mla_gain(mla_q_norm[layer]), mla_gain(mla_k_norm[layer]), rope_r, layer, tpb)
        o_win = _attention_call("win", (q_win, 0), (k_win, 0), (proj, C_WV), dict(sink=win_sink[layer]),
                                n_batch, tpb, False)
        o_mla = _attention_call("mla", (q_mla, 0), (k_mla, 0), (v_mla, 0), {}, n_batch, tpb, False)
        o_glb = _attention_call("glb", (q_glb, 0), (k_glb, 0), (proj, C_GV), {}, n_batch, tpb, False)
        xs = _merge_call(xs, o_win, o_mla, o_glb, proj, mod, w_branch_b, w_out_b, layer, tpb)
        if layer % 2 == 0:
            xs = _ffn_call(xs, mod, ffn_gu, ffn_dn, layer // 2, layer, tpb)
        else:
            skip = 1 if layer == depth - 1 else 0
            h, route = _router_call(xs, mod, moe_router[layer // 2], layer, tpb, skip)
            tile = 768 if 2 * h.shape[0] >= 16 * 768 else 128
            dest, tile_expert, n_active, pad_lo, pad_hi, n_tiles = _moe_plan(route, tile)
            rows = _dispatch_call(h, dest, pad_lo, pad_hi, n_tiles * tile)
            y = _moe_group_call(rows, tile_expert, n_active, moe_gu, moe_dn, layer // 2, tile)
            xs = _combine_call(xs, route, mod, y, dest, layer, tpb, skip)
            latent_only = bool(skip)
    if latent_only:
        return xs.reshape(n_batch, seq, d)
    return xs.reshape(n_batch, tpb, d)[:, CTX_LEN:, :]
```

```python
import functools

import jax
import jax.numpy as jnp
import numpy as np
from jax import lax
from jax.experimental import pallas as pl
from jax.experimental.pallas import tpu as pltpu

CTX_LEN = 256
GRID_W = 64
HEAD_DIM = 128
WIN_HEADS = 8
WIN_KV_HEADS = 2
WINDOW = 128
BLOCK = 128
MLA_HEADS = 8
MLA_NOPE = 128
MLA_ROPE = 64
MLA_V = 128
MLA_Q_RANK = 512
MLA_KV_RANK = 256
GLB_HEADS = 8
GLB_KV_HEADS = 2
BRANCH_W = 1024
N_BRANCH = 3
N_EXPERTS = 8
ROPE_THETA = 10000.0
EPS = 1e-6
NEG_INF = -1e30

LANES = 128
MLA_HEAD_PAD = 256
MOD_ROWS = 24
VMEM_LIMIT = 60 * 1024 * 1024

C_WQ = 0
C_GQ = C_WQ + WIN_HEADS * HEAD_DIM
C_WK = C_GQ + GLB_HEADS * HEAD_DIM
C_WV = C_WK + WIN_KV_HEADS * HEAD_DIM
C_GK = C_WV + WIN_KV_HEADS * HEAD_DIM
C_GV = C_GK + GLB_KV_HEADS * HEAD_DIM
C_MQ = C_GV + GLB_KV_HEADS * HEAD_DIM
C_MKV = C_MQ + MLA_Q_RANK
C_MKR = C_MKV + MLA_KV_RANK
C_GATE = 4096

BF16 = jnp.bfloat16
F32 = jnp.float32


def _largest_divisor(n, candidates):
    for cand in candidates:
        if n % cand == 0:
            return cand
    raise ValueError(f"no tile in {candidates} divides {n}")


def _params(n_axes):
    return pltpu.CompilerParams(dimension_semantics=("arbitrary",) * n_axes,
                                vmem_limit_bytes=VMEM_LIMIT)


def _rms(x, width):
    ms = jnp.sum(x * x, axis=-1, keepdims=True) * (1.0 / width)
    return x * lax.rsqrt(ms + EPS)


def _rope(y, cos, sin):
    return y * cos + pltpu.roll(y, LANES // 2, 1) * sin


def _norm_modulate(x_ref, sh_ref, sc_ref, h_ref, batch, row0, n_rows):
    block = 128
    assert n_rows % block == 0 and CTX_LEN % block == 0
    pick = lambda ref, row: ref[pl.ds(row, 1), :]
    sh_lat, sc_lat = pick(sh_ref, batch), 1.0 + pick(sc_ref, batch)
    sh_ctx, sc_ctx = pick(sh_ref, MOD_ROWS - 8), 1.0 + pick(sc_ref, MOD_ROWS - 8)
    for blk in range(n_rows // block):
        rows = slice(blk * block, (blk + 1) * block)
        is_ctx = row0 + blk * block < CTX_LEN
        sh = jnp.where(is_ctx, sh_ctx, sh_lat)
        sc = jnp.where(is_ctx, sc_ctx, sc_lat)
        xf = x_ref[rows, :]
        h_ref[rows, :] = (_rms(xf, xf.shape[-1]) * sc + sh).astype(h_ref.dtype)


def _gate_rows(g_ref, batch, row0, n_rows):
    rows = lax.broadcasted_iota(jnp.int32, (n_rows, 1), 0) + row0
    g_lat = g_ref[pl.ds(batch, 1), :]
    g_ctx = g_ref[pl.ds(MOD_ROWS - 8, 1), :]
    return jnp.where(rows < CTX_LEN, g_ctx, g_lat)


def _mod_kernel(c_ref, w_ref, b_ref, o_ref):
    cv = c_ref[...]
    sc = (cv * jax.nn.sigmoid(cv)).astype(BF16)
    o_ref[...] = jnp.dot(sc, w_ref[...].astype(BF16), preferred_element_type=F32) + b_ref[...]


def _mod_call(cvec, w_mod, b_mod):
    depth, d, n = w_mod.shape
    tn = _largest_divisor(n, (1024, 512, 256, 128))
    return pl.pallas_call(
        _mod_kernel,
        grid=(depth, n // tn),
        in_specs=[pl.BlockSpec((MOD_ROWS, d), lambda l, j: (0, 0)),
                  pl.BlockSpec((None, d, tn), lambda l, j: (l, 0, j)),
                  pl.BlockSpec((None, 1, tn), lambda l, j: (l, 0, j))],
        out_specs=pl.BlockSpec((None, MOD_ROWS, tn), lambda l, j: (l, 0, j)),
        out_shape=jax.ShapeDtypeStruct((depth, MOD_ROWS, n), F32),
        compiler_params=_params(2),
        name="mod_vectors",
    )(cvec, w_mod, b_mod.reshape(depth, 1, n))


def _inproj_kernel(x_ref, sh_ref, sc_ref, w_ref, o_ref, h_ref, *, tiles_per_batch):
    i = pl.program_id(0)
    tm = x_ref.shape[0]

    @pl.when(pl.program_id(1) == 0)
    def _():
        _norm_modulate(x_ref, sh_ref, sc_ref, h_ref, i // tiles_per_batch,
                       (i % tiles_per_batch) * tm, tm)

    o_ref[...] = jnp.dot(h_ref[...], w_ref[...], preferred_element_type=F32).astype(o_ref.dtype)


def _inproj_call(xs, mod, w_in_p, layer, tpb):
    t, d = xs.shape
    n = w_in_p.shape[-1]
    tm = _largest_divisor(tpb, (768, 384, 256, 128))
    tn = _largest_divisor(n, (2048, 1024, 512, 256, 128))
    kern = functools.partial(_inproj_kernel, tiles_per_batch=tpb // tm)
    return pl.pallas_call(
        kern,
        grid=(t // tm, n // tn),
        in_specs=[pl.BlockSpec((tm, d), lambda i, j: (i, 0)),
                  pl.BlockSpec((None, MOD_ROWS, d), lambda i, j: (layer, 0, 0)),
                  pl.BlockSpec((None, MOD_ROWS, d), lambda i, j: (layer, 0, 1)),
                  pl.BlockSpec((None, d, tn), lambda i, j: (layer, 0, j))],
        out_specs=pl.BlockSpec((tm, tn), lambda i, j: (i, j)),
        out_shape=jax.ShapeDtypeStruct((t, n), BF16),
        scratch_shapes=[pltpu.VMEM((tm, d), BF16)],
        compiler_params=_params(2),
        name="in_projection",
    )(xs, mod, mod, w_in_p)


def _gqa_prep_kernel(wq_ref, wk_ref, gq_ref, gk_ref, wqg_ref, wkg_ref, gqg_ref, gkg_ref,
                     cos_ref, sin_ref, qa_ref, ka_ref, qc_ref, kc_ref):
    cos, sin = cos_ref[...], sin_ref[...]
    q_scale = HEAD_DIM ** -0.5 * LOG2E
    for src, gain, dst, scale in ((wq_ref, wqg_ref, qa_ref, q_scale), (wk_ref, wkg_ref, ka_ref, None),
                                  (gq_ref, gqg_ref, qc_ref, q_scale), (gk_ref, gkg_ref, kc_ref, None)):
        g = gain[...] if scale is None else gain[...] * scale
        for hd in range(src.shape[1] // HEAD_DIM):
            cols = slice(hd * HEAD_DIM, (hd + 1) * HEAD_DIM)
            y = _rms(src[:, cols].astype(F32), HEAD_DIM) * g
            dst[:, cols] = _rope(y, cos, sin).astype(dst.dtype)


def _gqa_prep_call(proj, gains, rope_h, tpb):
    t = proj.shape[0]
    tm = _largest_divisor(tpb, (384, 256, 128))
    per_batch = tpb // tm
    qw, kw = WIN_HEADS * HEAD_DIM, WIN_KV_HEADS * HEAD_DIM
    row_spec = lambda width, col: pl.BlockSpec((tm, width), lambda i: (i, col // width))
    gain_spec = pl.BlockSpec((1, HEAD_DIM), lambda i: (0, 0))
    rope_spec = pl.BlockSpec((tm, LANES), lambda i: (i % per_batch, 0))
    q_out, k_out = jax.ShapeDtypeStruct((t, qw), BF16), jax.ShapeDtypeStruct((t, kw), BF16)
    return pl.pallas_call(
        _gqa_prep_kernel,
        grid=(t // tm,),
        in_specs=[row_spec(qw, C_WQ), row_spec(kw, C_WK), row_spec(qw, C_GQ), row_spec(kw, C_GK),
                  gain_spec, gain_spec, gain_spec, gain_spec, rope_spec, rope_spec],
        out_specs=[pl.BlockSpec((tm, qw), lambda i: (i, 0)), pl.BlockSpec((tm, kw), lambda i: (i, 0)),
                   pl.BlockSpec((tm, qw), lambda i: (i, 0)), pl.BlockSpec((tm, kw), lambda i: (i, 0))],
        out_shape=[q_out, k_out, q_out, k_out],
        compiler_params=_params(1),
        name="gqa_prep",
    )(proj, proj, proj, proj, *gains, *rope_h)


def _mla_prep_kernel(mq_ref, mkv_ref, mkr_ref, wq_ref, wkv_ref, qln_ref, kvln_ref, qn_ref, kn_ref,
                     cos_ref, sin_ref, q_out, k_out, v_out):
    width = MLA_NOPE + MLA_ROPE
    cos, sin = cos_ref[...], sin_ref[...]
    q_lat = (_rms(mq_ref[...].astype(F32), MLA_Q_RANK) * qln_ref[...]).astype(BF16)
    kv_lat = (_rms(mkv_ref[...].astype(F32), MLA_KV_RANK) * kvln_ref[...]).astype(BF16)
    qu = jnp.dot(q_lat, wq_ref[...], preferred_element_type=F32)
    kvu = jnp.dot(kv_lat, wkv_ref[...], preferred_element_type=F32)
    kr = mkr_ref[...].astype(F32)
    kr_sq = jnp.sum(kr * kr, axis=-1, keepdims=True)
    kr_rot = _rope(kr * kn_ref[:, LANES:], cos, sin)
    q_scale = width ** -0.5 * LOG2E
    for hd in range(MLA_HEADS):
        lo = hd * MLA_HEAD_PAD
        qa = qu[:, lo:lo + LANES]
        qb = qu[:, lo + LANES:lo + 2 * LANES]
        ms = jnp.sum(qa * qa + qb * qb, axis=-1, keepdims=True) / width
        rstd = lax.rsqrt(ms + EPS) * q_scale
        q_out[:, lo:lo + LANES] = (qa * rstd * qn_ref[:, :LANES]).astype(q_out.dtype)
        q_out[:, lo + LANES:lo + 2 * LANES] = _rope(qb * rstd * qn_ref[:, LANES:], cos, sin).astype(q_out.dtype)
        kn = kvu[:, lo:lo + MLA_NOPE]
        ms = (jnp.sum(kn * kn, axis=-1, keepdims=True) + kr_sq) / width
        rstd = lax.rsqrt(ms + EPS)
        k_out[:, lo:lo + LANES] = (kn * rstd * kn_ref[:, :LANES]).astype(k_out.dtype)
        k_out[:, lo + LANES:lo + 2 * LANES] = (kr_rot * rstd).astype(k_out.dtype)
        v_out[:, hd * MLA_V:(hd + 1) * MLA_V] = kvu[:, lo + MLA_NOPE:lo + MLA_NOPE + MLA_V].astype(v_out.dtype)


def _mla_prep_call(proj, wq_p, wkv, qln, kvln, qn_p, kn_p, rope_r, layer, tpb):
    t = proj.shape[0]
    tm = _largest_divisor(tpb, (768, 384, 256, 128))
    per_batch = tpb // tm
    hp = MLA_HEADS * MLA_HEAD_PAD
    rope_spec = pl.BlockSpec((tm, LANES), lambda i: (i % per_batch, 0))
    vec = lambda w: pl.BlockSpec((1, w), lambda i: (0, 0))
    return pl.pallas_call(
        _mla_prep_kernel,
        grid=(t // tm,),
        in_specs=[pl.BlockSpec((tm, MLA_Q_RANK), lambda i: (i, C_MQ // MLA_Q_RANK)),
                  pl.BlockSpec((tm, MLA_KV_RANK), lambda i: (i, C_MKV // MLA_KV_RANK)),
                  pl.BlockSpec((tm, LANES), lambda i: (i, C_MKR // LANES)),
                  pl.BlockSpec((None, MLA_Q_RANK, hp), lambda i: (layer, 0, 0)),
                  pl.BlockSpec((None, MLA_KV_RANK, hp), lambda i: (layer, 0, 0)),
                  vec(MLA_Q_RANK), vec(MLA_KV_RANK), vec(MLA_HEAD_PAD), vec(MLA_HEAD_PAD),
                  rope_spec, rope_spec],
        out_specs=[pl.BlockSpec((tm, hp), lambda i: (i, 0)),
                   pl.BlockSpec((tm, hp), lambda i: (i, 0)),
                   pl.BlockSpec((tm, MLA_HEADS * MLA_V), lambda i: (i, 0))],
        out_shape=[jax.ShapeDtypeStruct((t, hp), BF16), jax.ShapeDtypeStruct((t, hp), BF16),
                   jax.ShapeDtypeStruct((t, MLA_HEADS * MLA_V), BF16)],
        compiler_params=_params(1),
        name="mla_prep",
    )(proj, proj, proj, wq_p, wkv, qln, kvln, qn_p, kn_p, *rope_r)


LOG2E = 1.4426950408889634
KEY_CHUNK = 768
UNDERFLOW_GUARD = 2.0 ** -80


def _attention_kernel(*refs, mode, group, tq, seq, tile0):
    if mode == "win":
        sink_ref, q_ref, k_ref, v_ref, o_ref, vt_ref, kn_ref = refs
    else:
        q_ref, k_ref, v_ref, o_ref, vt_ref, kn_ref = refs
    tile = pl.program_id(2) + tile0
    hp, dv, tpb = vt_ref.shape
    dq = k_ref.shape[-1] // hp
    nq = group * tq
    heads = range(hp)
    contract_last = (((1,), (1,)), ((), ()))

    @pl.when(pl.program_id(2) == 0)
    def _():
        for hh in heads:
            vt_ref[hh] = v_ref[:, hh * dv:(hh + 1) * dv].astype(F32).T.astype(BF16)
            kf = k_ref[:, hh * dq:(hh + 1) * dq].astype(F32)
            norms = jnp.dot((kf * kf).astype(BF16), jnp.ones((dq, LANES), BF16), preferred_element_type=F32)
            kn_ref[hh] = jnp.broadcast_to(jnp.max(norms, axis=0, keepdims=True), kn_ref.shape[1:])

    q = [jnp.concatenate([q_ref[:, (hh * group + g) * dq:(hh * group + g + 1) * dq] for g in range(group)],
                         axis=0) for hh in heads]

    def sink_row(hh):
        first = (pl.program_id(1) * hp + hh) * group
        return jnp.concatenate(
            [jnp.full((1, tq), sink_ref[first + g] * LOG2E, F32) for g in range(group)], axis=1)

    def scores(hh, rows):
        return lax.dot_general(k_ref[rows, hh * dq:(hh + 1) * dq], q[hh], contract_last,
                               preferred_element_type=F32)

    def value_dot(hh, cols, p):
        return jnp.dot(vt_ref[hh, :, cols], p.astype(BF16), preferred_element_type=F32)

    def bound_init(hh):
        qf = q[hh].astype(F32)
        q_sq = lax.dot_general(jnp.ones((8, dq), BF16), (qf * qf).astype(BF16), contract_last,
                               preferred_element_type=F32)[0:1, :]
        shift = jnp.sqrt(q_sq * kn_ref[hh, 0:1, 0:1])
        if mode == "win":
            shift = jnp.maximum(shift, sink_row(hh))
            return shift, jnp.exp2(sink_row(hh) - shift), jnp.zeros((dv, nq), F32)
        return shift, jnp.zeros((1, nq), F32), jnp.zeros((dv, nq), F32)

    def bound_update(hh, carry, s, cols, mask=None):
        shift, l, acc = carry
        if mask is not None:
            s = jnp.where(mask(s.shape), s, NEG_INF)
        p = jnp.exp2(s - shift)
        return shift, l + jnp.sum(p, axis=0, keepdims=True), acc + value_dot(hh, cols, p)

    def max_init(hh):
        if mode == "win":
            return sink_row(hh), jnp.ones((1, nq), F32), jnp.zeros((dv, nq), F32)
        return jnp.full((1, nq), NEG_INF, F32), jnp.zeros((1, nq), F32), jnp.zeros((dv, nq), F32)

    def max_update(hh, carry, s, cols, mask=None):
        m, l, acc = carry
        if mask is not None:
            s = jnp.where(mask(s.shape), s, NEG_INF)
        m_new = jnp.maximum(m, jnp.max(s, axis=0, keepdims=True))
        alpha = jnp.exp2(m - m_new)
        p = jnp.exp2(s - m_new)
        return m_new, alpha * l + jnp.sum(p, axis=0, keepdims=True), alpha * acc + value_dot(hh, cols, p)

    def sweep(chunks, init, update):
        carry = [init(hh) for hh in heads]
        s = [scores(hh, chunks[0][0]) for hh in heads]
        for c, (keys, mask) in enumerate(chunks):
            s_next = [scores(hh, chunks[c + 1][0]) for hh in heads] if c + 1 < len(chunks) else None
            carry = [update(hh, carry[hh], s[hh], keys, mask) for hh in heads]
            s = s_next
        return carry

    def write(carry):
        for hh in heads:
            _, l, acc = carry[hh]
            out = acc / l
            for g in range(group):
                col = (hh * group + g) * dv
                o_ref[:, col:col + dv] = out[:, g * tq:(g + 1) * tq].T.astype(o_ref.dtype)

    def attend(chunks):
        fast = sweep(chunks, bound_init, bound_update)
        smallest = functools.reduce(jnp.minimum, [jnp.min(carry[1]) for carry in fast])
        safe = smallest > UNDERFLOW_GUARD

        @pl.when(safe)
        def _():
            write(fast)

        @pl.when(jnp.logical_not(safe))
        def _():
            write(sweep(chunks, max_init, max_update))

    ctx_keys = slice(0, CTX_LEN)

    @pl.when(tile < CTX_LEN // tq)
    def _():
        attend([(ctx_keys, None)])

    @pl.when(tile >= CTX_LEN // tq)
    def _():
        if mode == "win":
            blk = tile - CTX_LEN // tq
            start = jnp.clip((blk - 1) * BLOCK, 0, seq - 3 * BLOCK)
            local = pl.ds(pl.multiple_of(CTX_LEN + start, BLOCK), 3 * BLOCK)

            def band(shape):
                k_pos = start + lax.broadcasted_iota(jnp.int32, shape, 0)
                q_pos = blk * BLOCK + lax.broadcasted_iota(jnp.int32, shape, 1) % tq
                return jnp.abs(q_pos - k_pos) <= WINDOW

            attend([(ctx_keys, None), (local, band)])
        else:
            attend([(slice(c * KEY_CHUNK, (c + 1) * KEY_CHUNK), None) for c in range(tpb // KEY_CHUNK)])


def _attention_call(mode, q_src, k_src, v_src, extras, n_batch, tpb, skip_ctx):
    t = q_src[0].shape[0]
    if mode == "mla":
        n_kv, group, dq, tq, hp = MLA_HEADS, 1, MLA_HEAD_PAD, 256, 4
    else:
        n_kv, group, dq, tq, hp = WIN_KV_HEADS, WIN_HEADS // WIN_KV_HEADS, HEAD_DIM, BLOCK, 2
    dv = HEAD_DIM
    tiles = tpb // tq
    tile0 = CTX_LEN // tq if skip_ctx else 0
    qw, kw, vw = hp * group * dq, hp * dq, hp * dv
    assert q_src[1] % qw == 0 and k_src[1] % kw == 0 and v_src[1] % vw == 0 and n_kv % hp == 0
    q_blk, k_blk, v_blk = q_src[1] // qw, k_src[1] // kw, v_src[1] // vw
    q_spec = pl.BlockSpec((tq, qw), lambda b, h, i: (b * tiles + i + tile0, q_blk + h))
    k_spec = pl.BlockSpec((tpb, kw), lambda b, h, i: (b, k_blk + h))
    v_spec = pl.BlockSpec((tpb, vw), lambda b, h, i: (b, v_blk + h))
    in_specs, args = [], []
    if mode == "win":
        in_specs.append(pl.BlockSpec(memory_space=pltpu.SMEM))
        args.append(extras["sink"])
    in_specs += [q_spec, k_spec, v_spec]
    args += [q_src[0], k_src[0], v_src[0]]
    kern = functools.partial(_attention_kernel, mode=mode, group=group, tq=tq, seq=tpb - CTX_LEN,
                             tile0=tile0)
    return pl.pallas_call(
        kern,
        grid=(n_batch, n_kv // hp, tiles - tile0),
        in_specs=in_specs,
        out_specs=pl.BlockSpec((tq, hp * group * dv), lambda b, h, i: (b * tiles + i + tile0, h)),
        out_shape=jax.ShapeDtypeStruct((t, n_kv * group * dv), BF16),
        scratch_shapes=[pltpu.VMEM((hp, dv, tpb), BF16), pltpu.VMEM((hp, 8, LANES), F32)],
        compiler_params=_params(3),
        name=f"attention_{mode}",
    )(*args)


def _merge_kernel(ow_ref, om_ref, og_ref, gw_ref, gm_ref, gg_ref, wb_ref, wo_ref, x_ref, gate_ref,
                  o_ref, *, tiles_per_batch):
    i, j = pl.program_id(0), pl.program_id(1)
    tm = x_ref.shape[0]

    @pl.when(j == 0)
    def _():
        o_ref[...] = jnp.zeros_like(o_ref)

    y = None
    for br, (o_br, g_br) in enumerate(((ow_ref, gw_ref), (om_ref, gm_ref), (og_ref, gg_ref))):
        term = jax.nn.sigmoid(g_br[...].astype(F32)) * jnp.dot(o_br[...], wb_ref[br],
                                                                preferred_element_type=F32)
        y = term if y is None else y + term
    o_ref[...] += jnp.dot(y.astype(BF16), wo_ref[...], preferred_element_type=F32)

    @pl.when(j == pl.num_programs(1) - 1)
    def _():
        gate = _gate_rows(gate_ref, i // tiles_per_batch, (i % tiles_per_batch) * tm, tm)
        o_ref[...] = x_ref[...] + gate * o_ref[...]


def _merge_call(xs, o_win, o_mla, o_glb, proj, mod, w_branch, w_out, layer, tpb):
    t, d = xs.shape
    tm = _largest_divisor(tpb, (768, 384, 256, 128))
    tn = _largest_divisor(d, (512, 256, 128))
    o_spec = pl.BlockSpec((tm, BRANCH_W), lambda i, j: (i, 0))
    gate_spec = lambda br: pl.BlockSpec((tm, tn), lambda i, j: (i, (C_GATE + br * d) // tn + j))
    kern = functools.partial(_merge_kernel, tiles_per_batch=tpb // tm)
    return pl.pallas_call(
        kern,
        grid=(t // tm, d // tn),
        in_specs=[o_spec, o_spec, o_spec, gate_spec(0), gate_spec(1), gate_spec(2),
                  pl.BlockSpec((None, N_BRANCH, BRANCH_W, tn), lambda i, j: (layer, 0, 0, j)),
                  pl.BlockSpec((None, tn, d), lambda i, j: (layer, j, 0)),
                  pl.BlockSpec((tm, d), lambda i, j: (i, 0)),
                  pl.BlockSpec((None, MOD_ROWS, d), lambda i, j: (layer, 0, 2))],
        out_specs=pl.BlockSpec((tm, d), lambda i, j: (i, 0)),
        out_shape=jax.ShapeDtypeStruct((t, d), F32),
        input_output_aliases={8: 0},
        compiler_params=_params(2),
        name="merge_out_projection",
    )(o_win, o_mla, o_glb, proj, proj, proj, w_branch, w_out, xs, mod)


def _ffn_kernel(x_ref, sh_ref, sc_ref, gate_ref, wg_ref, wu_ref, wd_ref, o_ref, h_ref, *, tiles_per_batch):
    i, f = pl.program_id(0), pl.program_id(1)
    tm = x_ref.shape[0]
    batch, row0 = i // tiles_per_batch, (i % tiles_per_batch) * tm

    @pl.when(f == 0)
    def _():
        _norm_modulate(x_ref, sh_ref, sc_ref, h_ref, batch, row0, tm)
        o_ref[...] = jnp.zeros_like(o_ref)

    h = h_ref[...]
    g = jnp.dot(h, wg_ref[...], preferred_element_type=F32)
    u = jnp.dot(h, wu_ref[...], preferred_element_type=F32)
    act = (g * jax.nn.sigmoid(g) * u).astype(BF16)
    o_ref[...] += jnp.dot(act, wd_ref[...], preferred_element_type=F32)

    @pl.when(f == pl.num_programs(1) - 1)
    def _():
        o_ref[...] = x_ref[...] + _gate_rows(gate_ref, batch, row0, tm) * o_ref[...]


def _ffn_call(xs, mod, w_gate_up, w_down, idx, layer, tpb):
    t, d = xs.shape
    ff = w_down.shape[1]
    tm = _largest_divisor(tpb, (768, 384, 256, 128))
    tf = _largest_divisor(ff, (512, 256, 128))
    n_f = ff // tf
    mod_spec = lambda k: pl.BlockSpec((None, MOD_ROWS, d), lambda i, f: (layer, 0, k))
    kern = functools.partial(_ffn_kernel, tiles_per_batch=tpb // tm)
    return pl.pallas_call(
        kern,
        grid=(t // tm, n_f),
        in_specs=[pl.BlockSpec((tm, d), lambda i, f: (i, 0)), mod_spec(3), mod_spec(4), mod_spec(5),
                  pl.BlockSpec((None, d, tf), lambda i, f: (idx, 0, f)),
                  pl.BlockSpec((None, d, tf), lambda i, f: (idx, 0, f + n_f)),
                  pl.BlockSpec((None, tf, d), lambda i, f: (idx, f, 0))],
        out_specs=pl.BlockSpec((tm, d), lambda i, f: (i, 0)),
        out_shape=jax.ShapeDtypeStruct((t, d), F32),
        scratch_shapes=[pltpu.VMEM((tm, d), BF16)],
        input_output_aliases={0: 0},
        compiler_params=_params(2),
        name="swiglu_ffn",
    )(xs, mod, mod, mod, w_gate_up, w_gate_up, w_down)


MOE_TOKEN_TILE = 256


def _moe_tile_map(step, tiles_per_batch, skip):
    per_batch = tiles_per_batch - skip
    batch, inner = step // per_batch, step % per_batch + skip
    return batch * tiles_per_batch + inner, batch, inner * MOE_TOKEN_TILE


def _router_kernel(x_ref, sh_ref, sc_ref, rhi_ref, rlo_ref, h_ref, o_ref, *, tiles_per_batch, skip):
    tm = x_ref.shape[0]
    _, batch, row0 = _moe_tile_map(pl.program_id(0), tiles_per_batch, skip)
    _norm_modulate(x_ref, sh_ref, sc_ref, h_ref, batch, row0, tm)
    h = h_ref[...]
    h_hi = h.astype(BF16)
    h_lo = (h - h_hi.astype(F32)).astype(BF16)
    logits = (jnp.dot(h_hi, rhi_ref[...], preferred_element_type=F32)
              + jnp.dot(h_hi, rlo_ref[...], preferred_element_type=F32)
              + jnp.dot(h_lo, rhi_ref[...], preferred_element_type=F32))
    lane = lax.broadcasted_iota(jnp.int32, logits.shape, 1).astype(F32)
    logits = jnp.where(lane < N_EXPERTS, logits, -jnp.inf)
    top1 = jnp.max(logits, axis=-1, keepdims=True)
    idx1 = jnp.min(jnp.where(logits == top1, lane, float(LANES)), axis=-1, keepdims=True)
    rest = jnp.where(lane == idx1, -jnp.inf, logits)
    top2 = jnp.max(rest, axis=-1, keepdims=True)
    idx2 = jnp.min(jnp.where(rest == top2, lane, float(LANES)), axis=-1, keepdims=True)
    e2 = jnp.exp(top2 - top1)
    den = 1.0 + e2
    o_ref[...] = jnp.where(lane == 0, 1.0 / den, jnp.where(lane == 1, e2 / den, jnp.where(
        lane == 2, idx1, jnp.where(lane == 3, idx2, 0.0))))


def _router_call(xs, mod, router, layer, tpb, skip):
    t, d = xs.shape
    tm = MOE_TOKEN_TILE
    tiles_per_batch = tpb // tm
    n_tok = t // tiles_per_batch * (tiles_per_batch - skip)
    r_hi = jnp.pad(router, ((0, 0), (0, LANES - N_EXPERTS))).astype(BF16)
    r_lo = jnp.pad(router - r_hi[:, :N_EXPERTS].astype(F32), ((0, 0), (0, LANES - N_EXPERTS))).astype(BF16)
    mod_spec = lambda k: pl.BlockSpec((None, MOD_ROWS, d), lambda i: (layer, 0, k))
    r_spec = pl.BlockSpec((d, LANES), lambda i: (0, 0))
    kern = functools.partial(_router_kernel, tiles_per_batch=tiles_per_batch, skip=skip)
    return pl.pallas_call(
        kern,
        grid=(n_tok // tm,),
        in_specs=[pl.BlockSpec((tm, d), lambda i: (_moe_tile_map(i, tiles_per_batch, skip)[0], 0)),
                  mod_spec(3), mod_spec(4), r_spec, r_spec],
        out_specs=[pl.BlockSpec((tm, d), lambda i: (i, 0)), pl.BlockSpec((tm, LANES), lambda i: (i, 0))],
        out_shape=[jax.ShapeDtypeStruct((n_tok, d), F32), jax.ShapeDtypeStruct((n_tok, LANES), F32)],
        compiler_params=_params(1),
        name="moe_router",
    )(xs, mod, mod, r_hi, r_lo)


def _moe_plan(route, tile):
    n_pairs = 2 * route.shape[0]
    n_tiles = -(-n_pairs // tile) + N_EXPERTS
    expert = route[:, 2:4].astype(jnp.int32).reshape(-1)
    onehot = (expert[:, None] == jnp.arange(N_EXPERTS, dtype=jnp.int32)[None]).astype(jnp.int32)
    csum = jnp.cumsum(onehot, axis=0)
    count = csum[-1]
    padded = (count + tile - 1) // tile * tile
    group_end = jnp.cumsum(padded)
    group_start = group_end - padded
    dest = jnp.sum(onehot * (group_start[None] + csum - 1), axis=1).astype(jnp.int32)
    tile_row = jnp.arange(n_tiles, dtype=jnp.int32) * tile
    tile_expert = jnp.minimum(jnp.sum((tile_row[:, None] >= group_end[None]).astype(jnp.int32), axis=1),
                              N_EXPERTS - 1).astype(jnp.int32)
    n_active = (group_end[-1] // tile).astype(jnp.int32).reshape(1)
    pad_start = jnp.concatenate([group_start + count, group_end[-1:]]).astype(jnp.int32)
    pad_end = jnp.concatenate([group_end, jnp.array([n_tiles * tile], jnp.int32)]).astype(jnp.int32)
    return dest, tile_expert, n_active, pad_start, pad_end, n_tiles


DISPATCH_TILE = 512
DMA_UNROLL = 8


def _dispatch_kernel(pad_lo_ref, pad_hi_ref, dest_ref, h_ref, rows_hbm, zero_ref, sem):
    tm = h_ref.shape[0]

    def row_copy(src_row, dst):
        return pltpu.make_async_copy(src_row, rows_hbm.at[pl.ds(dst, 1), :], sem.at[0])

    @pl.when(pl.program_id(0) == 0)
    def _():
        zero_ref[...] = jnp.zeros_like(zero_ref)
        zero_row = zero_ref.at[pl.ds(0, 1), :]
        for e in range(N_EXPERTS + 1):
            lo, hi = pad_lo_ref[e], pad_hi_ref[e]

            def start_zero(r, carry):
                row_copy(zero_row, r).start()
                return carry

            def wait_zero(r, carry):
                row_copy(zero_row, r).wait()
                return carry

            lax.fori_loop(lo, hi, start_zero, 0)
            lax.fori_loop(lo, hi, wait_zero, 0)

    def start(r, carry):
        for k in range(2):
            row_copy(h_ref.at[pl.ds(r, 1), :], dest_ref[0, 2 * r + k]).start()
        return carry

    def wait(r, carry):
        for k in range(2):
            row_copy(h_ref.at[pl.ds(r, 1), :], dest_ref[0, 2 * r + k]).wait()
        return carry

    lax.fori_loop(0, tm, start, 0, unroll=DMA_UNROLL)
    lax.fori_loop(0, tm, wait, 0, unroll=DMA_UNROLL)


def _dispatch_call(h, dest, pad_lo, pad_hi, n_rows):
    t, d = h.shape
    tm = DISPATCH_TILE
    grid_spec = pltpu.PrefetchScalarGridSpec(
        num_scalar_prefetch=2,
        grid=(t // tm,),
        in_specs=[pl.BlockSpec((None, 1, 2 * tm), lambda i, lo, hi: (i, 0, 0), memory_space=pltpu.SMEM),
                  pl.BlockSpec((tm, d), lambda i, lo, hi: (i, 0))],
        out_specs=pl.BlockSpec(memory_space=pl.ANY),
        scratch_shapes=[pltpu.VMEM((8, d), F32), pltpu.SemaphoreType.DMA((1,))],
    )
    return pl.pallas_call(
        _dispatch_kernel,
        grid_spec=grid_spec,
        out_shape=jax.ShapeDtypeStruct((n_rows, d), F32),
        compiler_params=_params(1),
        name="moe_dispatch",
    )(pad_lo, pad_hi, dest.reshape(t // tm, 1, 2 * tm), h)


def _moe_group_kernel(expert_ref, n_active_ref, x_ref, wg_ref, wu_ref, wd_ref, o_ref, h_ref):
    m, f = pl.program_id(0), pl.program_id(1)

    @pl.when(f == 0)
    def _():
        h_ref[...] = x_ref[...].astype(BF16)
        o_ref[...] = jnp.zeros_like(o_ref)

    @pl.when(m < n_active_ref[0])
    def _():
        h = h_ref[...]
        g = jnp.dot(h, wg_ref[...], preferred_element_type=F32)
        u = jnp.dot(h, wu_ref[...], preferred_element_type=F32)
        act = (g * jax.nn.sigmoid(g) * u).astype(BF16)
        o_ref[...] += jnp.dot(act, wd_ref[...], preferred_element_type=F32)


def _moe_group_call(rows, tile_expert, n_active, w_gate_up, w_down, idx, tile):
    n_rows, d = rows.shape
    ff = w_down.shape[2]
    tf = _largest_divisor(ff, (512, 256, 128))
    n_f = ff // tf

    def f_eff(m, f, na):
        return jnp.where(m < na[0], f, n_f - 1)

    grid_spec = pltpu.PrefetchScalarGridSpec(
        num_scalar_prefetch=2,
        grid=(n_rows // tile, n_f),
        in_specs=[pl.BlockSpec((tile, d), lambda m, f, te, na: (m, 0)),
                  pl.BlockSpec((None, None, d, tf), lambda m, f, te, na: (idx, te[m], 0, f_eff(m, f, na))),
                  pl.BlockSpec((None, None, d, tf),
                               lambda m, f, te, na: (idx, te[m], 0, f_eff(m, f, na) + n_f)),
                  pl.BlockSpec((None, None, tf, d), lambda m, f, te, na: (idx, te[m], f_eff(m, f, na), 0))],
        out_specs=pl.BlockSpec((tile, d), lambda m, f, te, na: (m, 0)),
        scratch_shapes=[pltpu.VMEM((tile, d), BF16)],
    )
    return pl.pallas_call(
        _moe_group_kernel,
        grid_spec=grid_spec,
        out_shape=jax.ShapeDtypeStruct((n_rows, d), F32),
        compiler_params=_params(2),
        name="moe_experts",
    )(tile_expert, n_active, rows, w_gate_up, w_gate_up, w_down)


def _combine_kernel(dest_ref, next_ref, x_ref, route_ref, gate_ref, y_hbm, o_ref, buf, sem, *,
                    tiles_per_batch, skip):
    i, n = pl.program_id(0), pl.num_programs(0)
    tm = x_ref.shape[0]
    slot = i % 2

    def row_copy(dests, to, r, k):
        return pltpu.make_async_copy(y_hbm.at[pl.ds(dests[0, 2 * r + k], 1), :],
                                     buf.at[to, k, pl.ds(r, 1), :], sem.at[to, k])

    def gather(dests, to, wait):
        def body(r, carry):
            for k in range(2):
                copy = row_copy(dests, to, r, k)
                copy.wait() if wait else copy.start()
            return carry
        lax.fori_loop(0, tm, body, 0, unroll=DMA_UNROLL)

    @pl.when(i == 0)
    def _():
        gather(dest_ref, slot, wait=False)

    @pl.when(i + 1 < n)
    def _():
        gather(next_ref, 1 - slot, wait=False)

    gather(dest_ref, slot, wait=True)
    route = route_ref[...]
    mix = route[:, 0:1] * buf[slot, 0] + route[:, 1:2] * buf[slot, 1]
    _, batch, row0 = _moe_tile_map(i, tiles_per_batch, skip)
    o_ref[...] = x_ref[...] + _gate_rows(gate_ref, batch, row0, tm) * mix


def _combine_call(xs, route, mod, y, dest, layer, tpb, skip):
    t, d = xs.shape
    tm = MOE_TOKEN_TILE
    tiles_per_batch = tpb // tm
    n_tok = route.shape[0]
    n_steps = n_tok // tm
    dest = dest.reshape(n_steps, 1, 2 * tm)
    kern = functools.partial(_combine_kernel, tiles_per_batch=tiles_per_batch, skip=skip)
    smem_spec = lambda index: pl.BlockSpec((None, 1, 2 * tm), index, memory_space=pltpu.SMEM)
    return pl.pallas_call(
        kern,
        grid=(n_steps,),
        in_specs=[smem_spec(lambda i: (i, 0, 0)),
                  smem_spec(lambda i: (jnp.minimum(i + 1, n_steps - 1), 0, 0)),
                  pl.BlockSpec((tm, d), lambda i: (_moe_tile_map(i, tiles_per_batch, skip)[0], 0)),
                  pl.BlockSpec((tm, LANES), lambda i: (i, 0)),
                  pl.BlockSpec((None, MOD_ROWS, d), lambda i: (layer, 0, 5)),
                  pl.BlockSpec(memory_space=pl.ANY)],
        out_specs=pl.BlockSpec((tm, d), lambda i: (i, 0)),
        out_shape=jax.ShapeDtypeStruct((n_tok, d), F32),
        scratch_shapes=[pltpu.VMEM((2, 2, tm, d), F32), pltpu.SemaphoreType.DMA((2, 2))],
        input_output_aliases={} if skip else {2: 0},
        compiler_params=_params(1),
        name="moe_combine",
    )(dest, dest, xs, route, mod, y)


def _rope_tables(seq, dim):
    rows = seq // GRID_W
    row = jnp.repeat(jnp.arange(rows), GRID_W).astype(F32)
    col = jnp.tile(jnp.arange(GRID_W), rows).astype(F32)
    quarter = dim // 4
    freqs = ROPE_THETA ** (-jnp.arange(quarter, dtype=F32) / quarter)
    ang_r = row[:, None] * freqs
    ang_c = col[:, None] * freqs
    fill = LANES // 2 - 2 * quarter
    cos_half = [jnp.cos(ang_r), jnp.cos(ang_c), jnp.ones((seq, fill), F32)]
    sin_half = [jnp.sin(ang_r), jnp.sin(ang_c), jnp.zeros((seq, fill), F32)]
    cos = jnp.concatenate(cos_half + cos_half, axis=-1)
    sin = jnp.concatenate([-t for t in sin_half] + sin_half, axis=-1)
    ctx_rows = lambda value: jnp.full((CTX_LEN, LANES), value, F32)
    return jnp.concatenate([ctx_rows(1.0), cos], axis=0), jnp.concatenate([ctx_rows(0.0), sin], axis=0)


def _pair_layout(v):
    quarter = v.shape[-1] // 4
    a, b, c, d = (v[..., k * quarter:(k + 1) * quarter] for k in range(4))
    zeros = jnp.zeros(v.shape[:-1] + (LANES // 2 - 2 * quarter,), v.dtype)
    return jnp.concatenate([a, c, zeros, b, d, zeros], axis=-1)


def _pair_layout_heads(w, n_heads):
    w = w.reshape(w.shape[:-1] + (n_heads, HEAD_DIM))
    return _pair_layout(w).reshape(w.shape[:-2] + (n_heads * HEAD_DIM,))


def _layout_w_in(w_in):
    sizes = (WIN_HEADS * HEAD_DIM, WIN_KV_HEADS * HEAD_DIM, WIN_KV_HEADS * HEAD_DIM,
             MLA_Q_RANK, MLA_KV_RANK, MLA_ROPE,
             GLB_HEADS * HEAD_DIM, GLB_KV_HEADS * HEAD_DIM, GLB_KV_HEADS * HEAD_DIM)
    offs = np.concatenate([[0], np.cumsum(sizes)])
    wq, wk, wv, mq, mkv, mkr, gq, gk, gv = (w_in[..., int(offs[k]):int(offs[k + 1])] for k in range(9))
    gates = w_in[..., int(offs[9]):]
    wq, gq = _pair_layout_heads(wq, WIN_HEADS), _pair_layout_heads(gq, GLB_HEADS)
    wk, gk = _pair_layout_heads(wk, WIN_KV_HEADS), _pair_layout_heads(gk, GLB_KV_HEADS)
    pad = jnp.zeros(w_in.shape[:-1] + (C_GATE - C_MKR - LANES,), w_in.dtype)
    return jnp.concatenate([wq, gq, wk, wv, gk, gv, mq, mkv, _pair_layout(mkr), pad, gates],
                           axis=-1).astype(BF16)


def _mla_head_layout(w):
    w = w.reshape(w.shape[:-1] + (MLA_HEADS, MLA_NOPE + MLA_ROPE))
    w = jnp.concatenate([w[..., :MLA_NOPE], _pair_layout(w[..., MLA_NOPE:])], axis=-1)
    return w.reshape(w.shape[:-2] + (MLA_HEADS * MLA_HEAD_PAD,))


def kernel(x, c, ctx, c_ctx, w_mod, b_mod, w_in, win_q_norm, win_k_norm, win_sink, mla_q_lat_norm,
           mla_kv_lat_norm, mla_w_q_up, mla_w_kv_up, mla_q_norm, mla_k_norm, glb_q_norm, glb_k_norm,
           w_branch, w_out, ffn_w_gate_up, ffn_w_down, moe_router, moe_w_gate_up, moe_w_down):
    n_batch, seq, d = x.shape
    depth = w_mod.shape[0]
    tpb = CTX_LEN + seq
    assert ctx.shape[1] == CTX_LEN and seq % GRID_W == 0 and seq % BLOCK == 0 and seq >= 3 * BLOCK
    assert n_batch <= MOD_ROWS - 8 and C_MKR + LANES <= C_GATE

    xs = jnp.concatenate([ctx, x], axis=1).reshape(n_batch * tpb, d)
    cvec = jnp.zeros((MOD_ROWS, d), F32).at[:n_batch].set(c).at[MOD_ROWS - 8].set(c_ctx)
    mod = _mod_call(cvec, w_mod, b_mod)

    rope_h = _rope_tables(seq, HEAD_DIM)
    rope_r = _rope_tables(seq, MLA_ROPE)
    w_in_p = _layout_w_in(w_in)
    wq_up = _mla_head_layout(mla_w_q_up).astype(BF16)
    wkv_up = mla_w_kv_up.astype(BF16)
    mla_gain = lambda g: jnp.concatenate([g[:MLA_NOPE], _pair_layout(g[MLA_NOPE:])])[None]
    gqa_gain = lambda g: _pair_layout(g)[None]
    w_branch_b, w_out_b = w_branch.astype(BF16), w_out.astype(BF16)
    ffn_gu, ffn_dn = ffn_w_gate_up.astype(BF16), ffn_w_down.astype(BF16)
    moe_gu, moe_dn = moe_w_gate_up.astype(BF16), moe_w_down.astype(BF16)

    latent_only = False
    for layer in range(depth):
        proj = _inproj_call(xs, mod, w_in_p, layer, tpb)
        q_win, k_win, q_glb, k_glb = _gqa_prep_call(
            proj, [gqa_gain(g[layer]) for g in (win_q_norm, win_k_norm, glb_q_norm, glb_k_norm)], rope_h, tpb)
        q_mla, k_mla, v_mla = _mla_prep_call(
            proj, wq_up, wkv_up, mla_q_lat_norm[layer][None], mla_kv_lat_norm[layer][None],
            mla_gain(mla_q_norm[layer]), mla_gain(mla_k_norm[layer]), rope_r, layer, tpb)
        o_win = _attention_call("win", (q_win, 0), (k_win, 0), (proj, C_WV), dict(sink=win_sink[layer]),
                                n_batch, tpb, False)
        o_mla = _attention_call("mla", (q_mla, 0), (k_mla, 0), (v_mla, 0), {}, n_batch, tpb, False)
        o_glb = _attention_call("glb", (q_glb, 0), (k_glb, 0), (proj, C_GV), {}, n_batch, tpb, False)
        xs = _merge_call(xs, o_win, o_mla, o_glb, proj, mod, w_branch_b, w_out_b, layer, tpb)
        if layer % 2 == 0:
            xs = _ffn_call(xs, mod, ffn_gu, ffn_dn, layer // 2, layer, tpb)
        else:
            skip = 1 if layer == depth - 1 else 0
            h, route = _router_call(xs, mod, moe_router[layer // 2], layer, tpb, skip)
            tile = 768 if 2 * h.shape[0] >= 16 * 768 else 128
            dest, tile_expert, n_active, pad_lo, pad_hi, n_tiles = _moe_plan(route, tile)
            rows = _dispatch_call(h, dest, pad_lo, pad_hi, n_tiles * tile)
            y = _moe_group_call(rows, tile_expert, n_active, moe_gu, moe_dn, layer // 2, tile)
            xs = _combine_call(xs, route, mod, y, dest, layer, tpb, skip)
            latent_only = bool(skip)
    if latent_only:
        return xs.reshape(n_batch, seq, d)
    return xs.reshape(n_batch, tpb, d)[:, CTX_LEN:, :]
```

```python
import functools

import jax
import jax.numpy as jnp
import numpy as np
from jax import lax
from jax.experimental import pallas as pl
from jax.experimental.pallas import tpu as pltpu

CTX_LEN = 256
GRID_W = 64
HEAD_DIM = 128
WIN_HEADS = 8
WIN_KV_HEADS = 2
WINDOW = 128
BLOCK = 128
MLA_HEADS = 8
MLA_NOPE = 128
MLA_ROPE = 64
MLA_V = 128
MLA_Q_RANK = 512
MLA_KV_RANK = 256
GLB_HEADS = 8
GLB_KV_HEADS = 2
BRANCH_W = 1024
N_BRANCH = 3
N_EXPERTS = 8
ROPE_THETA = 10000.0
EPS = 1e-6
NEG_INF = -1e30

LANES = 128
MLA_HEAD_PAD = 256
MOD_ROWS = 24
VMEM_LIMIT = 60 * 1024 * 1024

C_WQ = 0
C_GQ = C_WQ + WIN_HEADS * HEAD_DIM
C_WK = C_GQ + GLB_HEADS * HEAD_DIM
C_WV = C_WK + WIN_KV_HEADS * HEAD_DIM
C_GK = C_WV + WIN_KV_HEADS * HEAD_DIM
C_GV = C_GK + GLB_KV_HEADS * HEAD_DIM
C_MQ = C_GV + GLB_KV_HEADS * HEAD_DIM
C_MKV = C_MQ + MLA_Q_RANK
C_MKR = C_MKV + MLA_KV_RANK
C_GATE = 4096

BF16 = jnp.bfloat16
F32 = jnp.float32


def _largest_divisor(n, candidates):
    for cand in candidates:
        if n % cand == 0:
            return cand
    raise ValueError(f"no tile in {candidates} divides {n}")


def _params(n_axes):
    return pltpu.CompilerParams(dimension_semantics=("arbitrary",) * n_axes,
                                vmem_limit_bytes=VMEM_LIMIT)


def _rms(x, width):
    ms = jnp.sum(x * x, axis=-1, keepdims=True) * (1.0 / width)
    return x * lax.rsqrt(ms + EPS)


def _rope(y, cos, sin):
    return y * cos + pltpu.roll(y, LANES // 2, 1) * sin


def _norm_modulate(x_ref, sh_ref, sc_ref, h_ref, batch, row0, n_rows):
    block = 128
    assert n_rows % block == 0 and CTX_LEN % block == 0
    pick = lambda ref, row: ref[pl.ds(row, 1), :]
    sh_lat, sc_lat = pick(sh_ref, batch), 1.0 + pick(sc_ref, batch)
    sh_ctx, sc_ctx = pick(sh_ref, MOD_ROWS - 8), 1.0 + pick(sc_ref, MOD_ROWS - 8)
    for blk in range(n_rows // block):
        rows = slice(blk * block, (blk + 1) * block)
        is_ctx = row0 + blk * block < CTX_LEN
        sh = jnp.where(is_ctx, sh_ctx, sh_lat)
        sc = jnp.where(is_ctx, sc_ctx, sc_lat)
        xf = x_ref[rows, :]
        h_ref[rows, :] = (_rms(xf, xf.shape[-1]) * sc + sh).astype(h_ref.dtype)


def _gate_rows(g_ref, batch, row0, n_rows):
    rows = lax.broadcasted_iota(jnp.int32, (n_rows, 1), 0) + row0
    g_lat = g_ref[pl.ds(batch, 1), :]
    g_ctx = g_ref[pl.ds(MOD_ROWS - 8, 1), :]
    return jnp.where(rows < CTX_LEN, g_ctx, g_lat)


def _mod_kernel(c_ref, w_ref, b_ref, o_ref):
    cv = c_ref[...]
    sc = (cv * jax.nn.sigmoid(cv)).astype(BF16)
    o_ref[...] = jnp.dot(sc, w_ref[...].astype(BF16), preferred_element_type=F32) + b_ref[...]


def _mod_call(cvec, w_mod, b_mod):
    depth, d, n = w_mod.shape
    tn = _largest_divisor(n, (1024, 512, 256, 128))
    return pl.pallas_call(
        _mod_kernel,
        grid=(depth, n // tn),
        in_specs=[pl.BlockSpec((MOD_ROWS, d), lambda l, j: (0, 0)),
                  pl.BlockSpec((None, d, tn), lambda l, j: (l, 0, j)),
                  pl.BlockSpec((None, 1, tn), lambda l, j: (l, 0, j))],
        out_specs=pl.BlockSpec((None, MOD_ROWS, tn), lambda l, j: (l, 0, j)),
        out_shape=jax.ShapeDtypeStruct((depth, MOD_ROWS, n), F32),
        compiler_params=_params(2),
        name="mod_vectors",
    )(cvec, w_mod, b_mod.reshape(depth, 1, n))


def _inproj_kernel(x_ref, sh_ref, sc_ref, w_ref, o_ref, h_ref, *, tiles_per_batch):
    i = pl.program_id(0)
    tm = x_ref.shape[0]

    @pl.when(pl.program_id(1) == 0)
    def _():
        _norm_modulate(x_ref, sh_ref, sc_ref, h_ref, i // tiles_per_batch,
                       (i % tiles_per_batch) * tm, tm)

    o_ref[...] = jnp.dot(h_ref[...], w_ref[...], preferred_element_type=F32).astype(o_ref.dtype)


def _inproj_call(xs, mod, w_in_p, layer, tpb):
    t, d = xs.shape
    n = w_in_p.shape[-1]
    tm = _largest_divisor(tpb, (768, 384, 256, 128))
    tn = _largest_divisor(n, (2048, 1024, 512, 256, 128))
    kern = functools.partial(_inproj_kernel, tiles_per_batch=tpb // tm)
    return pl.pallas_call(
        kern,
        grid=(t // tm, n // tn),
        in_specs=[pl.BlockSpec((tm, d), lambda i, j: (i, 0)),
                  pl.BlockSpec((None, MOD_ROWS, d), lambda i, j: (layer, 0, 0)),
                  pl.BlockSpec((None, MOD_ROWS, d), lambda i, j: (layer, 0, 1)),
                  pl.BlockSpec((None, d, tn), lambda i, j: (layer, 0, j))],
        out_specs=pl.BlockSpec((tm, tn), lambda i, j: (i, j)),
        out_shape=jax.ShapeDtypeStruct((t, n), BF16),
        scratch_shapes=[pltpu.VMEM((tm, d), BF16)],
        compiler_params=_params(2),
        name="in_projection",
    )(xs, mod, mod, w_in_p)


def _gqa_prep_kernel(wq_ref, wk_ref, gq_ref, gk_ref, wqg_ref, wkg_ref, gqg_ref, gkg_ref,
                     cos_ref, sin_ref, qa_ref, ka_ref, qc_ref, kc_ref):
    cos, sin = cos_ref[...], sin_ref[...]
    q_scale = HEAD_DIM ** -0.5 * LOG2E
    for src, gain, dst, scale in ((wq_ref, wqg_ref, qa_ref, q_scale), (wk_ref, wkg_ref, ka_ref, None),
                                  (gq_ref, gqg_ref, qc_ref, q_scale), (gk_ref, gkg_ref, kc_ref, None)):
        g = gain[...] if scale is None else gain[...] * scale
        for hd in range(src.shape[1] // HEAD_DIM):
            cols = slice(hd * HEAD_DIM, (hd + 1) * HEAD_DIM)
            y = _rms(src[:, cols].astype(F32), HEAD_DIM) * g
            dst[:, cols] = _rope(y, cos, sin).astype(dst.dtype)


def _gqa_prep_call(proj, gains, rope_h, tpb):
    t = proj.shape[0]
    tm = _largest_divisor(tpb, (384, 256, 128))
    per_batch = tpb // tm
    qw, kw = WIN_HEADS * HEAD_DIM, WIN_KV_HEADS * HEAD_DIM
    row_spec = lambda width, col: pl.BlockSpec((tm, width), lambda i: (i, col // width))
    gain_spec = pl.BlockSpec((1, HEAD_DIM), lambda i: (0, 0))
    rope_spec = pl.BlockSpec((tm, LANES), lambda i: (i % per_batch, 0))
    q_out, k_out = jax.ShapeDtypeStruct((t, qw), BF16), jax.ShapeDtypeStruct((t, kw), BF16)
    return pl.pallas_call(
        _gqa_prep_kernel,
        grid=(t // tm,),
        in_specs=[row_spec(qw, C_WQ), row_spec(kw, C_WK), row_spec(qw, C_GQ), row_spec(kw, C_GK),
                  gain_spec, gain_spec, gain_spec, gain_spec, rope_spec, rope_spec],
        out_specs=[pl.BlockSpec((tm, qw), lambda i: (i, 0)), pl.BlockSpec((tm, kw), lambda i: (i, 0)),
                   pl.BlockSpec((tm, qw), lambda i: (i, 0)), pl.BlockSpec((tm, kw), lambda i: (i, 0))],
        out_shape=[q_out, k_out, q_out, k_out],
        compiler_params=_params(1),
        name="gqa_prep",
    )(proj, proj, proj, proj, *gains, *rope_h)


def _mla_prep_kernel(mq_ref, mkv_ref, mkr_ref, wq_ref, wkv_ref, qln_ref, kvln_ref, qn_ref, kn_ref,
                     cos_ref, sin_ref, q_out, k_out, v_out):
    width = MLA_NOPE + MLA_ROPE
    cos, sin = cos_ref[...], sin_ref[...]
    q_lat = (_rms(mq_ref[...].astype(F32), MLA_Q_RANK) * qln_ref[...]).astype(BF16)
    kv_lat = (_rms(mkv_ref[...].astype(F32), MLA_KV_RANK) * kvln_ref[...]).astype(BF16)
    kr = mkr_ref[...].astype(F32)
    kr_sq = jnp.sum(kr * kr, axis=-1, keepdims=True)
    kr_rot = _rope(kr * kn_ref[:, LANES:], cos, sin)
    q_scale = width ** -0.5 * LOG2E
    for hd in range(MLA_HEADS):
        lo = hd * MLA_HEAD_PAD
        qu = jnp.dot(q_lat, wq_ref[:, lo:lo + MLA_HEAD_PAD], preferred_element_type=F32)
        kvu = jnp.dot(kv_lat, wkv_ref[:, lo:lo + MLA_HEAD_PAD], preferred_element_type=F32)
        qa = qu[:, :LANES]
        qb = qu[:, LANES:]
        ms = jnp.sum(qa * qa + qb * qb, axis=-1, keepdims=True) / width
        rstd = lax.rsqrt(ms + EPS) * q_scale
        q_out[:, lo:lo + LANES] = (qa * rstd * qn_ref[:, :LANES]).astype(q_out.dtype)
        q_out[:, lo + LANES:lo + 2 * LANES] = _rope(qb * rstd * qn_ref[:, LANES:], cos, sin).astype(q_out.dtype)
        kn = kvu[:, :MLA_NOPE]
        ms = (jnp.sum(kn * kn, axis=-1, keepdims=True) + kr_sq) / width
        rstd = lax.rsqrt(ms + EPS)
        k_out[:, lo:lo + LANES] = (kn * rstd * kn_ref[:, :LANES]).astype(k_out.dtype)
        k_out[:, lo + LANES:lo + 2 * LANES] = (kr_rot * rstd).astype(k_out.dtype)
        v_out[:, hd * MLA_V:(hd + 1) * MLA_V] = kvu[:, MLA_NOPE:].astype(v_out.dtype)


def _mla_prep_call(proj, wq_p, wkv, qln, kvln, qn_p, kn_p, rope_r, layer, tpb):
    t = proj.shape[0]
    tm = _largest_divisor(tpb, (768, 384, 256, 128))
    per_batch = tpb // tm
    hp = MLA_HEADS * MLA_HEAD_PAD
    rope_spec = pl.BlockSpec((tm, LANES), lambda i: (i % per_batch, 0))
    vec = lambda w: pl.BlockSpec((1, w), lambda i: (0, 0))
    return pl.pallas_call(
        _mla_prep_kernel,
        grid=(t // tm,),
        in_specs=[pl.BlockSpec((tm, MLA_Q_RANK), lambda i: (i, C_MQ // MLA_Q_RANK)),
                  pl.BlockSpec((tm, MLA_KV_RANK), lambda i: (i, C_MKV // MLA_KV_RANK)),
                  pl.BlockSpec((tm, LANES), lambda i: (i, C_MKR // LANES)),
                  pl.BlockSpec((None, MLA_Q_RANK, hp), lambda i: (layer, 0, 0)),
                  pl.BlockSpec((None, MLA_KV_RANK, hp), lambda i: (layer, 0, 0)),
                  vec(MLA_Q_RANK), vec(MLA_KV_RANK), vec(MLA_HEAD_PAD), vec(MLA_HEAD_PAD),
                  rope_spec, rope_spec],
        out_specs=[pl.BlockSpec((tm, hp), lambda i: (i, 0)),
                   pl.BlockSpec((tm, hp), lambda i: (i, 0)),
                   pl.BlockSpec((tm, MLA_HEADS * MLA_V), lambda i: (i, 0))],
        out_shape=[jax.ShapeDtypeStruct((t, hp), BF16), jax.ShapeDtypeStruct((t, hp), BF16),
                   jax.ShapeDtypeStruct((t, MLA_HEADS * MLA_V), BF16)],
        compiler_params=_params(1),
        name="mla_prep",
    )(proj, proj, proj, wq_p, wkv, qln, kvln, qn_p, kn_p, *rope_r)


LOG2E = 1.4426950408889634
KEY_CHUNK = 768
UNDERFLOW_GUARD = 2.0 ** -80


def _attention_kernel(*refs, mode, group, tq, seq, tile0):
    if mode == "win":
        sink_ref, q_ref, k_ref, v_ref, o_ref, vt_ref, kn_ref = refs
    else:
        q_ref, k_ref, v_ref, o_ref, vt_ref, kn_ref = refs
    tile = pl.program_id(2) + tile0
    hp, dv, tpb = vt_ref.shape
    dq = k_ref.shape[-1] // hp
    nq = group * tq
    heads = range(hp)
    contract_last = (((1,), (1,)), ((), ()))

    @pl.when(pl.program_id(2) == 0)
    def _():
        for hh in heads:
            vt_ref[hh] = v_ref[:, hh * dv:(hh + 1) * dv].astype(F32).T.astype(BF16)
            kf = k_ref[:, hh * dq:(hh + 1) * dq].astype(F32)
            norms = jnp.dot((kf * kf).astype(BF16), jnp.ones((dq, LANES), BF16), preferred_element_type=F32)
            kn_ref[hh] = jnp.broadcast_to(jnp.max(norms, axis=0, keepdims=True), kn_ref.shape[1:])

    q = [jnp.concatenate([q_ref[:, (hh * group + g) * dq:(hh * group + g + 1) * dq] for g in range(group)],
                         axis=0) for hh in heads]

    def sink_row(hh):
        first = (pl.program_id(1) * hp + hh) * group
        return jnp.concatenate(
            [jnp.full((1, tq), sink_ref[first + g] * LOG2E, F32) for g in range(group)], axis=1)

    def scores(hh, rows):
        return lax.dot_general(k_ref[rows, hh * dq:(hh + 1) * dq], q[hh], contract_last,
                               preferred_element_type=F32)

    def value_dot(hh, cols, p):
        return jnp.dot(vt_ref[hh, :, cols], p.astype(BF16), preferred_element_type=F32)

    def bound_init(hh):
        qf = q[hh].astype(F32)
        q_sq = lax.dot_general(jnp.ones((8, dq), BF16), (qf * qf).astype(BF16), contract_last,
                               preferred_element_type=F32)[0:1, :]
        shift = jnp.sqrt(q_sq * kn_ref[hh, 0:1, 0:1])
        if mode == "win":
            shift = jnp.maximum(shift, sink_row(hh))
            return shift, jnp.exp2(sink_row(hh) - shift), jnp.zeros((dv, nq), F32)
        return shift, jnp.zeros((1, nq), F32), jnp.zeros((dv, nq), F32)

    def bound_update(hh, carry, s, cols, mask=None):
        shift, l, acc = carry
        if mask is not None:
            s = jnp.where(mask(s.shape), s, NEG_INF)
        p = jnp.exp2(s - shift)
        return shift, l + jnp.sum(p, axis=0, keepdims=True), acc + value_dot(hh, cols, p)

    def max_init(hh):
        if mode == "win":
            return sink_row(hh), jnp.ones((1, nq), F32), jnp.zeros((dv, nq), F32)
        return jnp.full((1, nq), NEG_INF, F32), jnp.zeros((1, nq), F32), jnp.zeros((dv, nq), F32)

    def max_update(hh, carry, s, cols, mask=None):
        m, l, acc = carry
        if mask is not None:
            s = jnp.where(mask(s.shape), s, NEG_INF)
        m_new = jnp.maximum(m, jnp.max(s, axis=0, keepdims=True))
        alpha = jnp.exp2(m - m_new)
        p = jnp.exp2(s - m_new)
        return m_new, alpha * l + jnp.sum(p, axis=0, keepdims=True), alpha * acc + value_dot(hh, cols, p)

    def sweep(chunks, init, update):
        carry = [init(hh) for hh in heads]
        s = [scores(hh, chunks[0][0]) for hh in heads]
        for c, (keys, mask) in enumerate(chunks):
            s_next = [scores(hh, chunks[c + 1][0]) for hh in heads] if c + 1 < len(chunks) else None
            carry = [update(hh, carry[hh], s[hh], keys, mask) for hh in heads]
            s = s_next
        return carry

    def write(carry):
        for hh in heads:
            _, l, acc = carry[hh]
            out = acc / l
            for g in range(group):
                col = (hh * group + g) * dv
                o_ref[:, col:col + dv] = out[:, g * tq:(g + 1) * tq].T.astype(o_ref.dtype)

    def attend(chunks):
        fast = sweep(chunks, bound_init, bound_update)
        smallest = functools.reduce(jnp.minimum, [jnp.min(carry[1]) for carry in fast])
        safe = smallest > UNDERFLOW_GUARD

        @pl.when(safe)
        def _():
            write(fast)

        @pl.when(jnp.logical_not(safe))
        def _():
            write(sweep(chunks, max_init, max_update))

    ctx_keys = slice(0, CTX_LEN)

    @pl.when(tile < CTX_LEN // tq)
    def _():
        attend([(ctx_keys, None)])

    @pl.when(tile >= CTX_LEN // tq)
    def _():
        if mode == "win":
            blk = tile - CTX_LEN // tq
            start = jnp.clip((blk - 1) * BLOCK, 0, seq - 3 * BLOCK)
            local = pl.ds(pl.multiple_of(CTX_LEN + start, BLOCK), 3 * BLOCK)

            def band(shape):
                k_pos = start + lax.broadcasted_iota(jnp.int32, shape, 0)
                q_pos = blk * BLOCK + lax.broadcasted_iota(jnp.int32, shape, 1) % tq
                return jnp.abs(q_pos - k_pos) <= WINDOW

            attend([(ctx_keys, None), (local, band)])
        else:
            attend([(slice(c * KEY_CHUNK, (c + 1) * KEY_CHUNK), None) for c in range(tpb // KEY_CHUNK)])


def _attention_call(mode, q_src, k_src, v_src, extras, n_batch, tpb, skip_ctx):
    t = q_src[0].shape[0]
    if mode == "mla":
        n_kv, group, dq, tq, hp = MLA_HEADS, 1, MLA_HEAD_PAD, 256, 4
    else:
        n_kv, group, dq, tq, hp = WIN_KV_HEADS, WIN_HEADS // WIN_KV_HEADS, HEAD_DIM, BLOCK, 2
    dv = HEAD_DIM
    tiles = tpb // tq
    tile0 = CTX_LEN // tq if skip_ctx else 0
    qw, kw, vw = hp * group * dq, hp * dq, hp * dv
    assert q_src[1] % qw == 0 and k_src[1] % kw == 0 and v_src[1] % vw == 0 and n_kv % hp == 0
    q_blk, k_blk, v_blk = q_src[1] // qw, k_src[1] // kw, v_src[1] // vw
    q_spec = pl.BlockSpec((tq, qw), lambda b, h, i: (b * tiles + i + tile0, q_blk + h))
    k_spec = pl.BlockSpec((tpb, kw), lambda b, h, i: (b, k_blk + h))
    v_spec = pl.BlockSpec((tpb, vw), lambda b, h, i: (b, v_blk + h))
    in_specs, args = [], []
    if mode == "win":
        in_specs.append(pl.BlockSpec(memory_space=pltpu.SMEM))
        args.append(extras["sink"])
    in_specs += [q_spec, k_spec, v_spec]
    args += [q_src[0], k_src[0], v_src[0]]
    kern = functools.partial(_attention_kernel, mode=mode, group=group, tq=tq, seq=tpb - CTX_LEN,
                             tile0=tile0)
    return pl.pallas_call(
        kern,
        grid=(n_batch, n_kv // hp, tiles - tile0),
        in_specs=in_specs,
        out_specs=pl.BlockSpec((tq, hp * group * dv), lambda b, h, i: (b * tiles + i + tile0, h)),
        out_shape=jax.ShapeDtypeStruct((t, n_kv * group * dv), BF16),
        scratch_shapes=[pltpu.VMEM((hp, dv, tpb), BF16), pltpu.VMEM((hp, 8, LANES), F32)],
        compiler_params=_params(3),
        name=f"attention_{mode}",
    )(*args)


def _merge_kernel(ow_ref, om_ref, og_ref, gw_ref, gm_ref, gg_ref, wb_ref, wo_ref, x_ref, gate_ref,
                  o_ref, *, tiles_per_batch):
    i, j = pl.program_id(0), pl.program_id(1)
    tm = x_ref.shape[0]

    @pl.when(j == 0)
    def _():
        o_ref[...] = jnp.zeros_like(o_ref)

    y = None
    for br, (o_br, g_br) in enumerate(((ow_ref, gw_ref), (om_ref, gm_ref), (og_ref, gg_ref))):
        term = jax.nn.sigmoid(g_br[...].astype(F32)) * jnp.dot(o_br[...], wb_ref[br],
                                                                preferred_element_type=F32)
        y = term if y is None else y + term
    o_ref[...] += jnp.dot(y.astype(BF16), wo_ref[...], preferred_element_type=F32)

    @pl.when(j == pl.num_programs(1) - 1)
    def _():
        gate = _gate_rows(gate_ref, i // tiles_per_batch, (i % tiles_per_batch) * tm, tm)
        o_ref[...] = x_ref[...] + gate * o_ref[...]


def _merge_call(xs, o_win, o_mla, o_glb, proj, mod, w_branch, w_out, layer, tpb):
    t, d = xs.shape
    tm = _largest_divisor(tpb, (768, 384, 256, 128))
    tn = _largest_divisor(d, (512, 256, 128))
    o_spec = pl.BlockSpec((tm, BRANCH_W), lambda i, j: (i, 0))
    gate_spec = lambda br: pl.BlockSpec((tm, tn), lambda i, j: (i, (C_GATE + br * d) // tn + j))
    kern = functools.partial(_merge_kernel, tiles_per_batch=tpb // tm)
    return pl.pallas_call(
        kern,
        grid=(t // tm, d // tn),
        in_specs=[o_spec, o_spec, o_spec, gate_spec(0), gate_spec(1), gate_spec(2),
                  pl.BlockSpec((None, N_BRANCH, BRANCH_W, tn), lambda i, j: (layer, 0, 0, j)),
                  pl.BlockSpec((None, tn, d), lambda i, j: (layer, j, 0)),
                  pl.BlockSpec((tm, d), lambda i, j: (i, 0)),
                  pl.BlockSpec((None, MOD_ROWS, d), lambda i, j: (layer, 0, 2))],
        out_specs=pl.BlockSpec((tm, d), lambda i, j: (i, 0)),
        out_shape=jax.ShapeDtypeStruct((t, d), F32),
        input_output_aliases={8: 0},
        compiler_params=_params(2),
        name="merge_out_projection",
    )(o_win, o_mla, o_glb, proj, proj, proj, w_branch, w_out, xs, mod)


def _ffn_kernel(x_ref, sh_ref, sc_ref, gate_ref, wg_ref, wu_ref, wd_ref, o_ref, h_ref, *, tiles_per_batch):
    i, f = pl.program_id(0), pl.program_id(1)
    tm = x_ref.shape[0]
    batch, row0 = i // tiles_per_batch, (i % tiles_per_batch) * tm

    @pl.when(f == 0)
    def _():
        _norm_modulate(x_ref, sh_ref, sc_ref, h_ref, batch, row0, tm)
        o_ref[...] = jnp.zeros_like(o_ref)

    h = h_ref[...]
    g = jnp.dot(h, wg_ref[...], preferred_element_type=F32)
    u = jnp.dot(h, wu_ref[...], preferred_element_type=F32)
    act = (g * jax.nn.sigmoid(g) * u).astype(BF16)
    o_ref[...] += jnp.dot(act, wd_ref[...], preferred_element_type=F32)

    @pl.when(f == pl.num_programs(1) - 1)
    def _():
        o_ref[...] = x_ref[...] + _gate_rows(gate_ref, batch, row0, tm) * o_ref[...]


def _ffn_call(xs, mod, w_gate_up, w_down, idx, layer, tpb):
    t, d = xs.shape
    ff = w_down.shape[1]
    tm = _largest_divisor(tpb, (768, 384, 256, 128))
    tf = _largest_divisor(ff, (512, 256, 128))
    n_f = ff // tf
    mod_spec = lambda k: pl.BlockSpec((None, MOD_ROWS, d), lambda i, f: (layer, 0, k))
    kern = functools.partial(_ffn_kernel, tiles_per_batch=tpb // tm)
    return pl.pallas_call(
        kern,
        grid=(t // tm, n_f),
        in_specs=[pl.BlockSpec((tm, d), lambda i, f: (i, 0)), mod_spec(3), mod_spec(4), mod_spec(5),
                  pl.BlockSpec((None, d, tf), lambda i, f: (idx, 0, f)),
                  pl.BlockSpec((None, d, tf), lambda i, f: (idx, 0, f + n_f)),
                  pl.BlockSpec((None, tf, d), lambda i, f: (idx, f, 0))],
        out_specs=pl.BlockSpec((tm, d), lambda i, f: (i, 0)),
        out_shape=jax.ShapeDtypeStruct((t, d), F32),
        scratch_shapes=[pltpu.VMEM((tm, d), BF16)],
        input_output_aliases={0: 0},
        compiler_params=_params(2),
        name="swiglu_ffn",
    )(xs, mod, mod, mod, w_gate_up, w_gate_up, w_down)


MOE_TOKEN_TILE = 256


def _moe_tile_map(step, tiles_per_batch, skip):
    per_batch = tiles_per_batch - skip
    batch, inner = step // per_batch, step % per_batch + skip
    return batch * tiles_per_batch + inner, batch, inner * MOE_TOKEN_TILE


def _router_kernel(x_ref, sh_ref, sc_ref, rhi_ref, rlo_ref, h_ref, o_ref, *, tiles_per_batch, skip):
    tm = x_ref.shape[0]
    _, batch, row0 = _moe_tile_map(pl.program_id(0), tiles_per_batch, skip)
    _norm_modulate(x_ref, sh_ref, sc_ref, h_ref, batch, row0, tm)
    h = h_ref[...]
    h_hi = h.astype(BF16)
    h_lo = (h - h_hi.astype(F32)).astype(BF16)
    logits = (jnp.dot(h_hi, rhi_ref[...], preferred_element_type=F32)
              + jnp.dot(h_hi, rlo_ref[...], preferred_element_type=F32)
              + jnp.dot(h_lo, rhi_ref[...], preferred_element_type=F32))
    lane = lax.broadcasted_iota(jnp.int32, logits.shape, 1).astype(F32)
    logits = jnp.where(lane < N_EXPERTS, logits, -jnp.inf)
    top1 = jnp.max(logits, axis=-1, keepdims=True)
    idx1 = jnp.min(jnp.where(logits == top1, lane, float(LANES)), axis=-1, keepdims=True)
    rest = jnp.where(lane == idx1, -jnp.inf, logits)
    top2 = jnp.max(rest, axis=-1, keepdims=True)
    idx2 = jnp.min(jnp.where(rest == top2, lane, float(LANES)), axis=-1, keepdims=True)
    e2 = jnp.exp(top2 - top1)
    den = 1.0 + e2
    o_ref[...] = jnp.where(lane == 0, 1.0 / den, jnp.where(lane == 1, e2 / den, jnp.where(
        lane == 2, idx1, jnp.where(lane == 3, idx2, 0.0))))


def _router_call(xs, mod, router, layer, tpb, skip):
    t, d = xs.shape
    tm = MOE_TOKEN_TILE
    tiles_per_batch = tpb // tm
    n_tok = t // tiles_per_batch * (tiles_per_batch - skip)
    r_hi = jnp.pad(router, ((0, 0), (0, LANES - N_EXPERTS))).astype(BF16)
    r_lo = jnp.pad(router - r_hi[:, :N_EXPERTS].astype(F32), ((0, 0), (0, LANES - N_EXPERTS))).astype(BF16)
    mod_spec = lambda k: pl.BlockSpec((None, MOD_ROWS, d), lambda i: (layer, 0, k))
    r_spec = pl.BlockSpec((d, LANES), lambda i: (0, 0))
    kern = functools.partial(_router_kernel, tiles_per_batch=tiles_per_batch, skip=skip)
    return pl.pallas_call(
        kern,
        grid=(n_tok // tm,),
        in_specs=[pl.BlockSpec((tm, d), lambda i: (_moe_tile_map(i, tiles_per_batch, skip)[0], 0)),
                  mod_spec(3), mod_spec(4), r_spec, r_spec],
        out_specs=[pl.BlockSpec((tm, d), lambda i: (i, 0)), pl.BlockSpec((tm, LANES), lambda i: (i, 0))],
        out_shape=[jax.ShapeDtypeStruct((n_tok, d), F32), jax.ShapeDtypeStruct((n_tok, LANES), F32)],
        compiler_params=_params(1),
        name="moe_router",
    )(xs, mod, mod, r_hi, r_lo)


def _moe_plan(route, tile):
    n_pairs = 2 * route.shape[0]
    n_tiles = -(-n_pairs // tile) + N_EXPERTS
    expert = route[:, 2:4].astype(jnp.int32).reshape(-1)
    onehot = (expert[:, None] == jnp.arange(N_EXPERTS, dtype=jnp.int32)[None]).astype(jnp.int32)
    csum = jnp.cumsum(onehot, axis=0)
    count = csum[-1]
    padded = (count + tile - 1) // tile * tile
    group_end = jnp.cumsum(padded)
    group_start = group_end - padded
    dest = jnp.sum(onehot * (group_start[None] + csum - 1), axis=1).astype(jnp.int32)
    tile_row = jnp.arange(n_tiles, dtype=jnp.int32) * tile
    tile_expert = jnp.minimum(jnp.sum((tile_row[:, None] >= group_end[None]).astype(jnp.int32), axis=1),
                              N_EXPERTS - 1).astype(jnp.int32)
    n_active = (group_end[-1] // tile).astype(jnp.int32).reshape(1)
    pad_start = jnp.concatenate([group_start + count, group_end[-1:]]).astype(jnp.int32)
    pad_end = jnp.concatenate([group_end, jnp.array([n_tiles * tile], jnp.int32)]).astype(jnp.int32)
    return dest, tile_expert, n_active, pad_start, pad_end, n_tiles


DISPATCH_TILE = 512
DMA_UNROLL = 8


def _dispatch_kernel(pad_lo_ref, pad_hi_ref, dest_ref, h_ref, rows_hbm, zero_ref, sem):
    tm = h_ref.shape[0]

    def row_copy(src_row, dst):
        return pltpu.make_async_copy(src_row, rows_hbm.at[pl.ds(dst, 1), :], sem.at[0])

    @pl.when(pl.program_id(0) == 0)
    def _():
        zero_ref[...] = jnp.zeros_like(zero_ref)
        zero_row = zero_ref.at[pl.ds(0, 1), :]
        for e in range(N_EXPERTS + 1):
            lo, hi = pad_lo_ref[e], pad_hi_ref[e]

            def start_zero(r, carry):
                row_copy(zero_row, r).start()
                return carry

            def wait_zero(r, carry):
                row_copy(zero_row, r).wait()
                return carry

            lax.fori_loop(lo, hi, start_zero, 0)
            lax.fori_loop(lo, hi, wait_zero, 0)

    def start(r, carry):
        for k in range(2):
            row_copy(h_ref.at[pl.ds(r, 1), :], dest_ref[0, 2 * r + k]).start()
        return carry

    def wait(r, carry):
        for k in range(2):
            row_copy(h_ref.at[pl.ds(r, 1), :], dest_ref[0, 2 * r + k]).wait()
        return carry

    lax.fori_loop(0, tm, start, 0, unroll=DMA_UNROLL)
    lax.fori_loop(0, tm, wait, 0, unroll=DMA_UNROLL)


def _dispatch_call(h, dest, pad_lo, pad_hi, n_rows):
    t, d = h.shape
    tm = DISPATCH_TILE
    grid_spec = pltpu.PrefetchScalarGridSpec(
        num_scalar_prefetch=2,
        grid=(t // tm,),
        in_specs=[pl.BlockSpec((None, 1, 2 * tm), lambda i, lo, hi: (i, 0, 0), memory_space=pltpu.SMEM),
                  pl.BlockSpec((tm, d), lambda i, lo, hi: (i, 0))],
        out_specs=pl.BlockSpec(memory_space=pl.ANY),
        scratch_shapes=[pltpu.VMEM((8, d), F32), pltpu.SemaphoreType.DMA((1,))],
    )
    return pl.pallas_call(
        _dispatch_kernel,
        grid_spec=grid_spec,
        out_shape=jax.ShapeDtypeStruct((n_rows, d), F32),
        compiler_params=_params(1),
        name="moe_dispatch",
    )(pad_lo, pad_hi, dest.reshape(t // tm, 1, 2 * tm), h)


def _moe_group_kernel(expert_ref, n_active_ref, x_ref, wg_ref, wu_ref, wd_ref, o_ref, h_ref):
    m, f = pl.program_id(0), pl.program_id(1)

    @pl.when(f == 0)
    def _():
        h_ref[...] = x_ref[...].astype(BF16)
        o_ref[...] = jnp.zeros_like(o_ref)

    @pl.when(m < n_active_ref[0])
    def _():
        h = h_ref[...]
        g = jnp.dot(h, wg_ref[...], preferred_element_type=F32)
        u = jnp.dot(h, wu_ref[...], preferred_element_type=F32)
        act = (g * jax.nn.sigmoid(g) * u).astype(BF16)
        o_ref[...] += jnp.dot(act, wd_ref[...], preferred_element_type=F32)


def _moe_group_call(rows, tile_expert, n_active, w_gate_up, w_down, idx, tile):
    n_rows, d = rows.shape
    ff = w_down.shape[2]
    tf = _largest_divisor(ff, (512, 256, 128))
    n_f = ff // tf

    def f_eff(m, f, na):
        return jnp.where(m < na[0], f, n_f - 1)

    grid_spec = pltpu.PrefetchScalarGridSpec(
        num_scalar_prefetch=2,
        grid=(n_rows // tile, n_f),
        in_specs=[pl.BlockSpec((tile, d), lambda m, f, te, na: (m, 0)),
                  pl.BlockSpec((None, None, d, tf), lambda m, f, te, na: (idx, te[m], 0, f_eff(m, f, na))),
                  pl.BlockSpec((None, None, d, tf),
                               lambda m, f, te, na: (idx, te[m], 0, f_eff(m, f, na) + n_f)),
                  pl.BlockSpec((None, None, tf, d), lambda m, f, te, na: (idx, te[m], f_eff(m, f, na), 0))],
        out_specs=pl.BlockSpec((tile, d), lambda m, f, te, na: (m, 0)),
        scratch_shapes=[pltpu.VMEM((tile, d), BF16)],
    )
    return pl.pallas_call(
        _moe_group_kernel,
        grid_spec=grid_spec,
        out_shape=jax.ShapeDtypeStruct((n_rows, d), F32),
        compiler_params=_params(2),
        name="moe_experts",
    )(tile_expert, n_active, rows, w_gate_up, w_gate_up, w_down)


def _combine_kernel(dest_ref, next_ref, x_ref, route_ref, gate_ref, y_hbm, o_ref, buf, sem, *,
                    tiles_per_batch, skip):
    i, n = pl.program_id(0), pl.num_programs(0)
    tm = x_ref.shape[0]
    slot = i % 2

    def row_copy(dests, to, r, k):
        return pltpu.make_async_copy(y_hbm.at[pl.ds(dests[0, 2 * r + k], 1), :],
                                     buf.at[to, k, pl.ds(r, 1), :], sem.at[to, k])

    def gather(dests, to, wait):
        def body(r, carry):
            for k in range(2):
                copy = row_copy(dests, to, r, k)
                copy.wait() if wait else copy.start()
            return carry
        lax.fori_loop(0, tm, body, 0, unroll=DMA_UNROLL)

    @pl.when(i == 0)
    def _():
        gather(dest_ref, slot, wait=False)

    @pl.when(i + 1 < n)
    def _():
        gather(next_ref, 1 - slot, wait=False)

    gather(dest_ref, slot, wait=True)
    route = route_ref[...]
    mix = route[:, 0:1] * buf[slot, 0] + route[:, 1:2] * buf[slot, 1]
    _, batch, row0 = _moe_tile_map(i, tiles_per_batch, skip)
    o_ref[...] = x_ref[...] + _gate_rows(gate_ref, batch, row0, tm) * mix


def _combine_call(xs, route, mod, y, dest, layer, tpb, skip):
    t, d = xs.shape
    tm = MOE_TOKEN_TILE
    tiles_per_batch = tpb // tm
    n_tok = route.shape[0]
    n_steps = n_tok // tm
    dest = dest.reshape(n_steps, 1, 2 * tm)
    kern = functools.partial(_combine_kernel, tiles_per_batch=tiles_per_batch, skip=skip)
    smem_spec = lambda index: pl.BlockSpec((None, 1, 2 * tm), index, memory_space=pltpu.SMEM)
    return pl.pallas_call(
        kern,
        grid=(n_steps,),
        in_specs=[smem_spec(lambda i: (i, 0, 0)),
                  smem_spec(lambda i: (jnp.minimum(i + 1, n_steps - 1), 0, 0)),
                  pl.BlockSpec((tm, d), lambda i: (_moe_tile_map(i, tiles_per_batch, skip)[0], 0)),
                  pl.BlockSpec((tm, LANES), lambda i: (i, 0)),
                  pl.BlockSpec((None, MOD_ROWS, d), lambda i: (layer, 0, 5)),
                  pl.BlockSpec(memory_space=pl.ANY)],
        out_specs=pl.BlockSpec((tm, d), lambda i: (i, 0)),
        out_shape=jax.ShapeDtypeStruct((n_tok, d), F32),
        scratch_shapes=[pltpu.VMEM((2, 2, tm, d), F32), pltpu.SemaphoreType.DMA((2, 2))],
        input_output_aliases={} if skip else {2: 0},
        compiler_params=_params(1),
        name="moe_combine",
    )(dest, dest, xs, route, mod, y)


def _rope_tables(seq, dim):
    rows = seq // GRID_W
    row = jnp.repeat(jnp.arange(rows), GRID_W).astype(F32)
    col = jnp.tile(jnp.arange(GRID_W), rows).astype(F32)
    quarter = dim // 4
    freqs = ROPE_THETA ** (-jnp.arange(quarter, dtype=F32) / quarter)
    ang_r = row[:, None] * freqs
    ang_c = col[:, None] * freqs
    fill = LANES // 2 - 2 * quarter
    cos_half = [jnp.cos(ang_r), jnp.cos(ang_c), jnp.ones((seq, fill), F32)]
    sin_half = [jnp.sin(ang_r), jnp.sin(ang_c), jnp.zeros((seq, fill), F32)]
    cos = jnp.concatenate(cos_half + cos_half, axis=-1)
    sin = jnp.concatenate([-t for t in sin_half] + sin_half, axis=-1)
    ctx_rows = lambda value: jnp.full((CTX_LEN, LANES), value, F32)
    return jnp.concatenate([ctx_rows(1.0), cos], axis=0), jnp.concatenate([ctx_rows(0.0), sin], axis=0)


def _pair_layout(v):
    quarter = v.shape[-1] // 4
    a, b, c, d = (v[..., k * quarter:(k + 1) * quarter] for k in range(4))
    zeros = jnp.zeros(v.shape[:-1] + (LANES // 2 - 2 * quarter,), v.dtype)
    return jnp.concatenate([a, c, zeros, b, d, zeros], axis=-1)


def _pair_layout_heads(w, n_heads):
    w = w.reshape(w.shape[:-1] + (n_heads, HEAD_DIM))
    return _pair_layout(w).reshape(w.shape[:-2] + (n_heads * HEAD_DIM,))


def _layout_w_in(w_in):
    sizes = (WIN_HEADS * HEAD_DIM, WIN_KV_HEADS * HEAD_DIM, WIN_KV_HEADS * HEAD_DIM,
             MLA_Q_RANK, MLA_KV_RANK, MLA_ROPE,
             GLB_HEADS * HEAD_DIM, GLB_KV_HEADS * HEAD_DIM, GLB_KV_HEADS * HEAD_DIM)
    offs = np.concatenate([[0], np.cumsum(sizes)])
    wq, wk, wv, mq, mkv, mkr, gq, gk, gv = (w_in[..., int(offs[k]):int(offs[k + 1])] for k in range(9))
    gates = w_in[..., int(offs[9]):]
    wq, gq = _pair_layout_heads(wq, WIN_HEADS), _pair_layout_heads(gq, GLB_HEADS)
    wk, gk = _pair_layout_heads(wk, WIN_KV_HEADS), _pair_layout_heads(gk, GLB_KV_HEADS)
    pad = jnp.zeros(w_in.shape[:-1] + (C_GATE - C_MKR - LANES,), w_in.dtype)
    return jnp.concatenate([wq, gq, wk, wv, gk, gv, mq, mkv, _pair_layout(mkr), pad, gates],
                           axis=-1).astype(BF16)


def _mla_head_layout(w):
    w = w.reshape(w.shape[:-1] + (MLA_HEADS, MLA_NOPE + MLA_ROPE))
    w = jnp.concatenate([w[..., :MLA_NOPE], _pair_layout(w[..., MLA_NOPE:])], axis=-1)
    return w.reshape(w.shape[:-2] + (MLA_HEADS * MLA_HEAD_PAD,))


def kernel(x, c, ctx, c_ctx, w_mod, b_mod, w_in, win_q_norm, win_k_norm, win_sink, mla_q_lat_norm,
           mla_kv_lat_norm, mla_w_q_up, mla_w_kv_up, mla_q_norm, mla_k_norm, glb_q_norm, glb_k_norm,
           w_branch, w_out, ffn_w_gate_up, ffn_w_down, moe_router, moe_w_gate_up, moe_w_down):
    n_batch, seq, d = x.shape
    depth = w_mod.shape[0]
    tpb = CTX_LEN + seq
    assert ctx.shape[1] == CTX_LEN and seq % GRID_W == 0 and seq % BLOCK == 0 and seq >= 3 * BLOCK
    assert n_batch <= MOD_ROWS - 8 and C_MKR + LANES <= C_GATE

    xs = jnp.concatenate([ctx, x], axis=1).reshape(n_batch * tpb, d)
    cvec = jnp.zeros((MOD_ROWS, d), F32).at[:n_batch].set(c).at[MOD_ROWS - 8].set(c_ctx)
    mod = _mod_call(cvec, w_mod, b_mod)

    rope_h = _rope_tables(seq, HEAD_DIM)
    rope_r = _rope_tables(seq, MLA_ROPE)
    w_in_p = _layout_w_in(w_in)
    wq_up = _mla_head_layout(mla_w_q_up).astype(BF16)
    wkv_up = mla_w_kv_up.astype(BF16)
    mla_gain = lambda g: jnp.concatenate([g[:MLA_NOPE], _pair_layout(g[MLA_NOPE:])])[None]
    gqa_gain = lambda g: _pair_layout(g)[None]
    w_branch_b, w_out_b = w_branch.astype(BF16), w_out.astype(BF16)
    ffn_gu, ffn_dn = ffn_w_gate_up.astype(BF16), ffn_w_down.astype(BF16)
    moe_gu, moe_dn = moe_w_gate_up.astype(BF16), moe_w_down.astype(BF16)

    latent_only = False
    for layer in range(depth):
        proj = _inproj_call(xs, mod, w_in_p, layer, tpb)
        q_win, k_win, q_glb, k_glb = _gqa_prep_call(
            proj, [gqa_gain(g[layer]) for g in (win_q_norm, win_k_norm, glb_q_norm, glb_k_norm)], rope_h, tpb)
        q_mla, k_mla, v_mla = _mla_prep_call(
            proj, wq_up, wkv_up, mla_q_lat_norm[layer][None], mla_kv_lat_norm[layer][None],
            mla_gain(mla_q_norm[layer]), mla_gain(mla_k_norm[layer]), rope_r, layer, tpb)
        o_win = _attention_call("win", (q_win, 0), (k_win, 0), (proj, C_WV), dict(sink=win_sink[layer]),
                                n_batch, tpb, False)
        o_mla = _attention_call("mla", (q_mla, 0), (k_mla, 0), (v_mla, 0), {}, n_batch, tpb, False)
        o_glb = _attention_call("glb", (q_glb, 0), (k_glb, 0), (proj, C_GV), {}, n_batch, tpb, False)
        xs = _merge_call(xs, o_win, o_mla, o_glb, proj, mod, w_branch_b, w_out_b, layer, tpb)
        if layer % 2 == 0:
            xs = _ffn_call(xs, mod, ffn_gu, ffn_dn, layer // 2, layer, tpb)
        else:
            skip = 1 if layer == depth - 1 else 0
            h, route = _router_call(xs, mod, moe_router[layer // 2], layer, tpb, skip)
            tile = 1024 if 2 * h.shape[0] >= 16 * 1024 else 128
            dest, tile_expert, n_active, pad_lo, pad_hi, n_tiles = _moe_plan(route, tile)
            rows = _dispatch_call(h, dest, pad_lo, pad_hi, n_tiles * tile)
            y = _moe_group_call(rows, tile_expert, n_active, moe_gu, moe_dn, layer // 2, tile)
            xs = _combine_call(xs, route, mod, y, dest, layer, tpb, skip)
            latent_only = bool(skip)
    if latent_only:
        return xs.reshape(n_batch, seq, d)
    return xs.reshape(n_batch, tpb, d)[:, CTX_LEN:, :]
```

```python
import functools

import jax
import jax.numpy as jnp
import numpy as np
from jax import lax
from jax.experimental import pallas as pl
from jax.experimental.pallas import tpu as pltpu

CTX_LEN = 256
GRID_W = 64
HEAD_DIM = 128
WIN_HEADS = 8
WIN_KV_HEADS = 2
WINDOW = 128
BLOCK = 128
MLA_HEADS = 8
MLA_NOPE = 128
MLA_ROPE = 64
MLA_V = 128
MLA_Q_RANK = 512
MLA_KV_RANK = 256
GLB_HEADS = 8
GLB_KV_HEADS = 2
BRANCH_W = 1024
N_BRANCH = 3
N_EXPERTS = 8
ROPE_THETA = 10000.0
EPS = 1e-6
NEG_INF = -1e30

LANES = 128
MLA_HEAD_PAD = 256
MOD_ROWS = 24
VMEM_LIMIT = 60 * 1024 * 1024

C_WQ = 0
C_GQ = C_WQ + WIN_HEADS * HEAD_DIM
C_WK = C_GQ + GLB_HEADS * HEAD_DIM
C_WV = C_WK + WIN_KV_HEADS * HEAD_DIM
C_GK = C_WV + WIN_KV_HEADS * HEAD_DIM
C_GV = C_GK + GLB_KV_HEADS * HEAD_DIM
C_MQ = C_GV + GLB_KV_HEADS * HEAD_DIM
C_MKV = C_MQ + MLA_Q_RANK
C_MKR = C_MKV + MLA_KV_RANK
C_GATE = 4096

BF16 = jnp.bfloat16
F32 = jnp.float32


def _largest_divisor(n, candidates):
    for cand in candidates:
        if n % cand == 0:
            return cand
    raise ValueError(f"no tile in {candidates} divides {n}")


def _params(n_axes):
    return pltpu.CompilerParams(dimension_semantics=("arbitrary",) * n_axes,
                                vmem_limit_bytes=VMEM_LIMIT)


def _rms(x, width):
    ms = jnp.sum(x * x, axis=-1, keepdims=True) * (1.0 / width)
    return x * lax.rsqrt(ms + EPS)


def _rope(y, cos, sin):
    return y * cos + pltpu.roll(y, LANES // 2, 1) * sin


def _norm_modulate(x_ref, sh_ref, sc_ref, h_ref, batch, row0, n_rows):
    block = 128
    assert n_rows % block == 0 and CTX_LEN % block == 0
    pick = lambda ref, row: ref[pl.ds(row, 1), :]
    sh_lat, sc_lat = pick(sh_ref, batch), 1.0 + pick(sc_ref, batch)
    sh_ctx, sc_ctx = pick(sh_ref, MOD_ROWS - 8), 1.0 + pick(sc_ref, MOD_ROWS - 8)
    for blk in range(n_rows // block):
        rows = slice(blk * block, (blk + 1) * block)
        is_ctx = row0 + blk * block < CTX_LEN
        sh = jnp.where(is_ctx, sh_ctx, sh_lat)
        sc = jnp.where(is_ctx, sc_ctx, sc_lat)
        xf = x_ref[rows, :]
        h_ref[rows, :] = (_rms(xf, xf.shape[-1]) * sc + sh).astype(h_ref.dtype)


def _gate_rows(g_ref, batch, row0, n_rows):
    rows = lax.broadcasted_iota(jnp.int32, (n_rows, 1), 0) + row0
    g_lat = g_ref[pl.ds(batch, 1), :]
    g_ctx = g_ref[pl.ds(MOD_ROWS - 8, 1), :]
    return jnp.where(rows < CTX_LEN, g_ctx, g_lat)


def _mod_kernel(c_ref, w_ref, b_ref, o_ref):
    cv = c_ref[...]
    sc = (cv * jax.nn.sigmoid(cv)).astype(BF16)
    o_ref[...] = jnp.dot(sc, w_ref[...].astype(BF16), preferred_element_type=F32) + b_ref[...]


def _mod_call(cvec, w_mod, b_mod):
    depth, d, n = w_mod.shape
    tn = _largest_divisor(n, (1024, 512, 256, 128))
    return pl.pallas_call(
        _mod_kernel,
        grid=(depth, n // tn),
        in_specs=[pl.BlockSpec((MOD_ROWS, d), lambda l, j: (0, 0)),
                  pl.BlockSpec((None, d, tn), lambda l, j: (l, 0, j)),
                  pl.BlockSpec((None, 1, tn), lambda l, j: (l, 0, j))],
        out_specs=pl.BlockSpec((None, MOD_ROWS, tn), lambda l, j: (l, 0, j)),
        out_shape=jax.ShapeDtypeStruct((depth, MOD_ROWS, n), F32),
        compiler_params=_params(2),
        name="mod_vectors",
    )(cvec, w_mod, b_mod.reshape(depth, 1, n))


def _inproj_kernel(x_ref, sh_ref, sc_ref, w_ref, o_ref, h_ref, *, tiles_per_batch):
    i = pl.program_id(0)
    tm = x_ref.shape[0]

    @pl.when(pl.program_id(1) == 0)
    def _():
        _norm_modulate(x_ref, sh_ref, sc_ref, h_ref, i // tiles_per_batch,
                       (i % tiles_per_batch) * tm, tm)

    o_ref[...] = jnp.dot(h_ref[...], w_ref[...], preferred_element_type=F32).astype(o_ref.dtype)


def _inproj_call(xs, mod, w_in_p, layer, tpb):
    t, d = xs.shape
    n = w_in_p.shape[-1]
    tm = _largest_divisor(tpb, (768, 384, 256, 128))
    tn = _largest_divisor(n, (2048, 1024, 512, 256, 128))
    kern = functools.partial(_inproj_kernel, tiles_per_batch=tpb // tm)
    return pl.pallas_call(
        kern,
        grid=(t // tm, n // tn),
        in_specs=[pl.BlockSpec((tm, d), lambda i, j: (i, 0)),
                  pl.BlockSpec((None, MOD_ROWS, d), lambda i, j: (layer, 0, 0)),
                  pl.BlockSpec((None, MOD_ROWS, d), lambda i, j: (layer, 0, 1)),
                  pl.BlockSpec((None, d, tn), lambda i, j: (layer, 0, j))],
        out_specs=pl.BlockSpec((tm, tn), lambda i, j: (i, j)),
        out_shape=jax.ShapeDtypeStruct((t, n), BF16),
        scratch_shapes=[pltpu.VMEM((tm, d), BF16)],
        compiler_params=_params(2),
        name="in_projection",
    )(xs, mod, mod, w_in_p)


def _gqa_prep_kernel(wq_ref, wk_ref, gq_ref, gk_ref, wqg_ref, wkg_ref, gqg_ref, gkg_ref,
                     cos_ref, sin_ref, qa_ref, ka_ref, qc_ref, kc_ref):
    cos, sin = cos_ref[...], sin_ref[...]
    q_scale = HEAD_DIM ** -0.5 * LOG2E
    for src, gain, dst, scale in ((wq_ref, wqg_ref, qa_ref, q_scale), (wk_ref, wkg_ref, ka_ref, None),
                                  (gq_ref, gqg_ref, qc_ref, q_scale), (gk_ref, gkg_ref, kc_ref, None)):
        g = gain[...] if scale is None else gain[...] * scale
        for hd in range(src.shape[1] // HEAD_DIM):
            cols = slice(hd * HEAD_DIM, (hd + 1) * HEAD_DIM)
            y = _rms(src[:, cols].astype(F32), HEAD_DIM) * g
            dst[:, cols] = _rope(y, cos, sin).astype(dst.dtype)


def _gqa_prep_call(proj, gains, rope_h, tpb):
    t = proj.shape[0]
    tm = _largest_divisor(tpb, (384, 256, 128))
    per_batch = tpb // tm
    qw, kw = WIN_HEADS * HEAD_DIM, WIN_KV_HEADS * HEAD_DIM
    row_spec = lambda width, col: pl.BlockSpec((tm, width), lambda i: (i, col // width))
    gain_spec = pl.BlockSpec((1, HEAD_DIM), lambda i: (0, 0))
    rope_spec = pl.BlockSpec((tm, LANES), lambda i: (i % per_batch, 0))
    q_out, k_out = jax.ShapeDtypeStruct((t, qw), BF16), jax.ShapeDtypeStruct((t, kw), BF16)
    return pl.pallas_call(
        _gqa_prep_kernel,
        grid=(t // tm,),
        in_specs=[row_spec(qw, C_WQ), row_spec(kw, C_WK), row_spec(qw, C_GQ), row_spec(kw, C_GK),
                  gain_spec, gain_spec, gain_spec, gain_spec, rope_spec, rope_spec],
        out_specs=[pl.BlockSpec((tm, qw), lambda i: (i, 0)), pl.BlockSpec((tm, kw), lambda i: (i, 0)),
                   pl.BlockSpec((tm, qw), lambda i: (i, 0)), pl.BlockSpec((tm, kw), lambda i: (i, 0))],
        out_shape=[q_out, k_out, q_out, k_out],
        compiler_params=_params(1),
        name="gqa_prep",
    )(proj, proj, proj, proj, *gains, *rope_h)


def _mla_prep_kernel(mq_ref, mkv_ref, mkr_ref, wq_ref, wkv_ref, qln_ref, kvln_ref, qn_ref, kn_ref,
                     cos_ref, sin_ref, q_out, k_out, v_out):
    width = MLA_NOPE + MLA_ROPE
    cos, sin = cos_ref[...], sin_ref[...]
    q_lat = (_rms(mq_ref[...].astype(F32), MLA_Q_RANK) * qln_ref[...]).astype(BF16)
    kv_lat = (_rms(mkv_ref[...].astype(F32), MLA_KV_RANK) * kvln_ref[...]).astype(BF16)
    kr = mkr_ref[...].astype(F32)
    kr_sq = jnp.sum(kr * kr, axis=-1, keepdims=True)
    kr_rot = _rope(kr * kn_ref[:, LANES:], cos, sin)
    q_scale = width ** -0.5 * LOG2E
    for hd in range(MLA_HEADS):
        lo = hd * MLA_HEAD_PAD
        qu = jnp.dot(q_lat, wq_ref[:, lo:lo + MLA_HEAD_PAD], preferred_element_type=F32)
        kvu = jnp.dot(kv_lat, wkv_ref[:, lo:lo + MLA_HEAD_PAD], preferred_element_type=F32)
        qa = qu[:, :LANES]
        qb = qu[:, LANES:]
        ms = jnp.sum(qa * qa + qb * qb, axis=-1, keepdims=True) / width
        rstd = lax.rsqrt(ms + EPS) * q_scale
        q_out[:, lo:lo + LANES] = (qa * rstd * qn_ref[:, :LANES]).astype(q_out.dtype)
        q_out[:, lo + LANES:lo + 2 * LANES] = _rope(qb * rstd * qn_ref[:, LANES:], cos, sin).astype(q_out.dtype)
        kn = kvu[:, :MLA_NOPE]
        ms = (jnp.sum(kn * kn, axis=-1, keepdims=True) + kr_sq) / width
        rstd = lax.rsqrt(ms + EPS)
        k_out[:, lo:lo + LANES] = (kn * rstd * kn_ref[:, :LANES]).astype(k_out.dtype)
        k_out[:, lo + LANES:lo + 2 * LANES] = (kr_rot * rstd).astype(k_out.dtype)
        v_out[:, hd * MLA_V:(hd + 1) * MLA_V] = kvu[:, MLA_NOPE:].astype(v_out.dtype)


def _mla_prep_call(proj, wq_p, wkv, qln, kvln, qn_p, kn_p, rope_r, layer, tpb):
    t = proj.shape[0]
    tm = _largest_divisor(tpb, (768, 384, 256, 128))
    per_batch = tpb // tm
    hp = MLA_HEADS * MLA_HEAD_PAD
    rope_spec = pl.BlockSpec((tm, LANES), lambda i: (i % per_batch, 0))
    vec = lambda w: pl.BlockSpec((1, w), lambda i: (0, 0))
    return pl.pallas_call(
        _mla_prep_kernel,
        grid=(t // tm,),
        in_specs=[pl.BlockSpec((tm, MLA_Q_RANK), lambda i: (i, C_MQ // MLA_Q_RANK)),
                  pl.BlockSpec((tm, MLA_KV_RANK), lambda i: (i, C_MKV // MLA_KV_RANK)),
                  pl.BlockSpec((tm, LANES), lambda i: (i, C_MKR // LANES)),
                  pl.BlockSpec((None, MLA_Q_RANK, hp), lambda i: (layer, 0, 0)),
                  pl.BlockSpec((None, MLA_KV_RANK, hp), lambda i: (layer, 0, 0)),
                  vec(MLA_Q_RANK), vec(MLA_KV_RANK), vec(MLA_HEAD_PAD), vec(MLA_HEAD_PAD),
                  rope_spec, rope_spec],
        out_specs=[pl.BlockSpec((tm, hp), lambda i: (i, 0)),
                   pl.BlockSpec((tm, hp), lambda i: (i, 0)),
                   pl.BlockSpec((tm, MLA_HEADS * MLA_V), lambda i: (i, 0))],
        out_shape=[jax.ShapeDtypeStruct((t, hp), BF16), jax.ShapeDtypeStruct((t, hp), BF16),
                   jax.ShapeDtypeStruct((t, MLA_HEADS * MLA_V), BF16)],
        compiler_params=_params(1),
        name="mla_prep",
    )(proj, proj, proj, wq_p, wkv, qln, kvln, qn_p, kn_p, *rope_r)


LOG2E = 1.4426950408889634
KEY_CHUNK = 768
UNDERFLOW_GUARD = 2.0 ** -80


def _attention_kernel(*refs, mode, group, tq, seq, tile0):
    if mode == "win":
        sink_ref, q_ref, k_ref, v_ref, o_ref, vt_ref, kn_ref = refs
    else:
        q_ref, k_ref, v_ref, o_ref, vt_ref, kn_ref = refs
    tile = pl.program_id(2) + tile0
    hp, dv, tpb = vt_ref.shape
    dq = k_ref.shape[-1] // hp
    nq = group * tq
    heads = range(hp)
    contract_last = (((1,), (1,)), ((), ()))

    @pl.when(pl.program_id(2) == 0)
    def _():
        for hh in heads:
            vt_ref[hh] = v_ref[:, hh * dv:(hh + 1) * dv].astype(F32).T.astype(BF16)
            kf = k_ref[:, hh * dq:(hh + 1) * dq].astype(F32)
            norms = jnp.dot((kf * kf).astype(BF16), jnp.ones((dq, LANES), BF16), preferred_element_type=F32)
            kn_ref[hh] = jnp.broadcast_to(jnp.max(norms, axis=0, keepdims=True), kn_ref.shape[1:])

    q = [jnp.concatenate([q_ref[:, (hh * group + g) * dq:(hh * group + g + 1) * dq] for g in range(group)],
                         axis=0) for hh in heads]

    def sink_row(hh):
        first = (pl.program_id(1) * hp + hh) * group
        return jnp.concatenate(
            [jnp.full((1, tq), sink_ref[first + g] * LOG2E, F32) for g in range(group)], axis=1)

    def scores(hh, rows):
        return lax.dot_general(k_ref[rows, hh * dq:(hh + 1) * dq], q[hh], contract_last,
                               preferred_element_type=F32)

    def value_dot(hh, cols, p):
        return jnp.dot(vt_ref[hh, :, cols], p.astype(BF16), preferred_element_type=F32)

    def bound_init(hh):
        qf = q[hh].astype(F32)
        q_sq = lax.dot_general(jnp.ones((8, dq), BF16), (qf * qf).astype(BF16), contract_last,
                               preferred_element_type=F32)[0:1, :]
        shift = jnp.sqrt(q_sq * kn_ref[hh, 0:1, 0:1])
        if mode == "win":
            shift = jnp.maximum(shift, sink_row(hh))
            return shift, jnp.exp2(sink_row(hh) - shift), jnp.zeros((dv, nq), F32)
        return shift, jnp.zeros((1, nq), F32), jnp.zeros((dv, nq), F32)

    def bound_update(hh, carry, s, cols, mask=None):
        shift, l, acc = carry
        if mask is not None:
            s = jnp.where(mask(s.shape), s, NEG_INF)
        p = jnp.exp2(s - shift)
        return shift, l + jnp.sum(p, axis=0, keepdims=True), acc + value_dot(hh, cols, p)

    def max_init(hh):
        if mode == "win":
            return sink_row(hh), jnp.ones((1, nq), F32), jnp.zeros((dv, nq), F32)
        return jnp.full((1, nq), NEG_INF, F32), jnp.zeros((1, nq), F32), jnp.zeros((dv, nq), F32)

    def max_update(hh, carry, s, cols, mask=None):
        m, l, acc = carry
        if mask is not None:
            s = jnp.where(mask(s.shape), s, NEG_INF)
        m_new = jnp.maximum(m, jnp.max(s, axis=0, keepdims=True))
        alpha = jnp.exp2(m - m_new)
        p = jnp.exp2(s - m_new)
        return m_new, alpha * l + jnp.sum(p, axis=0, keepdims=True), alpha * acc + value_dot(hh, cols, p)

    def sweep(chunks, init, update):
        carry = [init(hh) for hh in heads]
        s = [scores(hh, chunks[0][0]) for hh in heads]
        for c, (keys, mask) in enumerate(chunks):
            s_next = [scores(hh, chunks[c + 1][0]) for hh in heads] if c + 1 < len(chunks) else None
            carry = [update(hh, carry[hh], s[hh], keys, mask) for hh in heads]
            s = s_next
        return carry

    def write(carry):
        for hh in heads:
            _, l, acc = carry[hh]
            out = acc / l
            for g in range(group):
                col = (hh * group + g) * dv
                o_ref[:, col:col + dv] = out[:, g * tq:(g + 1) * tq].T.astype(o_ref.dtype)

    def attend(chunks):
        fast = sweep(chunks, bound_init, bound_update)
        smallest = functools.reduce(jnp.minimum, [jnp.min(carry[1]) for carry in fast])
        safe = smallest > UNDERFLOW_GUARD

        @pl.when(safe)
        def _():
            write(fast)

        @pl.when(jnp.logical_not(safe))
        def _():
            write(sweep(chunks, max_init, max_update))

    ctx_keys = slice(0, CTX_LEN)

    @pl.when(tile < CTX_LEN // tq)
    def _():
        attend([(ctx_keys, None)])

    @pl.when(tile >= CTX_LEN // tq)
    def _():
        if mode == "win":
            blk = tile - CTX_LEN // tq
            start = jnp.clip((blk - 1) * BLOCK, 0, seq - 3 * BLOCK)
            local = pl.ds(pl.multiple_of(CTX_LEN + start, BLOCK), 3 * BLOCK)

            def band(shape):
                k_pos = start + lax.broadcasted_iota(jnp.int32, shape, 0)
                q_pos = blk * BLOCK + lax.broadcasted_iota(jnp.int32, shape, 1) % tq
                return jnp.abs(q_pos - k_pos) <= WINDOW

            attend([(ctx_keys, None), (local, band)])
        else:
            attend([(slice(c * KEY_CHUNK, (c + 1) * KEY_CHUNK), None) for c in range(tpb // KEY_CHUNK)])


def _attention_call(mode, q_src, k_src, v_src, extras, n_batch, tpb, skip_ctx):
    t = q_src[0].shape[0]
    if mode == "mla":
        n_kv, group, dq, tq, hp = MLA_HEADS, 1, MLA_HEAD_PAD, 256, 4
    else:
        n_kv, group, dq, tq, hp = WIN_KV_HEADS, WIN_HEADS // WIN_KV_HEADS, HEAD_DIM, BLOCK, 2
    dv = HEAD_DIM
    tiles = tpb // tq
    tile0 = CTX_LEN // tq if skip_ctx else 0
    qw, kw, vw = hp * group * dq, hp * dq, hp * dv
    assert q_src[1] % qw == 0 and k_src[1] % kw == 0 and v_src[1] % vw == 0 and n_kv % hp == 0
    q_blk, k_blk, v_blk = q_src[1] // qw, k_src[1] // kw, v_src[1] // vw
    q_spec = pl.BlockSpec((tq, qw), lambda b, h, i: (b * tiles + i + tile0, q_blk + h))
    k_spec = pl.BlockSpec((tpb, kw), lambda b, h, i: (b, k_blk + h))
    v_spec = pl.BlockSpec((tpb, vw), lambda b, h, i: (b, v_blk + h))
    in_specs, args = [], []
    if mode == "win":
        in_specs.append(pl.BlockSpec(memory_space=pltpu.SMEM))
        args.append(extras["sink"])
    in_specs += [q_spec, k_spec, v_spec]
    args += [q_src[0], k_src[0], v_src[0]]
    kern = functools.partial(_attention_kernel, mode=mode, group=group, tq=tq, seq=tpb - CTX_LEN,
                             tile0=tile0)
    return pl.pallas_call(
        kern,
        grid=(n_batch, n_kv // hp, tiles - tile0),
        in_specs=in_specs,
        out_specs=pl.BlockSpec((tq, hp * group * dv), lambda b, h, i: (b * tiles + i + tile0, h)),
        out_shape=jax.ShapeDtypeStruct((t, n_kv * group * dv), BF16),
        scratch_shapes=[pltpu.VMEM((hp, dv, tpb), BF16), pltpu.VMEM((hp, 8, LANES), F32)],
        compiler_params=_params(3),
        name=f"attention_{mode}",
    )(*args)


def _merge_kernel(ow_ref, om_ref, og_ref, gw_ref, gm_ref, gg_ref, wb_ref, wo_ref, x_ref, gate_ref,
                  o_ref, *, tiles_per_batch):
    i, j = pl.program_id(0), pl.program_id(1)
    tm = x_ref.shape[0]

    @pl.when(j == 0)
    def _():
        o_ref[...] = jnp.zeros_like(o_ref)

    y = None
    for br, (o_br, g_br) in enumerate(((ow_ref, gw_ref), (om_ref, gm_ref), (og_ref, gg_ref))):
        term = jax.nn.sigmoid(g_br[...].astype(F32)) * jnp.dot(o_br[...], wb_ref[br],
                                                                preferred_element_type=F32)
        y = term if y is None else y + term
    o_ref[...] += jnp.dot(y.astype(BF16), wo_ref[...], preferred_element_type=F32)

    @pl.when(j == pl.num_programs(1) - 1)
    def _():
        gate = _gate_rows(gate_ref, i // tiles_per_batch, (i % tiles_per_batch) * tm, tm)
        o_ref[...] = x_ref[...] + gate * o_ref[...]


def _merge_call(xs, o_win, o_mla, o_glb, proj, mod, w_branch, w_out, layer, tpb):
    t, d = xs.shape
    tm = _largest_divisor(tpb, (768, 384, 256, 128))
    tn = _largest_divisor(d, (512, 256, 128))
    o_spec = pl.BlockSpec((tm, BRANCH_W), lambda i, j: (i, 0))
    gate_spec = lambda br: pl.BlockSpec((tm, tn), lambda i, j: (i, (C_GATE + br * d) // tn + j))
    kern = functools.partial(_merge_kernel, tiles_per_batch=tpb // tm)
    return pl.pallas_call(
        kern,
        grid=(t // tm, d // tn),
        in_specs=[o_spec, o_spec, o_spec, gate_spec(0), gate_spec(1), gate_spec(2),
                  pl.BlockSpec((None, N_BRANCH, BRANCH_W, tn), lambda i, j: (layer, 0, 0, j)),
                  pl.BlockSpec((None, tn, d), lambda i, j: (layer, j, 0)),
                  pl.BlockSpec((tm, d), lambda i, j: (i, 0)),
                  pl.BlockSpec((None, MOD_ROWS, d), lambda i, j: (layer, 0, 2))],
        out_specs=pl.BlockSpec((tm, d), lambda i, j: (i, 0)),
        out_shape=jax.ShapeDtypeStruct((t, d), F32),
        input_output_aliases={8: 0},
        compiler_params=_params(2),
        name="merge_out_projection",
    )(o_win, o_mla, o_glb, proj, proj, proj, w_branch, w_out, xs, mod)


def _ffn_kernel(x_ref, sh_ref, sc_ref, gate_ref, wg_ref, wu_ref, wd_ref, o_ref, h_ref, *, tiles_per_batch):
    i, f = pl.program_id(0), pl.program_id(1)
    tm = x_ref.shape[0]
    batch, row0 = i // tiles_per_batch, (i % tiles_per_batch) * tm

    @pl.when(f == 0)
    def _():
        _norm_modulate(x_ref, sh_ref, sc_ref, h_ref, batch, row0, tm)
        o_ref[...] = jnp.zeros_like(o_ref)

    h = h_ref[...]
    g = jnp.dot(h, wg_ref[...], preferred_element_type=F32)
    u = jnp.dot(h, wu_ref[...], preferred_element_type=F32)
    act = (g * jax.nn.sigmoid(g) * u).astype(BF16)
    o_ref[...] += jnp.dot(act, wd_ref[...], preferred_element_type=F32)

    @pl.when(f == pl.num_programs(1) - 1)
    def _():
        o_ref[...] = x_ref[...] + _gate_rows(gate_ref, batch, row0, tm) * o_ref[...]


def _ffn_call(xs, mod, w_gate_up, w_down, idx, layer, tpb):
    t, d = xs.shape
    ff = w_down.shape[1]
    tm = _largest_divisor(tpb, (768, 384, 256, 128))
    tf = _largest_divisor(ff, (512, 256, 128))
    n_f = ff // tf
    mod_spec = lambda k: pl.BlockSpec((None, MOD_ROWS, d), lambda i, f: (layer, 0, k))
    kern = functools.partial(_ffn_kernel, tiles_per_batch=tpb // tm)
    return pl.pallas_call(
        kern,
        grid=(t // tm, n_f),
        in_specs=[pl.BlockSpec((tm, d), lambda i, f: (i, 0)), mod_spec(3), mod_spec(4), mod_spec(5),
                  pl.BlockSpec((None, d, tf), lambda i, f: (idx, 0, f)),
                  pl.BlockSpec((None, d, tf), lambda i, f: (idx, 0, f + n_f)),
                  pl.BlockSpec((None, tf, d), lambda i, f: (idx, f, 0))],
        out_specs=pl.BlockSpec((tm, d), lambda i, f: (i, 0)),
        out_shape=jax.ShapeDtypeStruct((t, d), F32),
        scratch_shapes=[pltpu.VMEM((tm, d), BF16)],
        input_output_aliases={0: 0},
        compiler_params=_params(2),
        name="swiglu_ffn",
    )(xs, mod, mod, mod, w_gate_up, w_gate_up, w_down)


MOE_TOKEN_TILE = 256


def _moe_tile_map(step, tiles_per_batch, skip):
    per_batch = tiles_per_batch - skip
    batch, inner = step // per_batch, step % per_batch + skip
    return batch * tiles_per_batch + inner, batch, inner * MOE_TOKEN_TILE


def _router_kernel(x_ref, sh_ref, sc_ref, rhi_ref, rlo_ref, h_ref, o_ref, *, tiles_per_batch, skip):
    tm = x_ref.shape[0]
    _, batch, row0 = _moe_tile_map(pl.program_id(0), tiles_per_batch, skip)
    _norm_modulate(x_ref, sh_ref, sc_ref, h_ref, batch, row0, tm)
    h = h_ref[...]
    h_hi = h.astype(BF16)
    h_lo = (h - h_hi.astype(F32)).astype(BF16)
    logits = (jnp.dot(h_hi, rhi_ref[...], preferred_element_type=F32)
              + jnp.dot(h_hi, rlo_ref[...], preferred_element_type=F32)
              + jnp.dot(h_lo, rhi_ref[...], preferred_element_type=F32))
    lane = lax.broadcasted_iota(jnp.int32, logits.shape, 1).astype(F32)
    logits = jnp.where(lane < N_EXPERTS, logits, -jnp.inf)
    top1 = jnp.max(logits, axis=-1, keepdims=True)
    idx1 = jnp.min(jnp.where(logits == top1, lane, float(LANES)), axis=-1, keepdims=True)
    rest = jnp.where(lane == idx1, -jnp.inf, logits)
    top2 = jnp.max(rest, axis=-1, keepdims=True)
    idx2 = jnp.min(jnp.where(rest == top2, lane, float(LANES)), axis=-1, keepdims=True)
    e2 = jnp.exp(top2 - top1)
    den = 1.0 + e2
    o_ref[...] = jnp.where(lane == 0, 1.0 / den, jnp.where(lane == 1, e2 / den, jnp.where(
        lane == 2, idx1, jnp.where(lane == 3, idx2, 0.0))))


def _router_call(xs, mod, router, layer, tpb, skip):
    t, d = xs.shape
    tm = MOE_TOKEN_TILE
    tiles_per_batch = tpb // tm
    n_tok = t // tiles_per_batch * (tiles_per_batch - skip)
    r_hi = jnp.pad(router, ((0, 0), (0, LANES - N_EXPERTS))).astype(BF16)
    r_lo = jnp.pad(router - r_hi[:, :N_EXPERTS].astype(F32), ((0, 0), (0, LANES - N_EXPERTS))).astype(BF16)
    mod_spec = lambda k: pl.BlockSpec((None, MOD_ROWS, d), lambda i: (layer, 0, k))
    r_spec = pl.BlockSpec((d, LANES), lambda i: (0, 0))
    kern = functools.partial(_router_kernel, tiles_per_batch=tiles_per_batch, skip=skip)
    return pl.pallas_call(
        kern,
        grid=(n_tok // tm,),
        in_specs=[pl.BlockSpec((tm, d), lambda i: (_moe_tile_map(i, tiles_per_batch, skip)[0], 0)),
                  mod_spec(3), mod_spec(4), r_spec, r_spec],
        out_specs=[pl.BlockSpec((tm, d), lambda i: (i, 0)), pl.BlockSpec((tm, LANES), lambda i: (i, 0))],
        out_shape=[jax.ShapeDtypeStruct((n_tok, d), F32), jax.ShapeDtypeStruct((n_tok, LANES), F32)],
        compiler_params=_params(1),
        name="moe_router",
    )(xs, mod, mod, r_hi, r_lo)


def _moe_plan(route, tile):
    n_pairs = 2 * route.shape[0]
    n_tiles = -(-n_pairs // tile) + N_EXPERTS
    expert = route[:, 2:4].astype(jnp.int32).reshape(-1)
    onehot = (expert[:, None] == jnp.arange(N_EXPERTS, dtype=jnp.int32)[None]).astype(jnp.int32)
    csum = jnp.cumsum(onehot, axis=0)
    count = csum[-1]
    padded = (count + tile - 1) // tile * tile
    group_end = jnp.cumsum(padded)
    group_start = group_end - padded
    dest = jnp.sum(onehot * (group_start[None] + csum - 1), axis=1).astype(jnp.int32)
    tile_row = jnp.arange(n_tiles, dtype=jnp.int32) * tile
    tile_expert = jnp.minimum(jnp.sum((tile_row[:, None] >= group_end[None]).astype(jnp.int32), axis=1),
                              N_EXPERTS - 1).astype(jnp.int32)
    n_active = (group_end[-1] // tile).astype(jnp.int32).reshape(1)
    pad_start = jnp.concatenate([group_start + count, group_end[-1:]]).astype(jnp.int32)
    pad_end = jnp.concatenate([group_end, jnp.array([n_tiles * tile], jnp.int32)]).astype(jnp.int32)
    return dest, tile_expert, n_active, pad_start, pad_end, n_tiles


DISPATCH_TILE = 512
DMA_UNROLL = 8


def _dispatch_kernel(pad_lo_ref, pad_hi_ref, dest_ref, h_ref, rows_hbm, zero_ref, sem):
    tm = h_ref.shape[0]

    def row_copy(src_row, dst):
        return pltpu.make_async_copy(src_row, rows_hbm.at[pl.ds(dst, 1), :], sem.at[0])

    @pl.when(pl.program_id(0) == 0)
    def _():
        zero_ref[...] = jnp.zeros_like(zero_ref)
        zero_row = zero_ref.at[pl.ds(0, 1), :]
        for e in range(N_EXPERTS + 1):
            lo, hi = pad_lo_ref[e], pad_hi_ref[e]

            def start_zero(r, carry):
                row_copy(zero_row, r).start()
                return carry

            def wait_zero(r, carry):
                row_copy(zero_row, r).wait()
                return carry

            lax.fori_loop(lo, hi, start_zero, 0)
            lax.fori_loop(lo, hi, wait_zero, 0)

    def start(r, carry):
        for k in range(2):
            row_copy(h_ref.at[pl.ds(r, 1), :], dest_ref[0, 2 * r + k]).start()
        return carry

    def wait(r, carry):
        for k in range(2):
            row_copy(h_ref.at[pl.ds(r, 1), :], dest_ref[0, 2 * r + k]).wait()
        return carry

    lax.fori_loop(0, tm, start, 0, unroll=DMA_UNROLL)
    lax.fori_loop(0, tm, wait, 0, unroll=DMA_UNROLL)


def _dispatch_call(h, dest, pad_lo, pad_hi, n_rows):
    t, d = h.shape
    tm = DISPATCH_TILE
    grid_spec = pltpu.PrefetchScalarGridSpec(
        num_scalar_prefetch=2,
        grid=(t // tm,),
        in_specs=[pl.BlockSpec((None, 1, 2 * tm), lambda i, lo, hi: (i, 0, 0), memory_space=pltpu.SMEM),
                  pl.BlockSpec((tm, d), lambda i, lo, hi: (i, 0))],
        out_specs=pl.BlockSpec(memory_space=pl.ANY),
        scratch_shapes=[pltpu.VMEM((8, d), F32), pltpu.SemaphoreType.DMA((1,))],
    )
    return pl.pallas_call(
        _dispatch_kernel,
        grid_spec=grid_spec,
        out_shape=jax.ShapeDtypeStruct((n_rows, d), F32),
        compiler_params=_params(1),
        name="moe_dispatch",
    )(pad_lo, pad_hi, dest.reshape(t // tm, 1, 2 * tm), h)


def _moe_group_kernel(expert_ref, n_active_ref, x_ref, wg_ref, wu_ref, wd_ref, o_ref, h_ref):
    m, f = pl.program_id(0), pl.program_id(1)

    @pl.when(f == 0)
    def _():
        h_ref[...] = x_ref[...].astype(BF16)
        o_ref[...] = jnp.zeros_like(o_ref)

    @pl.when(m < n_active_ref[0])
    def _():
        h = h_ref[...]
        g = jnp.dot(h, wg_ref[...], preferred_element_type=F32)
        u = jnp.dot(h, wu_ref[...], preferred_element_type=F32)
        act = (g * jax.nn.sigmoid(g) * u).astype(BF16)
        o_ref[...] += jnp.dot(act, wd_ref[...], preferred_element_type=F32)


def _moe_group_call(rows, tile_expert, n_active, w_gate_up, w_down, idx, tile):
    n_rows, d = rows.shape
    ff = w_down.shape[2]
    tf = _largest_divisor(ff, (512, 256, 128))
    n_f = ff // tf

    def f_eff(m, f, na):
        return jnp.where(m < na[0], f, n_f - 1)

    grid_spec = pltpu.PrefetchScalarGridSpec(
        num_scalar_prefetch=2,
        grid=(n_rows // tile, n_f),
        in_specs=[pl.BlockSpec((tile, d), lambda m, f, te, na: (m, 0)),
                  pl.BlockSpec((None, None, d, tf), lambda m, f, te, na: (idx, te[m], 0, f_eff(m, f, na))),
                  pl.BlockSpec((None, None, d, tf),
                               lambda m, f, te, na: (idx, te[m], 0, f_eff(m, f, na) + n_f)),
                  pl.BlockSpec((None, None, tf, d), lambda m, f, te, na: (idx, te[m], f_eff(m, f, na), 0))],
        out_specs=pl.BlockSpec((tile, d), lambda m, f, te, na: (m, 0)),
        scratch_shapes=[pltpu.VMEM((tile, d), BF16)],
    )
    return pl.pallas_call(
        _moe_group_kernel,
        grid_spec=grid_spec,
        out_shape=jax.ShapeDtypeStruct((n_rows, d), F32),
        compiler_params=_params(2),
        name="moe_experts",
    )(tile_expert, n_active, rows, w_gate_up, w_gate_up, w_down)


def _combine_kernel(dest_ref, next_ref, x_ref, route_ref, gate_ref, y_hbm, o_ref, buf, sem, *,
                    tiles_per_batch, skip):
    i, n = pl.program_id(0), pl.num_programs(0)
    tm = x_ref.shape[0]
    slot = i % 2

    def row_copy(dests, to, r, k):
        return pltpu.make_async_copy(y_hbm.at[pl.ds(dests[0, 2 * r + k], 1), :],
                                     buf.at[to, k, pl.ds(r, 1), :], sem.at[to, k])

    def gather(dests, to, wait):
        def body(r, carry):
            for k in range(2):
                copy = row_copy(dests, to, r, k)
                copy.wait() if wait else copy.start()
            return carry
        lax.fori_loop(0, tm, body, 0, unroll=DMA_UNROLL)

    @pl.when(i == 0)
    def _():
        gather(dest_ref, slot, wait=False)

    @pl.when(i + 1 < n)
    def _():
        gather(next_ref, 1 - slot, wait=False)

    gather(dest_ref, slot, wait=True)
    route = route_ref[...]
    mix = route[:, 0:1] * buf[slot, 0] + route[:, 1:2] * buf[slot, 1]
    _, batch, row0 = _moe_tile_map(i, tiles_per_batch, skip)
    o_ref[...] = x_ref[...] + _gate_rows(gate_ref, batch, row0, tm) * mix


def _combine_call(xs, route, mod, y, dest, layer, tpb, skip):
    t, d = xs.shape
    tm = MOE_TOKEN_TILE
    tiles_per_batch = tpb // tm
    n_tok = route.shape[0]
    n_steps = n_tok // tm
    dest = dest.reshape(n_steps, 1, 2 * tm)
    kern = functools.partial(_combine_kernel, tiles_per_batch=tiles_per_batch, skip=skip)
    smem_spec = lambda index: pl.BlockSpec((None, 1, 2 * tm), index, memory_space=pltpu.SMEM)
    return pl.pallas_call(
        kern,
        grid=(n_steps,),
        in_specs=[smem_spec(lambda i: (i, 0, 0)),
                  smem_spec(lambda i: (jnp.minimum(i + 1, n_steps - 1), 0, 0)),
                  pl.BlockSpec((tm, d), lambda i: (_moe_tile_map(i, tiles_per_batch, skip)[0], 0)),
                  pl.BlockSpec((tm, LANES), lambda i: (i, 0)),
                  pl.BlockSpec((None, MOD_ROWS, d), lambda i: (layer, 0, 5)),
                  pl.BlockSpec(memory_space=pl.ANY)],
        out_specs=pl.BlockSpec((tm, d), lambda i: (i, 0)),
        out_shape=jax.ShapeDtypeStruct((n_tok, d), F32),
        scratch_shapes=[pltpu.VMEM((2, 2, tm, d), F32), pltpu.SemaphoreType.DMA((2, 2))],
        input_output_aliases={} if skip else {2: 0},
        compiler_params=_params(1),
        name="moe_combine",
    )(dest, dest, xs, route, mod, y)


def _rope_tables(seq, dim):
    rows = seq // GRID_W
    row = jnp.repeat(jnp.arange(rows), GRID_W).astype(F32)
    col = jnp.tile(jnp.arange(GRID_W), rows).astype(F32)
    quarter = dim // 4
    freqs = ROPE_THETA ** (-jnp.arange(quarter, dtype=F32) / quarter)
    ang_r = row[:, None] * freqs
    ang_c = col[:, None] * freqs
    fill = LANES // 2 - 2 * quarter
    cos_half = [jnp.cos(ang_r), jnp.cos(ang_c), jnp.ones((seq, fill), F32)]
    sin_half = [jnp.sin(ang_r), jnp.sin(ang_c), jnp.zeros((seq, fill), F32)]
    cos = jnp.concatenate(cos_half + cos_half, axis=-1)
    sin = jnp.concatenate([-t for t in sin_half] + sin_half, axis=-1)
    ctx_rows = lambda value: jnp.full((CTX_LEN, LANES), value, F32)
    return jnp.concatenate([ctx_rows(1.0), cos], axis=0), jnp.concatenate([ctx_rows(0.0), sin], axis=0)


def _pair_layout(v):
    quarter = v.shape[-1] // 4
    a, b, c, d = (v[..., k * quarter:(k + 1) * quarter] for k in range(4))
    zeros = jnp.zeros(v.shape[:-1] + (LANES // 2 - 2 * quarter,), v.dtype)
    return jnp.concatenate([a, c, zeros, b, d, zeros], axis=-1)


def _pair_layout_heads(w, n_heads):
    w = w.reshape(w.shape[:-1] + (n_heads, HEAD_DIM))
    return _pair_layout(w).reshape(w.shape[:-2] + (n_heads * HEAD_DIM,))


def _layout_w_in(w_in):
    sizes = (WIN_HEADS * HEAD_DIM, WIN_KV_HEADS * HEAD_DIM, WIN_KV_HEADS * HEAD_DIM,
             MLA_Q_RANK, MLA_KV_RANK, MLA_ROPE,
             GLB_HEADS * HEAD_DIM, GLB_KV_HEADS * HEAD_DIM, GLB_KV_HEADS * HEAD_DIM)
    offs = np.concatenate([[0], np.cumsum(sizes)])
    wq, wk, wv, mq, mkv, mkr, gq, gk, gv = (w_in[..., int(offs[k]):int(offs[k + 1])] for k in range(9))
    gates = w_in[..., int(offs[9]):]
    wq, gq = _pair_layout_heads(wq, WIN_HEADS), _pair_layout_heads(gq, GLB_HEADS)
    wk, gk = _pair_layout_heads(wk, WIN_KV_HEADS), _pair_layout_heads(gk, GLB_KV_HEADS)
    pad = jnp.zeros(w_in.shape[:-1] + (C_GATE - C_MKR - LANES,), w_in.dtype)
    return jnp.concatenate([wq, gq, wk, wv, gk, gv, mq, mkv, _pair_layout(mkr), pad, gates],
                           axis=-1).astype(BF16)


def _mla_head_layout(w):
    w = w.reshape(w.shape[:-1] + (MLA_HEADS, MLA_NOPE + MLA_ROPE))
    w = jnp.concatenate([w[..., :MLA_NOPE], _pair_layout(w[..., MLA_NOPE:])], axis=-1)
    return w.reshape(w.shape[:-2] + (MLA_HEADS * MLA_HEAD_PAD,))


def kernel(x, c, ctx, c_ctx, w_mod, b_mod, w_in, win_q_norm, win_k_norm, win_sink, mla_q_lat_norm,
           mla_kv_lat_norm, mla_w_q_up, mla_w_kv_up, mla_q_norm, mla_k_norm, glb_q_norm, glb_k_norm,
           w_branch, w_out, ffn_w_gate_up, ffn_w_down, moe_router, moe_w_gate_up, moe_w_down):
    n_batch, seq, d = x.shape
    depth = w_mod.shape[0]
    tpb = CTX_LEN + seq
    assert ctx.shape[1] == CTX_LEN and seq % GRID_W == 0 and seq % BLOCK == 0 and seq >= 3 * BLOCK
    assert n_batch <= MOD_ROWS - 8 and C_MKR + LANES <= C_GATE

    xs = jnp.concatenate([ctx, x], axis=1).reshape(n_batch * tpb, d)
    cvec = jnp.zeros((MOD_ROWS, d), F32).at[:n_batch].set(c).at[MOD_ROWS - 8].set(c_ctx)
    mod = _mod_call(cvec, w_mod, b_mod)

    rope_h = _rope_tables(seq, HEAD_DIM)
    rope_r = _rope_tables(seq, MLA_ROPE)
    w_in_p = _layout_w_in(w_in)
    wq_up = _mla_head_layout(mla_w_q_up).astype(BF16)
    wkv_up = mla_w_kv_up.astype(BF16)
    mla_gain = lambda g: jnp.concatenate([g[:MLA_NOPE], _pair_layout(g[MLA_NOPE:])])[None]
    gqa_gain = lambda g: _pair_layout(g)[None]
    w_branch_b, w_out_b = w_branch.astype(BF16), w_out.astype(BF16)
    ffn_gu, ffn_dn = ffn_w_gate_up.astype(BF16), ffn_w_down.astype(BF16)
    moe_gu, moe_dn = moe_w_gate_up.astype(BF16), moe_w_down.astype(BF16)

    latent_only = False
    for layer in range(depth):
        proj = _inproj_call(xs, mod, w_in_p, layer, tpb)
        q_win, k_win, q_glb, k_glb = _gqa_prep_call(
            proj, [gqa_gain(g[layer]) for g in (win_q_norm, win_k_norm, glb_q_norm, glb_k_norm)], rope_h, tpb)
        q_mla, k_mla, v_mla = _mla_prep_call(
            proj, wq_up, wkv_up, mla_q_lat_norm[layer][None], mla_kv_lat_norm[layer][None],
            mla_gain(mla_q_norm[layer]), mla_gain(mla_k_norm[layer]), rope_r, layer, tpb)
        o_win = _attention_call("win", (q_win, 0), (k_win, 0), (proj, C_WV), dict(sink=win_sink[layer]),
                                n_batch, tpb, False)
        o_mla = _attention_call("mla", (q_mla, 0), (k_mla, 0), (v_mla, 0), {}, n_batch, tpb, False)
        o_glb = _attention_call("glb", (q_glb, 0), (k_glb, 0), (proj, C_GV), {}, n_batch, tpb, False)
        xs = _merge_call(xs, o_win, o_mla, o_glb, proj, mod, w_branch_b, w_out_b, layer, tpb)
        if layer % 2 == 0:
            xs = _ffn_call(xs, mod, ffn_gu, ffn_dn, layer // 2, layer, tpb)
        else:
            skip = 1 if layer == depth - 1 else 0
            h, route = _router_call(xs, mod, moe_router[layer // 2], layer, tpb, skip)
            tile = 768 if 2 * h.shape[0] >= 16 * 768 else 128
            dest, tile_expert, n_active, pad_lo, pad_hi, n_tiles = _moe_plan(route, tile)
            rows = _dispatch_call(h, dest, pad_lo, pad_hi, n_tiles * tile)
            y = _moe_group_call(rows, tile_expert, n_active, moe_gu, moe_dn, layer // 2, tile)
            xs = _combine_call(xs, route, mod, y, dest, layer, tpb, skip)
            latent_only = bool(skip)
    if latent_only:
        return xs.reshape(n_batch, seq, d)
    return xs.reshape(n_batch, tpb, d)[:, CTX_LEN:, :]
```

```python
import functools

import jax
import jax.numpy as jnp
import numpy as np
from jax import lax
from jax.experimental import pallas as pl
from jax.experimental.pallas import tpu as pltpu

CTX_LEN = 256
GRID_W = 64
HEAD_DIM = 128
WIN_HEADS = 8
WIN_KV_HEADS = 2
WINDOW = 128
BLOCK = 128
MLA_HEADS = 8
MLA_NOPE = 128
MLA_ROPE = 64
MLA_V = 128
MLA_Q_RANK = 512
MLA_KV_RANK = 256
GLB_HEADS = 8
GLB_KV_HEADS = 2
BRANCH_W = 1024
N_BRANCH = 3
N_EXPERTS = 8
ROPE_THETA = 10000.0
EPS = 1e-6
NEG_INF = -1e30

LANES = 128
MLA_HEAD_PAD = 256
MOD_ROWS = 24
VMEM_LIMIT = 60 * 1024 * 1024

C_WQ = 0
C_GQ = C_WQ + WIN_HEADS * HEAD_DIM
C_WK = C_GQ + GLB_HEADS * HEAD_DIM
C_WV = C_WK + WIN_KV_HEADS * HEAD_DIM
C_GK = C_WV + WIN_KV_HEADS * HEAD_DIM
C_GV = C_GK + GLB_KV_HEADS * HEAD_DIM
C_MQ = C_GV + GLB_KV_HEADS * HEAD_DIM
C_MKV = C_MQ + MLA_Q_RANK
C_MKR = C_MKV + MLA_KV_RANK
C_GATE = 4096

BF16 = jnp.bfloat16
F32 = jnp.float32


def _largest_divisor(n, candidates):
    for cand in candidates:
        if n % cand == 0:
            return cand
    raise ValueError(f"no tile in {candidates} divides {n}")


def _params(n_axes):
    return pltpu.CompilerParams(dimension_semantics=("arbitrary",) * n_axes,
                                vmem_limit_bytes=VMEM_LIMIT)


def _rms(x, width):
    ms = jnp.sum(x * x, axis=-1, keepdims=True) * (1.0 / width)
    return x * lax.rsqrt(ms + EPS)


def _rope(y, cos, sin):
    return y * cos + pltpu.roll(y, LANES // 2, 1) * sin


def _norm_modulate(x_ref, sh_ref, sc_ref, h_ref, batch, row0, n_rows):
    block = 128
    assert n_rows % block == 0 and CTX_LEN % block == 0
    pick = lambda ref, row: ref[pl.ds(row, 1), :]
    sh_lat, sc_lat = pick(sh_ref, batch), 1.0 + pick(sc_ref, batch)
    sh_ctx, sc_ctx = pick(sh_ref, MOD_ROWS - 8), 1.0 + pick(sc_ref, MOD_ROWS - 8)
    for blk in range(n_rows // block):
        rows = slice(blk * block, (blk + 1) * block)
        is_ctx = row0 + blk * block < CTX_LEN
        sh = jnp.where(is_ctx, sh_ctx, sh_lat)
        sc = jnp.where(is_ctx, sc_ctx, sc_lat)
        xf = x_ref[rows, :]
        h_ref[rows, :] = (_rms(xf, xf.shape[-1]) * sc + sh).astype(h_ref.dtype)


def _gate_rows(g_ref, batch, row0, n_rows):
    rows = lax.broadcasted_iota(jnp.int32, (n_rows, 1), 0) + row0
    g_lat = g_ref[pl.ds(batch, 1), :]
    g_ctx = g_ref[pl.ds(MOD_ROWS - 8, 1), :]
    return jnp.where(rows < CTX_LEN, g_ctx, g_lat)


def _mod_kernel(c_ref, w_ref, b_ref, o_ref):
    cv = c_ref[...]
    sc = (cv * jax.nn.sigmoid(cv)).astype(BF16)
    o_ref[...] = jnp.dot(sc, w_ref[...].astype(BF16), preferred_element_type=F32) + b_ref[...]


def _mod_call(cvec, w_mod, b_mod):
    depth, d, n = w_mod.shape
    tn = _largest_divisor(n, (1024, 512, 256, 128))
    return pl.pallas_call(
        _mod_kernel,
        grid=(depth, n // tn),
        in_specs=[pl.BlockSpec((MOD_ROWS, d), lambda l, j: (0, 0)),
                  pl.BlockSpec((None, d, tn), lambda l, j: (l, 0, j)),
                  pl.BlockSpec((None, 1, tn), lambda l, j: (l, 0, j))],
        out_specs=pl.BlockSpec((None, MOD_ROWS, tn), lambda l, j: (l, 0, j)),
        out_shape=jax.ShapeDtypeStruct((depth, MOD_ROWS, n), F32),
        compiler_params=_params(2),
        name="mod_vectors",
    )(cvec, w_mod, b_mod.reshape(depth, 1, n))


def _inproj_kernel(x_hbm, sh_ref, sc_ref, w_ref, o_ref, h_ref, x_buf, sem, *, tiles_per_batch):
    i, n_tiles = pl.program_id(0), pl.num_programs(0)
    tm = h_ref.shape[0]
    slot = i % 2

    def fetch(tile, to):
        start = pl.multiple_of(tile * tm, tm)
        return pltpu.make_async_copy(x_hbm.at[pl.ds(start, tm), :], x_buf.at[to], sem.at[to])

    @pl.when(pl.program_id(1) == 0)
    def _():
        @pl.when(i == 0)
        def _():
            fetch(i, slot).start()

        @pl.when(i + 1 < n_tiles)
        def _():
            fetch(i + 1, 1 - slot).start()

        fetch(i, slot).wait()
        _norm_modulate(x_buf.at[slot], sh_ref, sc_ref, h_ref, i // tiles_per_batch,
                       (i % tiles_per_batch) * tm, tm)

    o_ref[...] = jnp.dot(h_ref[...], w_ref[...], preferred_element_type=F32).astype(o_ref.dtype)


def _inproj_call(xs, mod, w_in_p, layer, tpb):
    t, d = xs.shape
    n = w_in_p.shape[-1]
    tm = _largest_divisor(tpb, (768, 384, 256, 128))
    tn = _largest_divisor(n, (2048, 1024, 512, 256, 128))
    kern = functools.partial(_inproj_kernel, tiles_per_batch=tpb // tm)
    return pl.pallas_call(
        kern,
        grid=(t // tm, n // tn),
        in_specs=[pl.BlockSpec(memory_space=pl.ANY),
                  pl.BlockSpec((None, MOD_ROWS, d), lambda i, j: (layer, 0, 0)),
                  pl.BlockSpec((None, MOD_ROWS, d), lambda i, j: (layer, 0, 1)),
                  pl.BlockSpec((None, d, tn), lambda i, j: (layer, 0, j))],
        out_specs=pl.BlockSpec((tm, tn), lambda i, j: (i, j)),
        out_shape=jax.ShapeDtypeStruct((t, n), BF16),
        scratch_shapes=[pltpu.VMEM((tm, d), BF16), pltpu.VMEM((2, tm, d), F32),
                        pltpu.SemaphoreType.DMA((2,))],
        compiler_params=_params(2),
        name="in_projection",
    )(xs, mod, mod, w_in_p)


def _gqa_prep_kernel(wq_ref, wk_ref, gq_ref, gk_ref, wqg_ref, wkg_ref, gqg_ref, gkg_ref,
                     cos_ref, sin_ref, qa_ref, ka_ref, qc_ref, kc_ref):
    cos, sin = cos_ref[...], sin_ref[...]
    q_scale = HEAD_DIM ** -0.5 * LOG2E
    for src, gain, dst, scale in ((wq_ref, wqg_ref, qa_ref, q_scale), (wk_ref, wkg_ref, ka_ref, None),
                                  (gq_ref, gqg_ref, qc_ref, q_scale), (gk_ref, gkg_ref, kc_ref, None)):
        g = gain[...] if scale is None else gain[...] * scale
        for hd in range(src.shape[1] // HEAD_DIM):
            cols = slice(hd * HEAD_DIM, (hd + 1) * HEAD_DIM)
            y = _rms(src[:, cols].astype(F32), HEAD_DIM) * g
            dst[:, cols] = _rope(y, cos, sin).astype(dst.dtype)


def _gqa_prep_call(proj, gains, rope_h, tpb):
    t = proj.shape[0]
    tm = _largest_divisor(tpb, (384, 256, 128))
    per_batch = tpb // tm
    qw, kw = WIN_HEADS * HEAD_DIM, WIN_KV_HEADS * HEAD_DIM
    row_spec = lambda width, col: pl.BlockSpec((tm, width), lambda i: (i, col // width))
    gain_spec = pl.BlockSpec((1, HEAD_DIM), lambda i: (0, 0))
    rope_spec = pl.BlockSpec((tm, LANES), lambda i: (i % per_batch, 0))
    q_out, k_out = jax.ShapeDtypeStruct((t, qw), BF16), jax.ShapeDtypeStruct((t, kw), BF16)
    return pl.pallas_call(
        _gqa_prep_kernel,
        grid=(t // tm,),
        in_specs=[row_spec(qw, C_WQ), row_spec(kw, C_WK), row_spec(qw, C_GQ), row_spec(kw, C_GK),
                  gain_spec, gain_spec, gain_spec, gain_spec, rope_spec, rope_spec],
        out_specs=[pl.BlockSpec((tm, qw), lambda i: (i, 0)), pl.BlockSpec((tm, kw), lambda i: (i, 0)),
                   pl.BlockSpec((tm, qw), lambda i: (i, 0)), pl.BlockSpec((tm, kw), lambda i: (i, 0))],
        out_shape=[q_out, k_out, q_out, k_out],
        compiler_params=_params(1),
        name="gqa_prep",
    )(proj, proj, proj, proj, *gains, *rope_h)


def _mla_prep_kernel(mq_ref, mkv_ref, mkr_ref, wq_ref, wkv_ref, qln_ref, kvln_ref, qn_ref, kn_ref,
                     cos_ref, sin_ref, q_out, k_out, v_out):
    width = MLA_NOPE + MLA_ROPE
    cos, sin = cos_ref[...], sin_ref[...]
    q_lat = (_rms(mq_ref[...].astype(F32), MLA_Q_RANK) * qln_ref[...]).astype(BF16)
    kv_lat = (_rms(mkv_ref[...].astype(F32), MLA_KV_RANK) * kvln_ref[...]).astype(BF16)
    kr = mkr_ref[...].astype(F32)
    kr_sq = jnp.sum(kr * kr, axis=-1, keepdims=True)
    kr_rot = _rope(kr * kn_ref[:, LANES:], cos, sin)
    q_scale = width ** -0.5 * LOG2E
    for hd in range(MLA_HEADS):
        lo = hd * MLA_HEAD_PAD
        qu = jnp.dot(q_lat, wq_ref[:, lo:lo + MLA_HEAD_PAD], preferred_element_type=F32)
        kvu = jnp.dot(kv_lat, wkv_ref[:, lo:lo + MLA_HEAD_PAD], preferred_element_type=F32)
        qa = qu[:, :LANES]
        qb = qu[:, LANES:]
        ms = jnp.sum(qa * qa + qb * qb, axis=-1, keepdims=True) / width
        rstd = lax.rsqrt(ms + EPS) * q_scale
        q_out[:, lo:lo + LANES] = (qa * rstd * qn_ref[:, :LANES]).astype(q_out.dtype)
        q_out[:, lo + LANES:lo + 2 * LANES] = _rope(qb * rstd * qn_ref[:, LANES:], cos, sin).astype(q_out.dtype)
        kn = kvu[:, :MLA_NOPE]
        ms = (jnp.sum(kn * kn, axis=-1, keepdims=True) + kr_sq) / width
        rstd = lax.rsqrt(ms + EPS)
        k_out[:, lo:lo + LANES] = (kn * rstd * kn_ref[:, :LANES]).astype(k_out.dtype)
        k_out[:, lo + LANES:lo + 2 * LANES] = (kr_rot * rstd).astype(k_out.dtype)
        v_out[:, hd * MLA_V:(hd + 1) * MLA_V] = kvu[:, MLA_NOPE:].astype(v_out.dtype)


def _mla_prep_call(proj, wq_p, wkv, qln, kvln, qn_p, kn_p, rope_r, layer, tpb):
    t = proj.shape[0]
    tm = _largest_divisor(tpb, (768, 384, 256, 128))
    per_batch = tpb // tm
    hp = MLA_HEADS * MLA_HEAD_PAD
    rope_spec = pl.BlockSpec((tm, LANES), lambda i: (i % per_batch, 0))
    vec = lambda w: pl.BlockSpec((1, w), lambda i: (0, 0))
    return pl.pallas_call(
        _mla_prep_kernel,
        grid=(t // tm,),
        in_specs=[pl.BlockSpec((tm, MLA_Q_RANK), lambda i: (i, C_MQ // MLA_Q_RANK)),
                  pl.BlockSpec((tm, MLA_KV_RANK), lambda i: (i, C_MKV // MLA_KV_RANK)),
                  pl.BlockSpec((tm, LANES), lambda i: (i, C_MKR // LANES)),
                  pl.BlockSpec((None, MLA_Q_RANK, hp), lambda i: (layer, 0, 0)),
                  pl.BlockSpec((None, MLA_KV_RANK, hp), lambda i: (layer, 0, 0)),
                  vec(MLA_Q_RANK), vec(MLA_KV_RANK), vec(MLA_HEAD_PAD), vec(MLA_HEAD_PAD),
                  rope_spec, rope_spec],
        out_specs=[pl.BlockSpec((tm, hp), lambda i: (i, 0)),
                   pl.BlockSpec((tm, hp), lambda i: (i, 0)),
                   pl.BlockSpec((tm, MLA_HEADS * MLA_V), lambda i: (i, 0))],
        out_shape=[jax.ShapeDtypeStruct((t, hp), BF16), jax.ShapeDtypeStruct((t, hp), BF16),
                   jax.ShapeDtypeStruct((t, MLA_HEADS * MLA_V), BF16)],
        compiler_params=_params(1),
        name="mla_prep",
    )(proj, proj, proj, wq_p, wkv, qln, kvln, qn_p, kn_p, *rope_r)


LOG2E = 1.4426950408889634
KEY_CHUNK = 768
UNDERFLOW_GUARD = 2.0 ** -80


def _attention_kernel(*refs, mode, group, tq, seq, tile0):
    if mode == "win":
        sink_ref, q_ref, k_ref, v_ref, o_ref, vt_ref, kn_ref = refs
    else:
        q_ref, k_ref, v_ref, o_ref, vt_ref, kn_ref = refs
    tile = pl.program_id(2) + tile0
    hp, dv, tpb = vt_ref.shape
    dq = k_ref.shape[-1] // hp
    nq = group * tq
    heads = range(hp)
    contract_last = (((1,), (1,)), ((), ()))

    @pl.when(pl.program_id(2) == 0)
    def _():
        for hh in heads:
            vt_ref[hh] = v_ref[:, hh * dv:(hh + 1) * dv].astype(F32).T.astype(BF16)
            kf = k_ref[:, hh * dq:(hh + 1) * dq].astype(F32)
            norms = jnp.dot((kf * kf).astype(BF16), jnp.ones((dq, LANES), BF16), preferred_element_type=F32)
            kn_ref[hh] = jnp.broadcast_to(jnp.max(norms, axis=0, keepdims=True), kn_ref.shape[1:])

    q = [jnp.concatenate([q_ref[:, (hh * group + g) * dq:(hh * group + g + 1) * dq] for g in range(group)],
                         axis=0) for hh in heads]

    def sink_row(hh):
        first = (pl.program_id(1) * hp + hh) * group
        return jnp.concatenate(
            [jnp.full((1, tq), sink_ref[first + g] * LOG2E, F32) for g in range(group)], axis=1)

    def scores(hh, rows):
        return lax.dot_general(k_ref[rows, hh * dq:(hh + 1) * dq], q[hh], contract_last,
                               preferred_element_type=F32)

    def value_dot(hh, cols, p):
        return jnp.dot(vt_ref[hh, :, cols], p.astype(BF16), preferred_element_type=F32)

    def bound_init(hh):
        qf = q[hh].astype(F32)
        q_sq = lax.dot_general(jnp.ones((8, dq), BF16), (qf * qf).astype(BF16), contract_last,
                               preferred_element_type=F32)[0:1, :]
        shift = jnp.sqrt(q_sq * kn_ref[hh, 0:1, 0:1])
        if mode == "win":
            shift = jnp.maximum(shift, sink_row(hh))
            return shift, jnp.exp2(sink_row(hh) - shift), jnp.zeros((dv, nq), F32)
        return shift, jnp.zeros((1, nq), F32), jnp.zeros((dv, nq), F32)

    def bound_update(hh, carry, s, cols, mask=None):
        shift, l, acc = carry
        if mask is not None:
            s = jnp.where(mask(s.shape), s, NEG_INF)
        p = jnp.exp2(s - shift)
        return shift, l + jnp.sum(p, axis=0, keepdims=True), acc + value_dot(hh, cols, p)

    def max_init(hh):
        if mode == "win":
            return sink_row(hh), jnp.ones((1, nq), F32), jnp.zeros((dv, nq), F32)
        return jnp.full((1, nq), NEG_INF, F32), jnp.zeros((1, nq), F32), jnp.zeros((dv, nq), F32)

    def max_update(hh, carry, s, cols, mask=None):
        m, l, acc = carry
        if mask is not None:
            s = jnp.where(mask(s.shape), s, NEG_INF)
        m_new = jnp.maximum(m, jnp.max(s, axis=0, keepdims=True))
        alpha = jnp.exp2(m - m_new)
        p = jnp.exp2(s - m_new)
        return m_new, alpha * l + jnp.sum(p, axis=0, keepdims=True), alpha * acc + value_dot(hh, cols, p)

    def sweep(chunks, init, update):
        carry = [init(hh) for hh in heads]
        s = [scores(hh, chunks[0][0]) for hh in heads]
        for c, (keys, mask) in enumerate(chunks):
            s_next = [scores(hh, chunks[c + 1][0]) for hh in heads] if c + 1 < len(chunks) else None
            carry = [update(hh, carry[hh], s[hh], keys, mask) for hh in heads]
            s = s_next
        return carry

    def write(carry):
        for hh in heads:
            _, l, acc = carry[hh]
            out = acc / l
            for g in range(group):
                col = (hh * group + g) * dv
                o_ref[:, col:col + dv] = out[:, g * tq:(g + 1) * tq].T.astype(o_ref.dtype)

    def attend(chunks):
        fast = sweep(chunks, bound_init, bound_update)
        smallest = functools.reduce(jnp.minimum, [jnp.min(carry[1]) for carry in fast])
        safe = smallest > UNDERFLOW_GUARD

        @pl.when(safe)
        def _():
            write(fast)

        @pl.when(jnp.logical_not(safe))
        def _():
            write(sweep(chunks, max_init, max_update))

    ctx_keys = slice(0, CTX_LEN)

    @pl.when(tile < CTX_LEN // tq)
    def _():
        attend([(ctx_keys, None)])

    @pl.when(tile >= CTX_LEN // tq)
    def _():
        if mode == "win":
            blk = tile - CTX_LEN // tq
            start = jnp.clip((blk - 1) * BLOCK, 0, seq - 3 * BLOCK)
            local = pl.ds(pl.multiple_of(CTX_LEN + start, BLOCK), 3 * BLOCK)

            def band(shape):
                k_pos = start + lax.broadcasted_iota(jnp.int32, shape, 0)
                q_pos = blk * BLOCK + lax.broadcasted_iota(jnp.int32, shape, 1) % tq
                return jnp.abs(q_pos - k_pos) <= WINDOW

            attend([(ctx_keys, None), (local, band)])
        else:
            attend([(slice(c * KEY_CHUNK, (c + 1) * KEY_CHUNK), None) for c in range(tpb // KEY_CHUNK)])


def _attention_call(mode, q_src, k_src, v_src, extras, n_batch, tpb, skip_ctx):
    t = q_src[0].shape[0]
    if mode == "mla":
        n_kv, group, dq, tq, hp = MLA_HEADS, 1, MLA_HEAD_PAD, 256, 4
    else:
        n_kv, group, dq, tq, hp = WIN_KV_HEADS, WIN_HEADS // WIN_KV_HEADS, HEAD_DIM, BLOCK, 2
    dv = HEAD_DIM
    tiles = tpb // tq
    tile0 = CTX_LEN // tq if skip_ctx else 0
    qw, kw, vw = hp * group * dq, hp * dq, hp * dv
    assert q_src[1] % qw == 0 and k_src[1] % kw == 0 and v_src[1] % vw == 0 and n_kv % hp == 0
    q_blk, k_blk, v_blk = q_src[1] // qw, k_src[1] // kw, v_src[1] // vw
    q_spec = pl.BlockSpec((tq, qw), lambda b, h, i: (b * tiles + i + tile0, q_blk + h))
    k_spec = pl.BlockSpec((tpb, kw), lambda b, h, i: (b, k_blk + h))
    v_spec = pl.BlockSpec((tpb, vw), lambda b, h, i: (b, v_blk + h))
    in_specs, args = [], []
    if mode == "win":
        in_specs.append(pl.BlockSpec(memory_space=pltpu.SMEM))
        args.append(extras["sink"])
    in_specs += [q_spec, k_spec, v_spec]
    args += [q_src[0], k_src[0], v_src[0]]
    kern = functools.partial(_attention_kernel, mode=mode, group=group, tq=tq, seq=tpb - CTX_LEN,
                             tile0=tile0)
    return pl.pallas_call(
        kern,
        grid=(n_batch, n_kv // hp, tiles - tile0),
        in_specs=in_specs,
        out_specs=pl.BlockSpec((tq, hp * group * dv), lambda b, h, i: (b * tiles + i + tile0, h)),
        out_shape=jax.ShapeDtypeStruct((t, n_kv * group * dv), BF16),
        scratch_shapes=[pltpu.VMEM((hp, dv, tpb), BF16), pltpu.VMEM((hp, 8, LANES), F32)],
        compiler_params=_params(3),
        name=f"attention_{mode}",
    )(*args)


def _merge_kernel(ow_ref, om_ref, og_ref, gw_ref, gm_ref, gg_ref, wb_ref, wo_ref, x_ref, gate_ref,
                  o_ref, *, tiles_per_batch):
    i, j = pl.program_id(0), pl.program_id(1)
    tm = x_ref.shape[0]

    @pl.when(j == 0)
    def _():
        o_ref[...] = jnp.zeros_like(o_ref)

    y = None
    for br, (o_br, g_br) in enumerate(((ow_ref, gw_ref), (om_ref, gm_ref), (og_ref, gg_ref))):
        term = jax.nn.sigmoid(g_br[...].astype(F32)) * jnp.dot(o_br[...], wb_ref[br],
                                                                preferred_element_type=F32)
        y = term if y is None else y + term
    o_ref[...] += jnp.dot(y.astype(BF16), wo_ref[...], preferred_element_type=F32)

    @pl.when(j == pl.num_programs(1) - 1)
    def _():
        gate = _gate_rows(gate_ref, i // tiles_per_batch, (i % tiles_per_batch) * tm, tm)
        o_ref[...] = x_ref[...] + gate * o_ref[...]


def _merge_call(xs, o_win, o_mla, o_glb, proj, mod, w_branch, w_out, layer, tpb):
    t, d = xs.shape
    tm = _largest_divisor(tpb, (768, 384, 256, 128))
    tn = _largest_divisor(d, (512, 256, 128))
    o_spec = pl.BlockSpec((tm, BRANCH_W), lambda i, j: (i, 0))
    gate_spec = lambda br: pl.BlockSpec((tm, tn), lambda i, j: (i, (C_GATE + br * d) // tn + j))
    kern = functools.partial(_merge_kernel, tiles_per_batch=tpb // tm)
    return pl.pallas_call(
        kern,
        grid=(t // tm, d // tn),
        in_specs=[o_spec, o_spec, o_spec, gate_spec(0), gate_spec(1), gate_spec(2),
                  pl.BlockSpec((None, N_BRANCH, BRANCH_W, tn), lambda i, j: (layer, 0, 0, j)),
                  pl.BlockSpec((None, tn, d), lambda i, j: (layer, j, 0)),
                  pl.BlockSpec((tm, d), lambda i, j: (i, 0)),
                  pl.BlockSpec((None, MOD_ROWS, d), lambda i, j: (layer, 0, 2))],
        out_specs=pl.BlockSpec((tm, d), lambda i, j: (i, 0)),
        out_shape=jax.ShapeDtypeStruct((t, d), F32),
        input_output_aliases={8: 0},
        compiler_params=_params(2),
        name="merge_out_projection",
    )(o_win, o_mla, o_glb, proj, proj, proj, w_branch, w_out, xs, mod)


def _ffn_kernel(x_ref, sh_ref, sc_ref, gate_ref, wg_ref, wu_ref, wd_ref, o_ref, h_ref, *, tiles_per_batch):
    i, f = pl.program_id(0), pl.program_id(1)
    tm = x_ref.shape[0]
    batch, row0 = i // tiles_per_batch, (i % tiles_per_batch) * tm

    @pl.when(f == 0)
    def _():
        _norm_modulate(x_ref, sh_ref, sc_ref, h_ref, batch, row0, tm)
        o_ref[...] = jnp.zeros_like(o_ref)

    h = h_ref[...]
    g = jnp.dot(h, wg_ref[...], preferred_element_type=F32)
    u = jnp.dot(h, wu_ref[...], preferred_element_type=F32)
    act = (g * jax.nn.sigmoid(g) * u).astype(BF16)
    o_ref[...] += jnp.dot(act, wd_ref[...], preferred_element_type=F32)

    @pl.when(f == pl.num_programs(1) - 1)
    def _():
        o_ref[...] = x_ref[...] + _gate_rows(gate_ref, batch, row0, tm) * o_ref[...]


def _ffn_call(xs, mod, w_gate_up, w_down, idx, layer, tpb):
    t, d = xs.shape
    ff = w_down.shape[1]
    tm = _largest_divisor(tpb, (768, 384, 256, 128))
    tf = _largest_divisor(ff, (512, 256, 128))
    n_f = ff // tf
    mod_spec = lambda k: pl.BlockSpec((None, MOD_ROWS, d), lambda i, f: (layer, 0, k))
    kern = functools.partial(_ffn_kernel, tiles_per_batch=tpb // tm)
    return pl.pallas_call(
        kern,
        grid=(t // tm, n_f),
        in_specs=[pl.BlockSpec((tm, d), lambda i, f: (i, 0)), mod_spec(3), mod_spec(4), mod_spec(5),
                  pl.BlockSpec((None, d, tf), lambda i, f: (idx, 0, f)),
                  pl.BlockSpec((None, d, tf), lambda i, f: (idx, 0, f + n_f)),
                  pl.BlockSpec((None, tf, d), lambda i, f: (idx, f, 0))],
        out_specs=pl.BlockSpec((tm, d), lambda i, f: (i, 0)),
        out_shape=jax.ShapeDtypeStruct((t, d), F32),
        scratch_shapes=[pltpu.VMEM((tm, d), BF16)],
        input_output_aliases={0: 0},
        compiler_params=_params(2),
        name="swiglu_ffn",
    )(xs, mod, mod, mod, w_gate_up, w_gate_up, w_down)


MOE_TOKEN_TILE = 256


def _moe_tile_map(step, tiles_per_batch, skip):
    per_batch = tiles_per_batch - skip
    batch, inner = step // per_batch, step % per_batch + skip
    return batch * tiles_per_batch + inner, batch, inner * MOE_TOKEN_TILE


def _router_kernel(x_ref, sh_ref, sc_ref, rhi_ref, rlo_ref, h_ref, o_ref, *, tiles_per_batch, skip):
    tm = x_ref.shape[0]
    _, batch, row0 = _moe_tile_map(pl.program_id(0), tiles_per_batch, skip)
    _norm_modulate(x_ref, sh_ref, sc_ref, h_ref, batch, row0, tm)
    h = h_ref[...]
    h_hi = h.astype(BF16)
    h_lo = (h - h_hi.astype(F32)).astype(BF16)
    logits = (jnp.dot(h_hi, rhi_ref[...], preferred_element_type=F32)
              + jnp.dot(h_hi, rlo_ref[...], preferred_element_type=F32)
              + jnp.dot(h_lo, rhi_ref[...], preferred_element_type=F32))
    lane = lax.broadcasted_iota(jnp.int32, logits.shape, 1).astype(F32)
    logits = jnp.where(lane < N_EXPERTS, logits, -jnp.inf)
    top1 = jnp.max(logits, axis=-1, keepdims=True)
    idx1 = jnp.min(jnp.where(logits == top1, lane, float(LANES)), axis=-1, keepdims=True)
    rest = jnp.where(lane == idx1, -jnp.inf, logits)
    top2 = jnp.max(rest, axis=-1, keepdims=True)
    idx2 = jnp.min(jnp.where(rest == top2, lane, float(LANES)), axis=-1, keepdims=True)
    e2 = jnp.exp(top2 - top1)
    den = 1.0 + e2
    o_ref[...] = jnp.where(lane == 0, 1.0 / den, jnp.where(lane == 1, e2 / den, jnp.where(
        lane == 2, idx1, jnp.where(lane == 3, idx2, 0.0))))


def _router_call(xs, mod, router, layer, tpb, skip):
    t, d = xs.shape
    tm = MOE_TOKEN_TILE
    tiles_per_batch = tpb // tm
    n_tok = t // tiles_per_batch * (tiles_per_batch - skip)
    r_hi = jnp.pad(router, ((0, 0), (0, LANES - N_EXPERTS))).astype(BF16)
    r_lo = jnp.pad(router - r_hi[:, :N_EXPERTS].astype(F32), ((0, 0), (0, LANES - N_EXPERTS))).astype(BF16)
    mod_spec = lambda k: pl.BlockSpec((None, MOD_ROWS, d), lambda i: (layer, 0, k))
    r_spec = pl.BlockSpec((d, LANES), lambda i: (0, 0))
    kern = functools.partial(_router_kernel, tiles_per_batch=tiles_per_batch, skip=skip)
    return pl.pallas_call(
        kern,
        grid=(n_tok // tm,),
        in_specs=[pl.BlockSpec((tm, d), lambda i: (_moe_tile_map(i, tiles_per_batch, skip)[0], 0)),
                  mod_spec(3), mod_spec(4), r_spec, r_spec],
        out_specs=[pl.BlockSpec((tm, d), lambda i: (i, 0)), pl.BlockSpec((tm, LANES), lambda i: (i, 0))],
        out_shape=[jax.ShapeDtypeStruct((n_tok, d), F32), jax.ShapeDtypeStruct((n_tok, LANES), F32)],
        compiler_params=_params(1),
        name="moe_router",
    )(xs, mod, mod, r_hi, r_lo)


def _moe_plan(route, tile):
    n_pairs = 2 * route.shape[0]
    n_tiles = -(-n_pairs // tile) + N_EXPERTS
    expert = route[:, 2:4].astype(jnp.int32).reshape(-1)
    onehot = (expert[:, None] == jnp.arange(N_EXPERTS, dtype=jnp.int32)[None]).astype(jnp.int32)
    csum = jnp.cumsum(onehot, axis=0)
    count = csum[-1]
    padded = (count + tile - 1) // tile * tile
    group_end = jnp.cumsum(padded)
    group_start = group_end - padded
    dest = jnp.sum(onehot * (group_start[None] + csum - 1), axis=1).astype(jnp.int32)
    tile_row = jnp.arange(n_tiles, dtype=jnp.int32) * tile
    tile_expert = jnp.minimum(jnp.sum((tile_row[:, None] >= group_end[None]).astype(jnp.int32), axis=1),
                              N_EXPERTS - 1).astype(jnp.int32)
    n_active = (group_end[-1] // tile).astype(jnp.int32).reshape(1)
    pad_start = jnp.concatenate([group_start + count, group_end[-1:]]).astype(jnp.int32)
    pad_end = jnp.concatenate([group_end, jnp.array([n_tiles * tile], jnp.int32)]).astype(jnp.int32)
    return dest, tile_expert, n_active, pad_start, pad_end, n_tiles


DISPATCH_TILE = 512
DMA_UNROLL = 8


def _dispatch_kernel(pad_lo_ref, pad_hi_ref, dest_ref, h_ref, rows_hbm, zero_ref, sem):
    tm = h_ref.shape[0]

    def row_copy(src_row, dst):
        return pltpu.make_async_copy(src_row, rows_hbm.at[pl.ds(dst, 1), :], sem.at[0])

    @pl.when(pl.program_id(0) == 0)
    def _():
        zero_ref[...] = jnp.zeros_like(zero_ref)
        zero_row = zero_ref.at[pl.ds(0, 1), :]
        for e in range(N_EXPERTS + 1):
            lo, hi = pad_lo_ref[e], pad_hi_ref[e]

            def start_zero(r, carry):
                row_copy(zero_row, r).start()
                return carry

            def wait_zero(r, carry):
                row_copy(zero_row, r).wait()
                return carry

            lax.fori_loop(lo, hi, start_zero, 0)
            lax.fori_loop(lo, hi, wait_zero, 0)

    def start(r, carry):
        for k in range(2):
            row_copy(h_ref.at[pl.ds(r, 1), :], dest_ref[0, 2 * r + k]).start()
        return carry

    def wait(r, carry):
        for k in range(2):
            row_copy(h_ref.at[pl.ds(r, 1), :], dest_ref[0, 2 * r + k]).wait()
        return carry

    lax.fori_loop(0, tm, start, 0, unroll=DMA_UNROLL)
    lax.fori_loop(0, tm, wait, 0, unroll=DMA_UNROLL)


def _dispatch_call(h, dest, pad_lo, pad_hi, n_rows):
    t, d = h.shape
    tm = DISPATCH_TILE
    grid_spec = pltpu.PrefetchScalarGridSpec(
        num_scalar_prefetch=2,
        grid=(t // tm,),
        in_specs=[pl.BlockSpec((None, 1, 2 * tm), lambda i, lo, hi: (i, 0, 0), memory_space=pltpu.SMEM),
                  pl.BlockSpec((tm, d), lambda i, lo, hi: (i, 0))],
        out_specs=pl.BlockSpec(memory_space=pl.ANY),
        scratch_shapes=[pltpu.VMEM((8, d), F32), pltpu.SemaphoreType.DMA((1,))],
    )
    return pl.pallas_call(
        _dispatch_kernel,
        grid_spec=grid_spec,
        out_shape=jax.ShapeDtypeStruct((n_rows, d), F32),
        compiler_params=_params(1),
        name="moe_dispatch",
    )(pad_lo, pad_hi, dest.reshape(t // tm, 1, 2 * tm), h)


def _moe_group_kernel(expert_ref, n_active_ref, x_ref, wg_ref, wu_ref, wd_ref, o_ref, h_ref):
    m, f = pl.program_id(0), pl.program_id(1)

    @pl.when(f == 0)
    def _():
        h_ref[...] = x_ref[...].astype(BF16)
        o_ref[...] = jnp.zeros_like(o_ref)

    @pl.when(m < n_active_ref[0])
    def _():
        h = h_ref[...]
        g = jnp.dot(h, wg_ref[...], preferred_element_type=F32)
        u = jnp.dot(h, wu_ref[...], preferred_element_type=F32)
        act = (g * jax.nn.sigmoid(g) * u).astype(BF16)
        o_ref[...] += jnp.dot(act, wd_ref[...], preferred_element_type=F32)


def _moe_group_call(rows, tile_expert, n_active, w_gate_up, w_down, idx, tile):
    n_rows, d = rows.shape
    ff = w_down.shape[2]
    tf = _largest_divisor(ff, (512, 256, 128))
    n_f = ff // tf

    def f_eff(m, f, na):
        return jnp.where(m < na[0], f, n_f - 1)

    grid_spec = pltpu.PrefetchScalarGridSpec(
        num_scalar_prefetch=2,
        grid=(n_rows // tile, n_f),
        in_specs=[pl.BlockSpec((tile, d), lambda m, f, te, na: (m, 0)),
                  pl.BlockSpec((None, None, d, tf), lambda m, f, te, na: (idx, te[m], 0, f_eff(m, f, na))),
                  pl.BlockSpec((None, None, d, tf),
                               lambda m, f, te, na: (idx, te[m], 0, f_eff(m, f, na) + n_f)),
                  pl.BlockSpec((None, None, tf, d), lambda m, f, te, na: (idx, te[m], f_eff(m, f, na), 0))],
        out_specs=pl.BlockSpec((tile, d), lambda m, f, te, na: (m, 0)),
        scratch_shapes=[pltpu.VMEM((tile, d), BF16)],
    )
    return pl.pallas_call(
        _moe_group_kernel,
        grid_spec=grid_spec,
        out_shape=jax.ShapeDtypeStruct((n_rows, d), F32),
        compiler_params=_params(2),
        name="moe_experts",
    )(tile_expert, n_active, rows, w_gate_up, w_gate_up, w_down)


def _combine_kernel(dest_ref, next_ref, x_ref, route_ref, gate_ref, y_hbm, o_ref, buf, sem, *,
                    tiles_per_batch, skip):
    i, n = pl.program_id(0), pl.num_programs(0)
    tm = x_ref.shape[0]
    slot = i % 2

    def row_copy(dests, to, r, k):
        return pltpu.make_async_copy(y_hbm.at[pl.ds(dests[0, 2 * r + k], 1), :],
                                     buf.at[to, k, pl.ds(r, 1), :], sem.at[to, k])

    def gather(dests, to, wait):
        def body(r, carry):
            for k in range(2):
                copy = row_copy(dests, to, r, k)
                copy.wait() if wait else copy.start()
            return carry
        lax.fori_loop(0, tm, body, 0, unroll=DMA_UNROLL)

    @pl.when(i == 0)
    def _():
        gather(dest_ref, slot, wait=False)

    @pl.when(i + 1 < n)
    def _():
        gather(next_ref, 1 - slot, wait=False)

    gather(dest_ref, slot, wait=True)
    route = route_ref[...]
    mix = route[:, 0:1] * buf[slot, 0] + route[:, 1:2] * buf[slot, 1]
    _, batch, row0 = _moe_tile_map(i, tiles_per_batch, skip)
    o_ref[...] = x_ref[...] + _gate_rows(gate_ref, batch, row0, tm) * mix


def _combine_call(xs, route, mod, y, dest, layer, tpb, skip):
    t, d = xs.shape
    tm = MOE_TOKEN_TILE
    tiles_per_batch = tpb // tm
    n_tok = route.shape[0]
    n_steps = n_tok // tm
    dest = dest.reshape(n_steps, 1, 2 * tm)
    kern = functools.partial(_combine_kernel, tiles_per_batch=tiles_per_batch, skip=skip)
    smem_spec = lambda index: pl.BlockSpec((None, 1, 2 * tm), index, memory_space=pltpu.SMEM)
    return pl.pallas_call(
        kern,
        grid=(n_steps,),
        in_specs=[smem_spec(lambda i: (i, 0, 0)),
                  smem_spec(lambda i: (jnp.minimum(i + 1, n_steps - 1), 0, 0)),
                  pl.BlockSpec((tm, d), lambda i: (_moe_tile_map(i, tiles_per_batch, skip)[0], 0)),
                  pl.BlockSpec((tm, LANES), lambda i: (i, 0)),
                  pl.BlockSpec((None, MOD_ROWS, d), lambda i: (layer, 0, 5)),
                  pl.BlockSpec(memory_space=pl.ANY)],
        out_specs=pl.BlockSpec((tm, d), lambda i: (i, 0)),
        out_shape=jax.ShapeDtypeStruct((n_tok, d), F32),
        scratch_shapes=[pltpu.VMEM((2, 2, tm, d), F32), pltpu.SemaphoreType.DMA((2, 2))],
        input_output_aliases={} if skip else {2: 0},
        compiler_params=_params(1),
        name="moe_combine",
    )(dest, dest, xs, route, mod, y)


def _rope_tables(seq, dim):
    rows = seq // GRID_W
    row = jnp.repeat(jnp.arange(rows), GRID_W).astype(F32)
    col = jnp.tile(jnp.arange(GRID_W), rows).astype(F32)
    quarter = dim // 4
    freqs = ROPE_THETA ** (-jnp.arange(quarter, dtype=F32) / quarter)
    ang_r = row[:, None] * freqs
    ang_c = col[:, None] * freqs
    fill = LANES // 2 - 2 * quarter
    cos_half = [jnp.cos(ang_r), jnp.cos(ang_c), jnp.ones((seq, fill), F32)]
    sin_half = [jnp.sin(ang_r), jnp.sin(ang_c), jnp.zeros((seq, fill), F32)]
    cos = jnp.concatenate(cos_half + cos_half, axis=-1)
    sin = jnp.concatenate([-t for t in sin_half] + sin_half, axis=-1)
    ctx_rows = lambda value: jnp.full((CTX_LEN, LANES), value, F32)
    return jnp.concatenate([ctx_rows(1.0), cos], axis=0), jnp.concatenate([ctx_rows(0.0), sin], axis=0)


def _pair_layout(v):
    quarter = v.shape[-1] // 4
    a, b, c, d = (v[..., k * quarter:(k + 1) * quarter] for k in range(4))
    zeros = jnp.zeros(v.shape[:-1] + (LANES // 2 - 2 * quarter,), v.dtype)
    return jnp.concatenate([a, c, zeros, b, d, zeros], axis=-1)


def _pair_layout_heads(w, n_heads):
    w = w.reshape(w.shape[:-1] + (n_heads, HEAD_DIM))
    return _pair_layout(w).reshape(w.shape[:-2] + (n_heads * HEAD_DIM,))


def _layout_w_in(w_in):
    sizes = (WIN_HEADS * HEAD_DIM, WIN_KV_HEADS * HEAD_DIM, WIN_KV_HEADS * HEAD_DIM,
             MLA_Q_RANK, MLA_KV_RANK, MLA_ROPE,
             GLB_HEADS * HEAD_DIM, GLB_KV_HEADS * HEAD_DIM, GLB_KV_HEADS * HEAD_DIM)
    offs = np.concatenate([[0], np.cumsum(sizes)])
    wq, wk, wv, mq, mkv, mkr, gq, gk, gv = (w_in[..., int(offs[k]):int(offs[k + 1])] for k in range(9))
    gates = w_in[..., int(offs[9]):]
    wq, gq = _pair_layout_heads(wq, WIN_HEADS), _pair_layout_heads(gq, GLB_HEADS)
    wk, gk = _pair_layout_heads(wk, WIN_KV_HEADS), _pair_layout_heads(gk, GLB_KV_HEADS)
    pad = jnp.zeros(w_in.shape[:-1] + (C_GATE - C_MKR - LANES,), w_in.dtype)
    return jnp.concatenate([wq, gq, wk, wv, gk, gv, mq, mkv, _pair_layout(mkr), pad, gates],
                           axis=-1).astype(BF16)


def _mla_head_layout(w):
    w = w.reshape(w.shape[:-1] + (MLA_HEADS, MLA_NOPE + MLA_ROPE))
    w = jnp.concatenate([w[..., :MLA_NOPE], _pair_layout(w[..., MLA_NOPE:])], axis=-1)
    return w.reshape(w.shape[:-2] + (MLA_HEADS * MLA_HEAD_PAD,))


def kernel(x, c, ctx, c_ctx, w_mod, b_mod, w_in, win_q_norm, win_k_norm, win_sink, mla_q_lat_norm,
           mla_kv_lat_norm, mla_w_q_up, mla_w_kv_up, mla_q_norm, mla_k_norm, glb_q_norm, glb_k_norm,
           w_branch, w_out, ffn_w_gate_up, ffn_w_down, moe_router, moe_w_gate_up, moe_w_down):
    n_batch, seq, d = x.shape
    depth = w_mod.shape[0]
    tpb = CTX_LEN + seq
    assert ctx.shape[1] == CTX_LEN and seq % GRID_W == 0 and seq % BLOCK == 0 and seq >= 3 * BLOCK
    assert n_batch <= MOD_ROWS - 8 and C_MKR + LANES <= C_GATE

    xs = jnp.concatenate([ctx, x], axis=1).reshape(n_batch * tpb, d)
    cvec = jnp.zeros((MOD_ROWS, d), F32).at[:n_batch].set(c).at[MOD_ROWS - 8].set(c_ctx)
    mod = _mod_call(cvec, w_mod, b_mod)

    rope_h = _rope_tables(seq, HEAD_DIM)
    rope_r = _rope_tables(seq, MLA_ROPE)
    w_in_p = _layout_w_in(w_in)
    wq_up = _mla_head_layout(mla_w_q_up).astype(BF16)
    wkv_up = mla_w_kv_up.astype(BF16)
    mla_gain = lambda g: jnp.concatenate([g[:MLA_NOPE], _pair_layout(g[MLA_NOPE:])])[None]
    gqa_gain = lambda g: _pair_layout(g)[None]
    w_branch_b, w_out_b = w_branch.astype(BF16), w_out.astype(BF16)
    ffn_gu, ffn_dn = ffn_w_gate_up.astype(BF16), ffn_w_down.astype(BF16)
    moe_gu, moe_dn = moe_w_gate_up.astype(BF16), moe_w_down.astype(BF16)

    latent_only = False
    for layer in range(depth):
        proj = _inproj_call(xs, mod, w_in_p, layer, tpb)
        q_win, k_win, q_glb, k_glb = _gqa_prep_call(
            proj, [gqa_gain(g[layer]) for g in (win_q_norm, win_k_norm, glb_q_norm, glb_k_norm)], rope_h, tpb)
        q_mla, k_mla, v_mla = _mla_prep_call(
            proj, wq_up, wkv_up, mla_q_lat_norm[layer][None], mla_kv_lat_norm[layer][None],
            mla_gain(mla_q_norm[layer]), mla_gain(mla_k_norm[layer]), rope_r, layer, tpb)
        o_win = _attention_call("win", (q_win, 0), (k_win, 0), (proj, C_WV), dict(sink=win_sink[layer]),
                                n_batch, tpb, False)
        o_mla = _attention_call("mla", (q_mla, 0), (k_mla, 0), (v_mla, 0), {}, n_batch, tpb, False)
        o_glb = _attention_call("glb", (q_glb, 0), (k_glb, 0), (proj, C_GV), {}, n_batch, tpb, False)
        xs = _merge_call(xs, o_win, o_mla, o_glb, proj, mod, w_branch_b, w_out_b, layer, tpb)
        if layer % 2 == 0:
            xs = _ffn_call(xs, mod, ffn_gu, ffn_dn, layer // 2, layer, tpb)
        else:
            skip = 1 if layer == depth - 1 else 0
            h, route = _router_call(xs, mod, moe_router[layer // 2], layer, tpb, skip)
            tile = 768 if 2 * h.shape[0] >= 16 * 768 else 128
            dest, tile_expert, n_active, pad_lo, pad_hi, n_tiles = _moe_plan(route, tile)
            rows = _dispatch_call(h, dest, pad_lo, pad_hi, n_tiles * tile)
            y = _moe_group_call(rows, tile_expert, n_active, moe_gu, moe_dn, layer // 2, tile)
            xs = _combine_call(xs, route, mod, y, dest, layer, tpb, skip)
            latent_only = bool(skip)
    if latent_only:
        return xs.reshape(n_batch, seq, d)
    return xs.reshape(n_batch, tpb, d)[:, CTX_LEN:, :]
```
